```python
import functools
import jax, jax.numpy as jnp
from jax import lax
import numpy as np

D_MODEL = 1024
BATCH = 4
SEQ = 4096
DEPTH = 2
DEC_BATCH = 128
DEC_SEQ = 8
PAST_LEN = 16384
PAGE_SIZE = 128

HEAD_DIM = 64
SCALE = HEAD_DIM ** -0.5
BLOCK = 128
STEPS = 128
POOL_WINDOWS = (2, 4, 8, 16)
POOL_GROUPS = len(POOL_WINDOWS)
POOL_CH = 128
POOL_WIDTH = POOL_GROUPS * POOL_CH
POOL_OUT = D_MODEL // POOL_GROUPS
POOL_STATE = max(POOL_WINDOWS) - 1
DIL_GROUPS = ((STEPS * 1, 1), (STEPS * 4, 4), (STEPS * 16, 16))
N_DIL = len(DIL_GROUPS)
DIL_HEADS = 4
DIL_WIDTH = N_DIL * DIL_HEADS * HEAD_DIM
DIL_OUT = DIL_HEADS * HEAD_DIM
SWA_WINDOW = STEPS
SWA_HEADS = 8
SWA_KV_HEADS = 2
SWA_GROUP = SWA_HEADS // SWA_KV_HEADS
N_BUCKETS = 32
MAX_DISTANCE = 2048
N_ATTN_HEADS = N_DIL * DIL_HEADS + SWA_HEADS
N_EXPERTS = 64
TOP_K = 8
EXPERT_FF = 256
SHARED_FF = 256
ROUTED_SCALE = 2.5
MOE_BLOCK = 128
EPS = 1e-6
IN_SPLITS = (POOL_WIDTH, DIL_WIDTH, DIL_WIDTH, DIL_WIDTH, SWA_HEADS * HEAD_DIM,
             SWA_KV_HEADS * HEAD_DIM, SWA_KV_HEADS * HEAD_DIM, 3 * D_MODEL)
IN_WIDTH = sum(IN_SPLITS)

kernel_name = 'hybrid_pool_dilated_swa_moe_step'


def t5_bucket(dist):
    n = np.asarray(dist, dtype=np.int64)
    exact = N_BUCKETS // 2
    log_ratio = np.log(np.maximum(n, 1) / exact) / np.log(MAX_DISTANCE / exact)
    large = np.minimum(exact + (log_ratio * (N_BUCKETS - exact)).astype(np.int64), N_BUCKETS - 1)
    return np.where(n < exact, n, large).astype(np.int32)


def step_bias(rel_bias, dil, h0, h1):
    buckets = t5_bucket(np.arange(STEPS + 1) * dil)
    return jnp.take(rel_bias, buckets, axis=0)[:, h0:h1].T.astype(jnp.float32)


def rms_norm(x, g):
    xf = x.astype(jnp.float32)
    y = xf * lax.rsqrt(jnp.mean(xf * xf, axis=-1, keepdims=True) + EPS)
    return (y * g.astype(jnp.float32)).astype(x.dtype)


def softmax_stats(s, mask, sink=None):
    s = jnp.where(mask, s, -jnp.inf)
    m = jnp.max(s, axis=-1, keepdims=True)
    if sink is not None:
        m = jnp.maximum(m, sink)
    p = jnp.exp(s - m)
    den = jnp.sum(p, axis=-1, keepdims=True)
    if sink is not None:
        den = den + jnp.exp(sink - m)
    return p / den, (m + jnp.log(den))[..., 0]


def band_attn(q, k, v, bias_s, steps, sink=None):
    bt, seq, n_h, hd = q.shape
    n_kv = k.shape[2]
    grp = n_h // n_kv
    nb = -(-seq // BLOCK)
    pad = nb * BLOCK - seq
    q, k, v = [jnp.pad(t, ((0, 0), (0, pad), (0, 0), (0, 0))) for t in (q, k, v)]
    qb = q.reshape(bt, nb, BLOCK, n_kv, grp, hd)

    def with_prev(t):
        tb = t.reshape(bt, nb, BLOCK, n_kv, hd)
        prev = jnp.pad(tb, ((0, 0), (1, 0), (0, 0), (0, 0), (0, 0)))[:, :nb]
        return jnp.concatenate([prev, tb], axis=2)

    kk, vv = with_prev(k), with_prev(v)
    dist = BLOCK + np.arange(BLOCK)[:, None] - np.arange(2 * BLOCK)[None, :]
    kpos = (np.arange(nb)[:, None, None] - 1) * BLOCK + np.arange(2 * BLOCK)[None, None, :]
    mask = (dist >= 0) & (dist <= steps) & (kpos >= 0)
    bias = bias_s[:, np.clip(dist, 0, steps)].reshape(n_kv, grp, BLOCK, 2 * BLOCK)
    s = jnp.einsum('bnqkgd,bnskd->bnkgqs', qb, kk, preferred_element_type=jnp.float32) * SCALE + bias
    p, lse = softmax_stats(s, mask[None, :, None, None], sink)
    o = jnp.einsum('bnkgqs,bnskd->bnqkgd', p.astype(vv.dtype), vv, preferred_element_type=jnp.float32)
    o = o.reshape(bt, nb * BLOCK, n_h, hd)[:, :seq].astype(q.dtype)
    lse = lse.transpose(0, 1, 4, 2, 3).reshape(bt, nb * BLOCK, n_h)[:, :seq]
    return o, lse


def dilated_prompt(q, k, v, dil, bias_s):
    bt, seq = q.shape[:2]

    def fold(t):
        return t.reshape(bt, seq // dil, dil, *t.shape[2:]).swapaxes(1, 2).reshape(bt * dil, seq // dil, *t.shape[2:])

    def unfold(t):
        return t.reshape(bt, dil, seq // dil, *t.shape[2:]).swapaxes(1, 2).reshape(bt, seq, *t.shape[2:])

    o, lse = band_attn(fold(q), fold(k), fold(v), bias_s, STEPS)
    return unfold(o), unfold(lse)


def gathered_attn(q, k_all, v_all, n_buf, dil, bias_s, steps, sink=None):
    bt, n_new, n_h, hd = q.shape
    n_kv = k_all.shape[2]
    grp = n_h // n_kv
    idx = n_buf + np.arange(n_new)[:, None] - dil * np.arange(steps + 1)[None, :]
    valid = idx >= 0
    idx = np.maximum(idx, 0)
    kg, vg = k_all[:, idx], v_all[:, idx]
    qg = q.reshape(bt, n_new, n_kv, grp, hd)
    s = jnp.einsum('btkgd,btjkd->btkgj', qg, kg, preferred_element_type=jnp.float32) * SCALE
    s = s + bias_s.reshape(n_kv, grp, steps + 1)
    p, lse = softmax_stats(s, valid[None, :, None, None, :], sink)
    o = jnp.einsum('btkgj,btjkd->btkgd', p.astype(vg.dtype), vg, preferred_element_type=jnp.float32)
    return o.reshape(bt, n_new, n_h, hd).astype(q.dtype), lse.reshape(bt, n_new, n_h)


def pool_branch(u_ext, n_past, pool_w, pool_scale):
    bt, n_ext, _ = u_ext.shape
    uf = u_ext.astype(jnp.float32)
    cs = jnp.concatenate([jnp.zeros((bt, 1, POOL_WIDTH), jnp.float32), jnp.cumsum(uf, axis=1)], axis=1)
    rows = np.arange(n_past, n_ext)
    parts = []
    for g, win in enumerate(POOL_WINDOWS):
        ch = slice(g * POOL_CH, (g + 1) * POOL_CH)
        lo = np.maximum(rows + 1 - win, 0)
        cnt = (rows + 1 - lo).astype(np.float32)[:, None]
        mean = (cs[:, n_past + 1:, ch] - cs[:, lo, ch]) / cnt
        parts.append(mean - uf[:, n_past:, ch])
    z = jnp.stack(parts, axis=2).astype(pool_w.dtype)
    y = jnp.einsum('blgc,gco->blgo', z, pool_w).reshape(bt, rows.size, D_MODEL)
    return y * pool_scale


def project(h, w_in):
    bt, n, _ = h.shape
    z = jnp.einsum('bld,de->ble', h, w_in)
    cuts = [int(i) for i in np.cumsum(IN_SPLITS)[:-1]]
    u, qb, kb, vb, qc, kc, vc, gl = jnp.split(z, cuts, axis=-1)
    dshape = (bt, n, N_DIL, DIL_HEADS, HEAD_DIM)
    return (u, qb.reshape(dshape), kb.reshape(dshape), vb.reshape(dshape),
            qc.reshape(bt, n, SWA_HEADS, HEAD_DIM), kc.reshape(bt, n, SWA_KV_HEADS, HEAD_DIM),
            vc.reshape(bt, n, SWA_KV_HEADS, HEAD_DIM), gl)


def merge(a, ob, lse, oc, gl, w_b_up, w_c_up, w_out):
    bt, n = a.shape[:2]
    alpha = jax.nn.softmax(jnp.stack(lse, axis=0), axis=0)[..., None]
    b = jnp.sum(alpha * jnp.stack(ob, axis=0).astype(jnp.float32), axis=0)
    b = b.reshape(bt, n, DIL_OUT).astype(w_b_up.dtype) @ w_b_up
    c = oc.reshape(bt, n, SWA_HEADS * HEAD_DIM) @ w_c_up
    g_a, g_b, g_c = jnp.split(jax.nn.sigmoid(gl), 3, axis=-1)
    return (g_a * a + g_b * b + g_c * c) @ w_out


def mixer_prompt(h, bias_b, bias_c, mix_w):
    w_in, pool_w, pool_scale, w_b_up, w_c_up, sinks, w_out = mix_w
    seq = h.shape[1]
    u, qb, kb, vb, qc, kc, vc, gl = project(h, w_in)
    a = pool_branch(u, 0, pool_w, pool_scale)
    ob, lse, kv_b = [], [], []
    for g, (win, dil) in enumerate(DIL_GROUPS):
        o_g, l_g = dilated_prompt(qb[:, :, g], kb[:, :, g], vb[:, :, g], dil, bias_b[g])
        ob.append(o_g)
        lse.append(l_g)
        keep = min(win, seq)
        kv_b.append(jnp.stack([kb[:, seq - keep:, g], vb[:, seq - keep:, g]], axis=2))
    sink = sinks.astype(jnp.float32).reshape(SWA_KV_HEADS, SWA_GROUP, 1, 1)
    oc, _ = band_attn(qc, kc, vc, bias_c, SWA_WINDOW, sink)
    keep = min(SWA_WINDOW, seq)
    kv_c = jnp.stack([kc[:, seq - keep:], vc[:, seq - keep:]], axis=2)
    y = merge(a, ob, lse, oc, gl, w_b_up, w_c_up, w_out)
    return y, (kv_b[0], kv_b[1], kv_b[2], kv_c, u[:, seq - POOL_STATE:])


def mixer_sample(h, caches, bias_b, bias_c, mix_w):
    w_in, pool_w, pool_scale, w_b_up, w_c_up, sinks, w_out = mix_w
    u, qb, kb, vb, qc, kc, vc, gl = project(h, w_in)
    pool_buf = caches[4]
    u_ext = jnp.concatenate([pool_buf.astype(u.dtype), u], axis=1)
    a = pool_branch(u_ext, pool_buf.shape[1], pool_w, pool_scale)
    ob, lse, kv_b = [], [], []
    for g, (win, dil) in enumerate(DIL_GROUPS):
        buf = caches[g].astype(kb.dtype)
        n_buf = buf.shape[1]
        k_all = jnp.concatenate([buf[:, :, 0], kb[:, :, g]], axis=1)
        v_all = jnp.concatenate([buf[:, :, 1], vb[:, :, g]], axis=1)
        o_g, l_g = gathered_attn(qb[:, :, g], k_all, v_all, n_buf, dil, bias_b[g], STEPS)
        ob.append(o_g)
        lse.append(l_g)
        kv_b.append(jnp.stack([k_all[:, -n_buf:], v_all[:, -n_buf:]], axis=2))
    buf = caches[3].astype(kc.dtype)
    n_buf = buf.shape[1]
    k_all = jnp.concatenate([buf[:, :, 0], kc], axis=1)
    v_all = jnp.concatenate([buf[:, :, 1], vc], axis=1)
    sink = sinks.astype(jnp.float32).reshape(SWA_KV_HEADS, SWA_GROUP, 1)
    oc, _ = gathered_attn(qc, k_all, v_all, n_buf, 1, bias_c, SWA_WINDOW, sink)
    kv_c = jnp.stack([k_all[:, -n_buf:], v_all[:, -n_buf:]], axis=2)
    y = merge(a, ob, lse, oc, gl, w_b_up, w_c_up, w_out)
    return y, (kv_b[0], kv_b[1], kv_b[2], kv_c, u_ext[:, -POOL_STATE:])


def swiglu(x, wg, wu, wd):
    return (jax.nn.silu(x @ wg) * (x @ wu)) @ wd


def routed_experts(xf, top_idx, top_w, we_gate, we_up, we_down):
    n_tok, d = xf.shape
    n_asg = n_tok * TOP_K
    e_flat = top_idx.reshape(n_asg)
    order = jnp.argsort(e_flat)
    e_sorted = e_flat[order]
    tok_sorted = (order // TOP_K).astype(jnp.int32)
    w_sorted = top_w.reshape(n_asg)[order].astype(xf.dtype)
    counts = jnp.bincount(e_flat, length=N_EXPERTS)
    padded = (counts + MOE_BLOCK - 1) // MOE_BLOCK * MOE_BLOCK
    pad_end = jnp.cumsum(padded)
    pad_start = pad_end - padded
    start = jnp.cumsum(counts) - counts
    dest = pad_start[e_sorted] + jnp.arange(n_asg, dtype=jnp.int32) - start[e_sorted]
    n_blocks = -(-(n_asg + N_EXPERTS * (MOE_BLOCK - 1)) // MOE_BLOCK)
    n_rows = n_blocks * MOE_BLOCK
    row_tok = jnp.full((n_rows,), n_tok, jnp.int32).at[dest].set(tok_sorted)
    row_w = jnp.zeros((n_rows,), xf.dtype).at[dest].set(w_sorted)
    block_exp = jnp.minimum(jnp.searchsorted(pad_end, jnp.arange(n_blocks) * MOE_BLOCK, side='right'), N_EXPERTS - 1)
    x_rows = jnp.concatenate([xf, jnp.zeros((1, d), xf.dtype)], axis=0)[row_tok].reshape(n_blocks, MOE_BLOCK, d)

    def one_block(args):
        xb, e = args
        return swiglu(xb, we_gate[e], we_up[e], we_down[e])

    y_rows = lax.map(one_block, (x_rows, block_exp)).reshape(n_rows, d)
    return jax.ops.segment_sum(y_rows * row_w[:, None], row_tok, num_segments=n_tok + 1)[:n_tok]


def moe(h, w_router, router_bias, we_gate, we_up, we_down, ws_gate, ws_up, ws_down):
    bt, n, d = h.shape
    xf = h.reshape(bt * n, d)
    scores = jax.nn.sigmoid(jnp.einsum('nd,de->ne', xf, w_router, preferred_element_type=jnp.float32))
    _, top_idx = lax.top_k(scores + router_bias.astype(jnp.float32), TOP_K)
    top_s = jnp.take_along_axis(scores, top_idx, axis=1)
    top_w = top_s / jnp.sum(top_s, axis=-1, keepdims=True) * ROUTED_SCALE
    y = routed_experts(xf, top_idx, top_w, we_gate, we_up, we_down) + swiglu(xf, ws_gate, ws_up, ws_down)
    return y.reshape(bt, n, d)


def sublayer_pair(x, c, mixer, w_ada, b_ada, g1, g2, g3, g4, moe_w):
    ada = jnp.einsum('bd,de->be', jax.nn.silu(c), w_ada) + b_ada
    sh_m, sc_m, gt_m, sh_f, sc_f, gt_f = jnp.split(ada[:, None, :], 6, axis=-1)
    m, new_state = mixer(rms_norm(x, g1) * (1 + sc_m) + sh_m)
    x = x + gt_m * rms_norm(m, g2)
    f = moe(rms_norm(x, g3) * (1 + sc_f) + sh_f, *moe_w)
    x = x + gt_f * rms_norm(f, g4)
    return x, new_state


def setup_inputs(seed: int = 0) -> dict:
    key = jax.random.key(seed)
    ks = iter(jax.random.split(key, 40))

    def nrm(shape, scale):
        return jax.random.normal(next(ks), shape, jnp.float32) * scale

    D = D_MODEL
    return {
        'x_prompt': nrm((BATCH, SEQ, D), 1.0),
        'x_sample': nrm((DEC_BATCH, DEC_SEQ, D), 1.0),
        'cache_b1': nrm((DEPTH, DEC_BATCH, min(DIL_GROUPS[0][0], PAST_LEN), 2, DIL_HEADS, HEAD_DIM), 1.0),
        'cache_b2': nrm((DEPTH, DEC_BATCH, min(DIL_GROUPS[1][0], PAST_LEN), 2, DIL_HEADS, HEAD_DIM), 1.0),
        'cache_b3': nrm((DEPTH, DEC_BATCH, min(DIL_GROUPS[2][0], PAST_LEN), 2, DIL_HEADS, HEAD_DIM), 1.0),
        'cache_c': nrm((DEPTH, DEC_BATCH, min(SWA_WINDOW, PAST_LEN), 2, SWA_KV_HEADS, HEAD_DIM), 1.0),
        'state_pool': nrm((DEPTH, DEC_BATCH, POOL_STATE, POOL_WIDTH), 1.0),
        'c_prompt': nrm((BATCH, D), 1.0),
        'c_sample': nrm((DEC_BATCH, D), 1.0),
        'rel_bias': nrm((N_BUCKETS, N_ATTN_HEADS), 0.5),
        'w_ada': nrm((DEPTH, D, 6 * D), 0.5 * D ** -0.5),
        'b_ada': nrm((DEPTH, 6 * D), 0.02),
        'g_pre_mix': 1.0 + nrm((DEPTH, D), 0.05),
        'g_post_mix': 1.0 + nrm((DEPTH, D), 0.05),
        'g_pre_ffn': 1.0 + nrm((DEPTH, D), 0.05),
        'g_post_ffn': 1.0 + nrm((DEPTH, D), 0.05),
        'w_in': nrm((DEPTH, D, IN_WIDTH), D ** -0.5),
        'pool_w': nrm((DEPTH, POOL_GROUPS, POOL_CH, POOL_OUT), POOL_CH ** -0.5),
        'pool_scale': 1.0 + nrm((DEPTH, D), 0.1),
        'w_b_up': nrm((DEPTH, DIL_OUT, D), DIL_OUT ** -0.5),
        'w_c_up': nrm((DEPTH, SWA_HEADS * HEAD_DIM, D), (SWA_HEADS * HEAD_DIM) ** -0.5),
        'sinks': nrm((DEPTH, SWA_HEADS), 1.0),
        'w_out': nrm((DEPTH, D, D), D ** -0.5),
        'w_router': nrm((DEPTH, D, N_EXPERTS), D ** -0.5),
        'router_bias': nrm((DEPTH, N_EXPERTS), 0.01),
        'we_gate': nrm((DEPTH, N_EXPERTS, D, EXPERT_FF), D ** -0.5),
        'we_up': nrm((DEPTH, N_EXPERTS, D, EXPERT_FF), D ** -0.5),
        'we_down': nrm((DEPTH, N_EXPERTS, EXPERT_FF, D), EXPERT_FF ** -0.5),
        'ws_gate': nrm((DEPTH, D, SHARED_FF), D ** -0.5),
        'ws_up': nrm((DEPTH, D, SHARED_FF), D ** -0.5),
        'ws_down': nrm((DEPTH, SHARED_FF, D), SHARED_FF ** -0.5),
    }


def reference(x_prompt, x_sample, cache_b1, cache_b2, cache_b3, cache_c, state_pool, c_prompt, c_sample,
              rel_bias, w_ada, b_ada, g_pre_mix, g_post_mix, g_pre_ffn, g_post_ffn, w_in, pool_w, pool_scale,
              w_b_up, w_c_up, sinks, w_out, w_router, router_bias, we_gate, we_up, we_down,
              ws_gate, ws_up, ws_down):
    bias_b = [step_bias(rel_bias, dil, g * DIL_HEADS, (g + 1) * DIL_HEADS) for g, (_, dil) in enumerate(DIL_GROUPS)]
    bias_c = step_bias(rel_bias, 1, N_DIL * DIL_HEADS, N_ATTN_HEADS)
    xp, xs = x_prompt, x_sample
    states_p, states_s = [], []
    for l in range(DEPTH):
        mix_w = (w_in[l], pool_w[l], pool_scale[l], w_b_up[l], w_c_up[l], sinks[l], w_out[l])
        moe_w = (w_router[l], router_bias[l], we_gate[l], we_up[l], we_down[l], ws_gate[l], ws_up[l], ws_down[l])
        layer_w = (w_ada[l], b_ada[l], g_pre_mix[l], g_post_mix[l], g_pre_ffn[l], g_post_ffn[l], moe_w)
        mix_p = functools.partial(mixer_prompt, bias_b=bias_b, bias_c=bias_c, mix_w=mix_w)
        mix_s = functools.partial(mixer_sample,
                                  caches=(cache_b1[l], cache_b2[l], cache_b3[l], cache_c[l], state_pool[l]),
                                  bias_b=bias_b, bias_c=bias_c, mix_w=mix_w)
        xp, st_p = sublayer_pair(xp, c_prompt, mix_p, *layer_w)
        xs, st_s = sublayer_pair(xs, c_sample, mix_s, *layer_w)
        states_p.append(st_p)
        states_s.append(st_s)
    b1_p, b2_p, b3_p, c_p, pool_p = [jnp.stack(s, axis=0) for s in zip(*states_p)]
    b1_s, b2_s, b3_s, c_s, pool_s = [jnp.stack(s, axis=0) for s in zip(*states_s)]
    return (xp, xs, b1_p, b2_p, b3_p, c_p, pool_p, b1_s, b2_s, b3_s, c_s, pool_s)
```

```python
import functools

import numpy as np
import jax
import jax.numpy as jnp
from jax import lax
from jax.experimental import pallas as pl
from jax.experimental.pallas import tpu as pltpu

F32 = jnp.float32
BF16 = jnp.bfloat16

HEAD_DIM = 64
SCALE = HEAD_DIM ** -0.5
BLOCK = 128
POOL_WINDOWS = (2, 4, 8, 16)
POOL_CH = 128
POOL_WIDTH = len(POOL_WINDOWS) * POOL_CH
POOL_HALO = 16
DIL_GROUPS = ((128, 1), (512, 4), (2048, 16))
DIL_HEADS = 4
DIL_W = DIL_HEADS * HEAD_DIM
SWA_HEADS = 8
SWA_KV_HEADS = 2
SWA_QW = SWA_HEADS * HEAD_DIM
SWA_KW = SWA_KV_HEADS * HEAD_DIM
N_BUCKETS = 32
MAX_DISTANCE = 2048
N_EXPERTS = 64
TOP_K = 8
ROUTED_SCALE = 2.5
EPS = 1e-6
NEG_INF = float("-inf")

V7X_VMEM_BYTES = 64 * 1024 * 1024
VMEM_LIMIT = 56 * 1024 * 1024


def _cparams(sem):
    return pltpu.CompilerParams(dimension_semantics=sem, vmem_limit_bytes=VMEM_LIMIT)


def _rms(x, g):
    ms = jnp.mean(x * x, axis=-1, keepdims=True)
    return x * lax.rsqrt(ms + EPS) * g


def _sigmoid(x):
    return 1.0 / (1.0 + jnp.exp(-x))


def _silu(x):
    return x * _sigmoid(x)


def _dot(a, b):
    return jnp.dot(a, b, preferred_element_type=F32)


def _dot_nt(a, b):
    return lax.dot_general(a, b, (((1,), (1,)), ((), ())), preferred_element_type=F32)


def _ada_kernel(c_ref, w_ref, b_ref, o_ref):
    c = _silu(c_ref[...]).astype(BF16)
    o_ref[0] = _dot(c, w_ref[0].astype(BF16)) + b_ref[0]


def _ada_all(c_all, w_ada, b_ada):
    depth, d, n = w_ada.shape
    rows = c_all.shape[0]
    tn = 1536
    return pl.pallas_call(
        _ada_kernel,
        grid=(depth, n // tn),
        in_specs=[
            pl.BlockSpec((rows, d), lambda l, j: (0, 0)),
            pl.BlockSpec((1, d, tn), lambda l, j: (l, 0, j)),
            pl.BlockSpec((1, 1, tn), lambda l, j: (l, 0, j)),
        ],
        out_specs=pl.BlockSpec((1, rows, tn), lambda l, j: (l, 0, j)),
        out_shape=jax.ShapeDtypeStruct((depth, rows, n), F32),
        compiler_params=_cparams(("parallel", "parallel")),
        name="ada",
    )(c_all, w_ada, b_ada.reshape(depth, 1, n))


def _in_proj_kernel(x_ref, sh_ref, sc_ref, g_ref, w_ref,
                    u_ref, q1_ref, kv1_ref, q2_ref, kv2_ref, q3_ref, kv3_ref, qc_ref, kvc_ref, gl_ref,
                    st1_ref, st2_ref, st3_ref, stc_ref):
    s, t, d = x_ref.shape
    rows = s * t
    h = _rms(x_ref[...], g_ref[...]) * (1.0 + sc_ref[...]) + sh_ref[...]
    hb = h.reshape(rows, d).astype(BF16)

    def proj(c0, c1):
        return _dot(hb, w_ref[:, c0:c1])

    u_ref[...] = proj(0, POOL_WIDTH).reshape(s, t, POOL_WIDTH)
    off = POOL_WIDTH
    for q_ref, kv_ref, st_ref in ((q1_ref, kv1_ref, st1_ref), (q2_ref, kv2_ref, st2_ref), (q3_ref, kv3_ref, st3_ref)):
        z = proj(off, off + 3 * DIL_W)
        q_ref[...] = (z[:, :DIL_W] * SCALE).astype(BF16)
        kv_ref[...] = z[:, DIL_W:].astype(BF16)
        st_ref[...] = z[rows - st_ref.shape[0]:, DIL_W:]
        off += 3 * DIL_W
    z = proj(off, off + SWA_QW + 2 * SWA_KW)
    qc_ref[...] = (z[:, :SWA_QW] * SCALE).astype(BF16)
    kvc_ref[...] = z[:, SWA_QW:].astype(BF16)
    stc_ref[...] = z[rows - stc_ref.shape[0]:, SWA_QW:]
    off += SWA_QW + 2 * SWA_KW
    gl_ref[...] = proj(off, off + 3 * d)


def _in_proj(x3, ada3, g_pre, w_perm, layer, tm, keeps):
    nseq, tseq, d = x3.shape
    ntok = nseq * tseq
    if tseq >= tm:
        s_blk, t_blk = 1, tm
    else:
        s_blk, t_blk = tm // tseq, tseq
    tps = tseq // t_blk
    n_tiles = ntok // tm
    in_w = w_perm.shape[-1]

    def x_map(i):
        return (i // tps, i % tps, 0)

    def tok_map(i):
        return (i, 0)

    widths = (2 * DIL_W, 2 * DIL_W, 2 * DIL_W, 2 * SWA_KW)
    st_specs, st_shapes = [], []
    for keep, w in zip(keeps, widths):
        keep = min(keep, tseq)
        if keep == tseq and tseq < tm:
            st_shapes.append(jax.ShapeDtypeStruct((ntok, w), F32))
            st_specs.append(pl.BlockSpec((tm, w), tok_map))
            continue
        sb = min(tm, keep)
        bps = keep // sb
        first = (tseq - keep) // tm

        def st_map(i, bps=bps, first=first):
            return ((i // tps) * bps + jnp.maximum(i % tps - first, 0) * (1 if bps > 1 else 0), 0)

        st_shapes.append(jax.ShapeDtypeStruct((nseq * keep, w), F32))
        st_specs.append(pl.BlockSpec((sb, w), st_map))

    out_shapes = [
        jax.ShapeDtypeStruct((nseq, tseq, POOL_WIDTH), F32),
        jax.ShapeDtypeStruct((ntok, DIL_W), BF16), jax.ShapeDtypeStruct((ntok, 2 * DIL_W), BF16),
        jax.ShapeDtypeStruct((ntok, DIL_W), BF16), jax.ShapeDtypeStruct((ntok, 2 * DIL_W), BF16),
        jax.ShapeDtypeStruct((ntok, DIL_W), BF16), jax.ShapeDtypeStruct((ntok, 2 * DIL_W), BF16),
        jax.ShapeDtypeStruct((ntok, SWA_QW), BF16), jax.ShapeDtypeStruct((ntok, 2 * SWA_KW), BF16),
        jax.ShapeDtypeStruct((ntok, 3 * d), F32),
    ] + st_shapes
    out_specs = [
        pl.BlockSpec((s_blk, t_blk, POOL_WIDTH), x_map),
        pl.BlockSpec((tm, DIL_W), tok_map), pl.BlockSpec((tm, 2 * DIL_W), tok_map),
        pl.BlockSpec((tm, DIL_W), tok_map), pl.BlockSpec((tm, 2 * DIL_W), tok_map),
        pl.BlockSpec((tm, DIL_W), tok_map), pl.BlockSpec((tm, 2 * DIL_W), tok_map),
        pl.BlockSpec((tm, SWA_QW), tok_map), pl.BlockSpec((tm, 2 * SWA_KW), tok_map),
        pl.BlockSpec((tm, 3 * d), tok_map),
    ] + st_specs
    return pl.pallas_call(
        _in_proj_kernel,
        grid=(n_tiles,),
        in_specs=[
            pl.BlockSpec((s_blk, t_blk, d), x_map),
            pl.BlockSpec((s_blk, 1, d), lambda i: (i // tps, 0, 0)),
            pl.BlockSpec((s_blk, 1, d), lambda i: (i // tps, 0, 1)),
            pl.BlockSpec((1, 1, d), lambda i: (layer, 0, 0)),
            pl.BlockSpec((d, in_w), lambda i: (0, 0)),
        ],
        out_specs=out_specs,
        out_shape=out_shapes,
        compiler_params=_cparams(("arbitrary",)),
        name="in_proj",
    )(x3, ada3, ada3, g_pre, w_perm)


def _band_attn_kernel(*refs, n_heads, n_kv, nq, with_sink):
    if with_sink:
        sink_ref, q_ref, kv_ref, halo_ref, bm_ref, o_ref = refs
        lse_ref = None
    else:
        q_ref, kv_ref, halo_ref, bm_ref, o_ref, lse_ref = refs
        sink_ref = None
    kw = n_kv * HEAD_DIM
    grp = n_heads // n_kv
    chunk = pl.program_id(2)

    def one_block(r0, kv_prev, first):
        qb = q_ref[0, pl.ds(r0, BLOCK), :]
        kv_cur = kv_ref[0, pl.ds(r0, BLOCK), :]
        for h in range(n_heads):
            kh = h // grp
            ks = slice(kh * HEAD_DIM, (kh + 1) * HEAD_DIM)
            vs = slice(kw + kh * HEAD_DIM, kw + (kh + 1) * HEAD_DIM)
            hs = slice(h * HEAD_DIM, (h + 1) * HEAD_DIM)
            qh = qb[:, hs]
            s_p = _dot_nt(qh, kv_prev[:, ks]) + bm_ref[h, :, :BLOCK]
            s_c = _dot_nt(qh, kv_cur[:, ks]) + bm_ref[h, :, BLOCK:]
            if first is not None:
                s_p = jnp.where(first, NEG_INF, s_p)
            m = jnp.maximum(jnp.max(s_p, axis=-1, keepdims=True), jnp.max(s_c, axis=-1, keepdims=True))
            if with_sink:
                m = jnp.maximum(m, sink_ref[h])
            p_p = jnp.exp(s_p - m)
            p_c = jnp.exp(s_c - m)
            den = jnp.sum(p_p, axis=-1, keepdims=True) + jnp.sum(p_c, axis=-1, keepdims=True)
            if with_sink:
                den = den + jnp.exp(sink_ref[h] - m)
            o = _dot(p_p.astype(BF16), kv_prev[:, vs]) + _dot(p_c.astype(BF16), kv_cur[:, vs])
            o_ref[0, pl.ds(r0, BLOCK), hs] = o / den
            if lse_ref is not None:
                lse_ref[0, pl.ds(r0, BLOCK), hs] = jnp.broadcast_to(m + jnp.log(den), (BLOCK, HEAD_DIM))

    one_block(0, halo_ref[0], chunk == 0)

    def body(j, carry):
        r0 = pl.multiple_of(j * BLOCK, BLOCK)
        one_block(r0, kv_ref[0, pl.ds(r0 - BLOCK, BLOCK), :], None)
        return carry

    if nq > 1:
        lax.fori_loop(1, nq, body, 0)


def _band_attn(q, kv, bm, batch, seq, dil, n_heads, n_kv, sinks=None):
    qw = n_heads * HEAD_DIM
    kvw = 2 * n_kv * HEAD_DIM
    fold = seq // dil
    nq = min(8, fold // BLOCK)
    rows = nq * BLOCK
    n_chunks = fold // rows
    qf = q.reshape(batch, fold, dil * qw)
    kvf = kv.reshape(batch, fold, dil * kvw)
    with_sink = sinks is not None
    in_specs = [
        pl.BlockSpec((1, rows, qw), lambda b, r, c: (b, c, r)),
        pl.BlockSpec((1, rows, kvw), lambda b, r, c: (b, c, r)),
        pl.BlockSpec((1, BLOCK, kvw), lambda b, r, c: (b, jnp.maximum(c * nq - 1, 0), r)),
        pl.BlockSpec((n_heads, BLOCK, 2 * BLOCK), lambda b, r, c: (0, 0, 0)),
    ]
    args = [qf, kvf, kvf, bm]
    o_spec = pl.BlockSpec((1, rows, qw), lambda b, r, c: (b, c, r))
    o_shape = jax.ShapeDtypeStruct((batch, fold, dil * qw), F32)
    if with_sink:
        in_specs = [pl.BlockSpec(memory_space=pltpu.SMEM)] + in_specs
        args = [sinks] + args
        out_specs, out_shape = o_spec, o_shape
    else:
        out_specs, out_shape = [o_spec, o_spec], [o_shape, o_shape]
    res = pl.pallas_call(
        functools.partial(_band_attn_kernel, n_heads=n_heads, n_kv=n_kv, nq=nq, with_sink=with_sink),
        grid=(batch, dil, n_chunks),
        in_specs=in_specs,
        out_specs=out_specs,
        out_shape=out_shape,
        compiler_params=_cparams(("parallel", "parallel", "parallel")),
        name="band_attn",
    )(*args)
    if with_sink:
        return res.reshape(batch * seq, qw)
    return res[0].reshape(batch * seq, qw), res[1].reshape(batch * seq, qw)


def _sample_attn_kernel(*refs, n_heads, n_kv, sb, t_new, with_sink, aliased):
    refs = list(refs)
    sink_ref = refs.pop(0) if with_sink else None
    q_ref, new_ref, cache_ref, bmc_ref, bmn_ref = refs[:5]
    refs = refs[5:]
    if aliased:
        refs = refs[1:]
    o_ref = refs[0]
    lse_ref = None if with_sink else refs[1]
    cout_ref = refs[-1]
    kw = n_kv * HEAD_DIM
    grp = n_heads // n_kv
    n_buf = cache_ref.shape[2]
    qf = q_ref[...].astype(F32)
    for s in range(sb):
        rs = slice(s * t_new, (s + 1) * t_new)
        new = new_ref[rs, :]
        cache = cache_ref[0, s]
        cb = cache.astype(BF16)
        nb = new.astype(BF16)
        qs = qf[rs, :].astype(BF16)
        for h in range(n_heads):
            kh = h // grp
            ks = slice(kh * HEAD_DIM, (kh + 1) * HEAD_DIM)
            vs = slice(kw + kh * HEAD_DIM, kw + (kh + 1) * HEAD_DIM)
            hs = slice(h * HEAD_DIM, (h + 1) * HEAD_DIM)
            qh = qs[:, hs]
            s_c = _dot_nt(qh, cb[:, ks]) + bmc_ref[h]
            s_n = _dot_nt(qh, nb[:, ks]) + bmn_ref[h]
            m = jnp.maximum(jnp.max(s_c, axis=-1, keepdims=True), jnp.max(s_n, axis=-1, keepdims=True))
            if with_sink:
                m = jnp.maximum(m, sink_ref[h])
            p_c = jnp.exp(s_c - m)
            p_n = jnp.exp(s_n - m)
            den = jnp.sum(p_c, axis=-1, keepdims=True) + jnp.sum(p_n, axis=-1, keepdims=True)
            if with_sink:
                den = den + jnp.exp(sink_ref[h] - m)
            o = _dot(p_c.astype(BF16), cb[:, vs]) + _dot(p_n.astype(BF16), nb[:, vs])
            o_ref[rs, hs] = o / den
            if lse_ref is not None:
                lse_ref[rs, hs] = jnp.broadcast_to(m + jnp.log(den), (t_new, HEAD_DIM))
        cout_ref[0, s, : n_buf - t_new, :] = cache[t_new:, :]
        cout_ref[0, s, n_buf - t_new:, :] = new


def _sample_attn(q, new, cache, prev_out, bmc, bmn, layer, n_heads, n_kv, t_new, sinks=None):
    depth, nseq, n_buf, w = cache.shape
    qw = n_heads * HEAD_DIM
    sb = max(1, min(8, 256 // n_buf)) if n_buf < 256 else 1
    sb = max(sb, 16 // t_new)
    with_sink = sinks is not None
    aliased = prev_out is not None
    rows = sb * t_new
    in_specs = [
        pl.BlockSpec((rows, qw), lambda i: (i, 0)),
        pl.BlockSpec((rows, w), lambda i: (i, 0)),
        pl.BlockSpec((1, sb, n_buf, w), lambda i: (layer, i, 0, 0)),
        pl.BlockSpec(bmc.shape, lambda i: (0, 0, 0)),
        pl.BlockSpec(bmn.shape, lambda i: (0, 0, 0)),
    ]
    args = [q, new, cache, bmc, bmn]
    if with_sink:
        in_specs = [pl.BlockSpec(memory_space=pltpu.SMEM)] + in_specs
        args = [sinks] + args
    aliases = {}
    if aliased:
        aliases = {len(args): 1 if with_sink else 2}
        in_specs.append(pl.BlockSpec(memory_space=pl.ANY))
        args.append(prev_out)
    o_spec = pl.BlockSpec((rows, qw), lambda i: (i, 0))
    o_shape = jax.ShapeDtypeStruct((nseq * t_new, qw), F32)
    c_spec = pl.BlockSpec((1, sb, n_buf, w), lambda i: (layer, i, 0, 0))
    c_shape = jax.ShapeDtypeStruct(cache.shape, F32)
    if with_sink:
        out_specs, out_shape = [o_spec, c_spec], [o_shape, c_shape]
    else:
        out_specs, out_shape = [o_spec, o_spec, c_spec], [o_shape, o_shape, c_shape]
    return pl.pallas_call(
        functools.partial(_sample_attn_kernel, n_heads=n_heads, n_kv=n_kv, sb=sb, t_new=t_new,
                          with_sink=with_sink, aliased=aliased),
        grid=(nseq // sb,),
        in_specs=in_specs,
        out_specs=out_specs,
        out_shape=out_shape,
        input_output_aliases=aliases,
        compiler_params=_cparams(("parallel",)),
        name="sample_attn",
    )(*args)


def _post_kernel(x_ref, gt_ref, shf_ref, scf_ref, g2_ref, g3_ref, u_ref, halo_ref,
                 o1_ref, l1_ref, o2_ref, l2_ref, o3_ref, l3_ref, oc_ref, gl_ref,
                 pw_ref, ps_ref, wb_ref, wc_ref, wo_ref, wr_ref, rb_ref,
                 xmid_ref, h2_ref, wmat_ref, idx_ref, *, tps, full_windows):
    s, t, d = x_ref.shape
    rows = s * t
    i = pl.program_id(0)

    u = u_ref[...]
    halo = halo_ref[...]
    if not full_windows:
        halo = jnp.where(i % tps == 0, 0.0, halo)
    ue = jnp.concatenate([halo, u], axis=1)
    if full_windows:
        row = None
    else:
        row = (i % tps) * t + lax.broadcasted_iota(jnp.int32, (1, t, 1), 1)
    parts = []
    for g, win in enumerate(POOL_WINDOWS):
        cs = slice(g * POOL_CH, (g + 1) * POOL_CH)
        acc = ue[:, :, cs]
        base = 0
        span = 1
        while span < win:
            acc = acc[:, span:, :] + acc[:, : acc.shape[1] - span, :]
            base += span
            span *= 2
        tot = acc[:, POOL_HALO - base:, :]
        if full_windows:
            mean = tot / float(win)
        else:
            cnt = jnp.minimum(row + 1, win).astype(F32)
            mean = tot / cnt
        zg = (mean - u[:, :, cs]).reshape(rows, POOL_CH).astype(BF16)
        parts.append(_dot(zg, pw_ref[0, g]))
    a = jnp.concatenate(parts, axis=-1) * ps_ref[0]

    l1, l2, l3 = l1_ref[...], l2_ref[...], l3_ref[...]
    lm = jnp.maximum(jnp.maximum(l1, l2), l3)
    e1, e2, e3 = jnp.exp(l1 - lm), jnp.exp(l2 - lm), jnp.exp(l3 - lm)
    esum = e1 + e2 + e3
    bmix = (e1 / esum) * o1_ref[...] + (e2 / esum) * o2_ref[...] + (e3 / esum) * o3_ref[...]
    b = _dot(bmix.astype(BF16), wb_ref[...])
    c = _dot(oc_ref[...].astype(BF16), wc_ref[...])
    g_a = _sigmoid(gl_ref[:, :d])
    g_b = _sigmoid(gl_ref[:, d:2 * d])
    g_c = _sigmoid(gl_ref[:, 2 * d:])
    mix = _dot((g_a * a + g_b * b + g_c * c).astype(BF16), wo_ref[...])

    x = x_ref[...]
    xm = x + gt_ref[...] * _rms(mix, g2_ref[0]).reshape(s, t, d)
    xmid_ref[...] = xm
    h2 = (_rms(xm, g3_ref[...]) * (1.0 + scf_ref[...]) + shf_ref[...]).reshape(rows, d).astype(BF16)
    h2_ref[...] = h2

    scores = _sigmoid(_dot(h2, wr_ref[...]))
    work = scores + rb_ref[0]
    lane = lax.broadcasted_iota(jnp.int32, (rows, N_EXPERTS), 1).astype(F32)
    sel = jnp.zeros((rows, N_EXPERTS), F32)
    picks = []
    for _ in range(TOP_K):
        mx = jnp.max(work, axis=-1, keepdims=True)
        pick = jnp.min(jnp.where(work == mx, lane, float(N_EXPERTS)), axis=-1, keepdims=True)
        hit = lane == pick
        sel = jnp.where(hit, 1.0, sel)
        work = jnp.where(hit, NEG_INF, work)
        picks.append(pick)
    top_s = scores * sel
    wmat_ref[...] = top_s / jnp.sum(top_s, axis=-1, keepdims=True) * ROUTED_SCALE
    kcol = lax.broadcasted_iota(jnp.int32, (rows, TOP_K), 1)
    idx = jnp.zeros((rows, TOP_K), F32)
    for k, pick in enumerate(picks):
        idx = jnp.where(kcol == k, pick, idx)
    idx_ref[...] = idx.astype(jnp.int32)


def _post_mixer(x3, ada3, u3, halo3, attn, gl, weights, layer, tm, full_windows):
    nseq, tseq, d = x3.shape
    ntok = nseq * tseq
    if tseq >= tm:
        s_blk, t_blk = 1, tm
    else:
        s_blk, t_blk = tm // tseq, tseq
    tps = tseq // t_blk
    n_tiles = ntok // tm
    (g2, g3, pool_w, pool_scale, w_b_up, w_c_up, w_out, w_router, router_bias) = weights
    o1, l1, o2, l2, o3, l3, oc = attn

    def x_map(i):
        return (i // tps, i % tps, 0)

    def tok_map(i):
        return (i, 0)

    def ada_spec(j):
        return pl.BlockSpec((s_blk, 1, d), lambda i: (i // tps, 0, j))

    if full_windows:
        halo_spec = pl.BlockSpec((s_blk, POOL_HALO, POOL_WIDTH), lambda i: (i, 0, 0))
    else:
        hb = t_blk // POOL_HALO
        halo_spec = pl.BlockSpec((1, POOL_HALO, POOL_WIDTH),
                                 lambda i: (i // tps, jnp.maximum((i % tps) * hb - 1, 0), 0))
    vec = pl.BlockSpec((1, 1, d), lambda i: (layer, 0, 0))
    in_specs = [
        pl.BlockSpec((s_blk, t_blk, d), x_map), ada_spec(2), ada_spec(3), ada_spec(4), vec, vec,
        pl.BlockSpec((s_blk, t_blk, POOL_WIDTH), x_map), halo_spec,
        pl.BlockSpec((tm, DIL_W), tok_map), pl.BlockSpec((tm, DIL_W), tok_map),
        pl.BlockSpec((tm, DIL_W), tok_map), pl.BlockSpec((tm, DIL_W), tok_map),
        pl.BlockSpec((tm, DIL_W), tok_map), pl.BlockSpec((tm, DIL_W), tok_map),
        pl.BlockSpec((tm, SWA_QW), tok_map), pl.BlockSpec((tm, 3 * d), tok_map),
        pl.BlockSpec((1,) + pool_w.shape[1:], lambda i: (layer, 0, 0, 0)),
        vec,
        pl.BlockSpec(w_b_up.shape, lambda i: (0, 0)), pl.BlockSpec(w_c_up.shape, lambda i: (0, 0)),
        pl.BlockSpec(w_out.shape, lambda i: (0, 0)), pl.BlockSpec(w_router.shape, lambda i: (0, 0)),
        pl.BlockSpec((1, 1, N_EXPERTS), lambda i: (layer, 0, 0)),
    ]
    out_shape = [
        jax.ShapeDtypeStruct((nseq, tseq, d), F32),
        jax.ShapeDtypeStruct((ntok, d), BF16),
        jax.ShapeDtypeStruct((ntok, N_EXPERTS), F32),
        jax.ShapeDtypeStruct((ntok, TOP_K), jnp.int32),
    ]
    out_specs = [
        pl.BlockSpec((s_blk, t_blk, d), x_map),
        pl.BlockSpec((tm, d), tok_map),
        pl.BlockSpec((tm, N_EXPERTS), tok_map),
        pl.BlockSpec((tm, TOP_K), tok_map),
    ]
    return pl.pallas_call(
        functools.partial(_post_kernel, tps=tps, full_windows=full_windows),
        grid=(n_tiles,),
        in_specs=in_specs,
        out_specs=out_specs,
        out_shape=out_shape,
        compiler_params=_cparams(("parallel",)),
        name="post_mixer",
    )(x3, ada3, ada3, ada3, g2, g3, u3, halo3, o1, l1, o2, l2, o3, l3, oc, gl,
      pool_w, pool_scale, w_b_up, w_c_up, w_out, w_router, router_bias)


def _expert_kernel(bexp_ref, nused_ref, x_ref, wg_ref, wu_ref, wd_ref, y_ref):
    i = pl.program_id(0)

    @pl.when(i < nused_ref[0])
    def _():
        x = x_ref[...]
        gate = _dot(x, wg_ref[0, 0].astype(BF16))
        up = _dot(x, wu_ref[0, 0].astype(BF16))
        y_ref[...] = _dot((_silu(gate) * up).astype(BF16), wd_ref[0, 0].astype(BF16))

    @pl.when(i >= nused_ref[0])
    def _():
        y_ref[...] = jnp.zeros_like(y_ref)


def _experts(x_sorted, block_exp, n_used, we_gate, we_up, we_down, layer, bm):
    n_rows, d = x_sorted.shape
    ff = we_gate.shape[-1]
    n_blocks = n_rows // bm
    grid_spec = pltpu.PrefetchScalarGridSpec(
        num_scalar_prefetch=2,
        grid=(n_blocks,),
        in_specs=[
            pl.BlockSpec((bm, d), lambda i, be, nu: (i, 0)),
            pl.BlockSpec((1, 1, d, ff), lambda i, be, nu: (layer, be[i], 0, 0)),
            pl.BlockSpec((1, 1, d, ff), lambda i, be, nu: (layer, be[i], 0, 0)),
            pl.BlockSpec((1, 1, ff, d), lambda i, be, nu: (layer, be[i], 0, 0)),
        ],
        out_specs=pl.BlockSpec((bm, d), lambda i, be, nu: (i, 0)),
    )
    return pl.pallas_call(
        _expert_kernel,
        grid_spec=grid_spec,
        out_shape=jax.ShapeDtypeStruct((n_rows, d), F32),
        compiler_params=_cparams(("arbitrary",)),
        name="experts",
    )(block_exp, n_used, x_sorted, we_gate, we_up, we_down)


def _final_kernel(x_ref, gt_ref, g4_ref, h2_ref, yr_ref, wg_ref, wu_ref, wd_ref, o_ref):
    s, t, d = x_ref.shape
    h2 = h2_ref[...]
    shared = _dot((_silu(_dot(h2, wg_ref[...])) * _dot(h2, wu_ref[...])).astype(BF16), wd_ref[...])
    f = yr_ref[...] + shared
    o_ref[...] = x_ref[...] + gt_ref[...] * _rms(f, g4_ref[0]).reshape(s, t, d)


def _final(xmid3, ada3, g4, h2, y_routed, ws_gate, ws_up, ws_down, layer, tm):
    nseq, tseq, d = xmid3.shape
    ntok = nseq * tseq
    if tseq >= tm:
        s_blk, t_blk = 1, tm
    else:
        s_blk, t_blk = tm // tseq, tseq
    tps = tseq // t_blk

    def x_map(i):
        return (i // tps, i % tps, 0)

    return pl.pallas_call(
        _final_kernel,
        grid=(ntok // tm,),
        in_specs=[
            pl.BlockSpec((s_blk, t_blk, d), x_map),
            pl.BlockSpec((s_blk, 1, d), lambda i: (i // tps, 0, 5)),
            pl.BlockSpec((1, 1, d), lambda i: (layer, 0, 0)),
            pl.BlockSpec((tm, d), lambda i: (i, 0)),
            pl.BlockSpec((tm, d), lambda i: (i, 0)),
            pl.BlockSpec(ws_gate.shape, lambda i: (0, 0)),
            pl.BlockSpec(ws_up.shape, lambda i: (0, 0)),
            pl.BlockSpec(ws_down.shape, lambda i: (0, 0)),
        ],
        out_specs=pl.BlockSpec((s_blk, t_blk, d), x_map),
        out_shape=jax.ShapeDtypeStruct((nseq, tseq, d), F32),
        compiler_params=_cparams(("parallel",)),
        name="final",
    )(xmid3, ada3, g4, h2, y_routed, ws_gate, ws_up, ws_down)


def _t5_bucket(dist):
    n = np.asarray(dist, dtype=np.int64)
    exact = N_BUCKETS // 2
    log_ratio = np.log(np.maximum(n, 1) / exact) / np.log(MAX_DISTANCE / exact)
    large = np.minimum(exact + (log_ratio * (N_BUCKETS - exact)).astype(np.int64), N_BUCKETS - 1)
    return np.where(n < exact, n, large).astype(np.int32)


def _step_bias(rel_bias, dil, h0, h1):
    buckets = _t5_bucket(np.arange(BLOCK + 1) * dil)
    return jnp.take(rel_bias, buckets, axis=0)[:, h0:h1].T.astype(F32)


def _band_bias(bias_s):
    dist = BLOCK + np.arange(BLOCK)[:, None] - np.arange(2 * BLOCK)[None, :]
    mask = (dist >= 0) & (dist <= BLOCK)
    return jnp.where(mask[None], bias_s[:, np.clip(dist, 0, BLOCK)], NEG_INF)


def _sample_bias(bias_s, n_buf, dil, t_new):
    t = np.arange(t_new)[:, None]
    delta_c = n_buf + t - np.arange(n_buf)[None, :]
    ok_c = (delta_c % dil == 0) & (delta_c // dil <= BLOCK)
    bmc = jnp.where(ok_c[None], bias_s[:, np.clip(delta_c // dil, 0, BLOCK)], NEG_INF)
    delta_n = t - np.arange(t_new)[None, :]
    ok_n = (delta_n >= 0) & (delta_n % dil == 0) & (delta_n // dil <= BLOCK)
    bmn = jnp.where(ok_n[None], bias_s[:, np.clip(delta_n // dil, 0, BLOCK)], NEG_INF)
    return bmc, bmn


def _dispatch_plan(idx, bm, n_tok):
    n_asg = n_tok * TOP_K
    n_blocks = -(-(n_asg + N_EXPERTS * (bm - 1)) // bm)
    n_rows = n_blocks * bm
    onehot = (idx[:, :, None] == jnp.arange(N_EXPERTS, dtype=jnp.int32)[None, None, :]).astype(jnp.int32)
    sel = jnp.sum(onehot, axis=1)
    csum = jnp.cumsum(sel, axis=0)
    counts = csum[-1]
    pos = csum - sel
    padded = (counts + bm - 1) // bm * bm
    pad_end = jnp.cumsum(padded)
    pad_start = pad_end - padded
    dest_all = pad_start[None, :] + pos
    dest = jnp.take_along_axis(dest_all, idx, axis=1)
    tok = jnp.broadcast_to(jnp.arange(n_tok, dtype=jnp.int32)[:, None], (n_tok, TOP_K))
    row_tok = jnp.full((n_rows,), n_tok, jnp.int32).at[dest.reshape(-1)].set(tok.reshape(-1))
    block_exp = jnp.minimum(
        jnp.searchsorted(pad_end, jnp.arange(n_blocks, dtype=jnp.int32) * bm, side="right"), N_EXPERTS - 1
    ).astype(jnp.int32)
    n_used = (pad_end[-1] // bm).astype(jnp.int32).reshape(1)
    return dest, row_tok, block_exp, n_used


def _moe_routed(h2, wmat, idx, we_gate, we_up, we_down, layer, bm):
    n_tok, d = h2.shape
    dest, row_tok, block_exp, n_used = _dispatch_plan(idx, bm, n_tok)
    h2_pad = jnp.concatenate([h2, jnp.zeros((1, d), h2.dtype)], axis=0)
    x_sorted = h2_pad[row_tok]
    y_sorted = _experts(x_sorted, block_exp, n_used, we_gate, we_up, we_down, layer, bm)
    w_k = jnp.take_along_axis(wmat, idx, axis=1)
    return jnp.sum(y_sorted[dest] * w_k[:, :, None], axis=1)


def kernel(x_prompt, x_sample, cache_b1, cache_b2, cache_b3, cache_c, state_pool, c_prompt, c_sample, rel_bias, w_ada, b_ada, g_pre_mix, g_post_mix, g_pre_ffn, g_post_ffn, w_in, pool_w, pool_scale, w_b_up, w_c_up, sinks, w_out, w_router, router_bias, we_gate, we_up, we_down, ws_gate, ws_up, ws_down):
    batch, seq, d = x_prompt.shape
    dec_batch, dec_seq, _ = x_sample.shape
    depth = w_in.shape[0]
    n_p, n_s = batch * seq, dec_batch * dec_seq
    tm_p = 256
    tm_s = 128

    n_seq_all = batch + dec_batch
    pad = -n_seq_all % 8
    c_all = jnp.concatenate([c_prompt, c_sample, jnp.zeros((pad, d), F32)], axis=0)
    ada = _ada_all(c_all, w_ada, b_ada)

    bias_b = [_step_bias(rel_bias, dil, g * DIL_HEADS, (g + 1) * DIL_HEADS) for g, (_, dil) in enumerate(DIL_GROUPS)]
    bias_c = _step_bias(rel_bias, 1, len(DIL_GROUPS) * DIL_HEADS, len(DIL_GROUPS) * DIL_HEADS + SWA_HEADS)
    band_b = [_band_bias(b) for b in bias_b]
    band_c = _band_bias(bias_c)
    caches_b = (cache_b1, cache_b2, cache_b3)
    samp_b = [_sample_bias(bias_b[g], caches_b[g].shape[2], dil, dec_seq) for g, (_, dil) in enumerate(DIL_GROUPS)]
    samp_c = _sample_bias(bias_c, cache_c.shape[2], 1, dec_seq)

    splits = np.cumsum([0, POOL_WIDTH, 3 * DIL_W, 3 * DIL_W, 3 * DIL_W, SWA_QW, SWA_KW, SWA_KW, 3 * d])
    col_slices = [(int(splits[0]), int(splits[1]))]
    for g in range(len(DIL_GROUPS)):
        for part in range(3):
            c0 = int(splits[1 + part]) + g * DIL_W
            col_slices.append((c0, c0 + DIL_W))
    col_slices.append((int(splits[4]), int(splits[8])))

    vec3 = lambda a: a.reshape(depth, 1, -1)
    g1, g2, g3, g4 = vec3(g_pre_mix), vec3(g_post_mix), vec3(g_pre_ffn), vec3(g_post_ffn)
    ps3, rb3 = vec3(pool_scale), vec3(router_bias)
    cb = [c.reshape(c.shape[0], c.shape[1], c.shape[2], -1) for c in (cache_b1, cache_b2, cache_b3, cache_c)]
    pool_halo = jnp.pad(state_pool, ((0, 0), (0, 0), (POOL_HALO - state_pool.shape[2], 0), (0, 0)))

    xp, xs = x_prompt, x_sample
    cache_out = [None, None, None, None]
    st_p = [[], [], [], [], []]
    pool_s = []
    keeps_p = tuple(min(win, seq) for win, _ in DIL_GROUPS) + (min(BLOCK, seq),)
    for l in range(depth):
        w_perm = jnp.concatenate([w_in[l][:, a:b] for a, b in col_slices], axis=1).astype(BF16)
        wb, wc, wo, wr = w_b_up[l].astype(BF16), w_c_up[l].astype(BF16), w_out[l].astype(BF16), w_router[l].astype(BF16)
        wsg, wsu, wsd = ws_gate[l].astype(BF16), ws_up[l].astype(BF16), ws_down[l].astype(BF16)
        pw = pool_w.astype(BF16)
        post_w = (g2, g3, pw, ps3, wb, wc, wo, wr, rb3)
        ada_p = ada[l, :batch].reshape(batch, 1, -1)
        ada_s = ada[l, batch:n_seq_all].reshape(dec_batch, 1, -1)

        (u, q1, kv1, q2, kv2, q3, kv3, qc, kvc, gl, s1, s2, s3, sc) = _in_proj(xp, ada_p, g1, w_perm, l, tm_p, keeps_p)
        attn = []
        for g, ((_, dil), q, kv) in enumerate(zip(DIL_GROUPS, (q1, q2, q3), (kv1, kv2, kv3))):
            attn.extend(_band_attn(q, kv, band_b[g], batch, seq, dil, DIL_HEADS, DIL_HEADS))
        attn.append(_band_attn(qc, kvc, band_c, batch, seq, 1, SWA_HEADS, SWA_KV_HEADS, sinks=sinks[l]))
        xmid, h2, wmat, idx = _post_mixer(xp, ada_p, u, u, attn, gl, post_w, l, tm_p, False)
        y_r = _moe_routed(h2, wmat, idx, we_gate, we_up, we_down, l, 256)
        xp = _final(xmid, ada_p, g4, h2, y_r, wsg, wsu, wsd, l, tm_p)
        for k, (s, keep) in enumerate(zip((s1, s2, s3, sc), keeps_p)):
            st_p[k].append(s.reshape(batch, keep, 2, -1, HEAD_DIM))
        st_p[4].append(u[:, seq - (POOL_HALO - 1):])

        (u, q1, kv1, q2, kv2, q3, kv3, qc, kvc, gl, s1, s2, s3, sc) = _in_proj(
            xs, ada_s, g1, w_perm, l, tm_s, (dec_seq,) * 4)
        attn = []
        for g, (q, new) in enumerate(zip((q1, q2, q3), (s1, s2, s3))):
            o, lse, cache_out[g] = _sample_attn(q, new, cb[g], cache_out[g], samp_b[g][0], samp_b[g][1], l,
                                                DIL_HEADS, DIL_HEADS, dec_seq)
            attn.extend([o, lse])
        o, cache_out[3] = _sample_attn(qc, sc, cb[3], cache_out[3], samp_c[0], samp_c[1], l,
                                       SWA_HEADS, SWA_KV_HEADS, dec_seq, sinks=sinks[l])
        attn.append(o)
        xmid, h2, wmat, idx = _post_mixer(xs, ada_s, u, pool_halo[l], attn, gl, post_w, l, tm_s, True)
        y_r = _moe_routed(h2, wmat, idx, we_gate, we_up, we_down, l, 128)
        xs = _final(xmid, ada_s, g4, h2, y_r, wsg, wsu, wsd, l, tm_s)
        pool_s.append(jnp.concatenate([state_pool[l], u], axis=1)[:, -(POOL_HALO - 1):])

    b1_p, b2_p, b3_p, c_p, pool_p = [jnp.stack(s, axis=0) for s in st_p]
    outs_s = [co.reshape(c.shape) for co, c in zip(cache_out, (cache_b1, cache_b2, cache_b3, cache_c))]
    return (xp, xs, b1_p, b2_p, b3_p, c_p, pool_p, outs_s[0], outs_s[1], outs_s[2], outs_s[3], jnp.stack(pool_s, axis=0))
```

```python
import functools

import numpy as np
import jax
import jax.numpy as jnp
from jax import lax
from jax.experimental import pallas as pl
from jax.experimental.pallas import tpu as pltpu
from jax.experimental.pallas import tpu_sc as plsc

F32 = jnp.float32
BF16 = jnp.bfloat16

HEAD_DIM = 64
SCALE = HEAD_DIM ** -0.5
BLOCK = 128
POOL_WINDOWS = (2, 4, 8, 16)
POOL_CH = 128
POOL_WIDTH = len(POOL_WINDOWS) * POOL_CH
POOL_HALO = 16
DIL_GROUPS = ((128, 1), (512, 4), (2048, 16))
DIL_HEADS = 4
DIL_W = DIL_HEADS * HEAD_DIM
SWA_HEADS = 8
SWA_KV_HEADS = 2
SWA_QW = SWA_HEADS * HEAD_DIM
SWA_KW = SWA_KV_HEADS * HEAD_DIM
N_BUCKETS = 32
MAX_DISTANCE = 2048
N_EXPERTS = 64
TOP_K = 8
ROUTED_SCALE = 2.5
EPS = 1e-6
NEG_INF = float("-inf")

V7X_VMEM_BYTES = 64 * 1024 * 1024
VMEM_LIMIT = 56 * 1024 * 1024


def _cparams(sem):
    return pltpu.CompilerParams(dimension_semantics=sem, vmem_limit_bytes=VMEM_LIMIT)


def _rms(x, g):
    ms = jnp.mean(x * x, axis=-1, keepdims=True)
    return x * lax.rsqrt(ms + EPS) * g


def _sigmoid(x):
    return 1.0 / (1.0 + jnp.exp(-x))


def _silu(x):
    return x * _sigmoid(x)


def _dot(a, b):
    return jnp.dot(a, b, preferred_element_type=F32)


def _dot_nt(a, b):
    return lax.dot_general(a, b, (((1,), (1,)), ((), ())), preferred_element_type=F32)


ROW_SUB = 4
ROW_LANES = 128
ROW_DTYPE = jnp.uint32
HI_MASK = 0xFFFF0000


def _pack_chunks(x):
    half = x.shape[1] // 2
    hi = pltpu.bitcast(x[:, :half].astype(BF16).astype(F32), jnp.uint32) & jnp.uint32(HI_MASK)
    lo = pltpu.bitcast(x[:, half:].astype(BF16).astype(F32), jnp.uint32) >> 16
    w = hi | lo
    return [w[:, j * ROW_LANES:(j + 1) * ROW_LANES] for j in range(ROW_SUB)]


def _unpack_chunks(chunks):
    his = [pltpu.bitcast(w & jnp.uint32(HI_MASK), F32) for w in chunks]
    los = [pltpu.bitcast(w << 16, F32) for w in chunks]
    return jnp.concatenate(his + los, axis=1)


def _store_rows(ref, chunks, rows, lead=None):
    for j, c in enumerate(chunks):
        if lead is None:
            ref[pl.ds(j, rows, stride=ROW_SUB), :] = c
        else:
            ref[lead, pl.ds(j, rows, stride=ROW_SUB), :] = c


def _load_rows(ref, rows, lead=None):
    if lead is None:
        return [ref[pl.ds(j, rows, stride=ROW_SUB), :] for j in range(ROW_SUB)]
    return [ref[lead, pl.ds(j, rows, stride=ROW_SUB), :] for j in range(ROW_SUB)]


SC_CORES = 2
SC_SUBCORES = 16
SC_WORKERS = SC_CORES * SC_SUBCORES
SC_CHUNK = 64


def _sc_scatter_rows(src, dest, n_out):
    n, sub, lanes = src.shape
    kk = dest.shape[1]
    ch = min(SC_CHUNK, n // SC_WORKERS)
    n_chunks = n // (SC_WORKERS * ch)
    idx3 = dest.reshape(n // ch, ch, kk).transpose(0, 2, 1)
    mesh = plsc.VectorSubcoreMesh(core_axis_name="c", subcore_axis_name="s")

    @functools.partial(
        pl.kernel, mesh=mesh,
        out_type=jax.ShapeDtypeStruct((n_out, sub, lanes), src.dtype),
        scratch_types=[pltpu.VMEM((kk, ch), jnp.int32), pltpu.VMEM((ch, sub, lanes), src.dtype),
                       pltpu.SemaphoreType.DMA],
    )
    def scatter_kernel(src_hbm, idx_hbm, out_hbm, idx_v, rows_v, sem):
        wid = lax.axis_index("s") * SC_CORES + lax.axis_index("c")

        @pl.loop(0, n_chunks)
        def _(c):
            j = wid * n_chunks + c
            pltpu.sync_copy(idx_hbm.at[j], idx_v)
            pltpu.sync_copy(src_hbm.at[pl.ds(j * ch, ch)], rows_v)
            copies = [pltpu.async_copy(rows_v, out_hbm.at[idx_v.at[q]], sem) for q in range(kk)]
            for cp in copies:
                cp.wait()

    return scatter_kernel(src, idx3)


def _sc_gather_rows(table, dest):
    v, sub, lanes = table.shape
    n, kk = dest.shape
    total = n * kk
    ch = min(SC_CHUNK, total // SC_WORKERS)
    n_chunks = total // (SC_WORKERS * ch)
    idx2 = dest.T.reshape(total // ch, ch)
    mesh = plsc.VectorSubcoreMesh(core_axis_name="c", subcore_axis_name="s")

    @functools.partial(
        pl.kernel, mesh=mesh,
        out_type=jax.ShapeDtypeStruct((total, sub, lanes), table.dtype),
        scratch_types=[pltpu.VMEM((1, ch), jnp.int32), pltpu.VMEM((ch, sub, lanes), table.dtype),
                       pltpu.SemaphoreType.DMA],
    )
    def gather_kernel(tab_hbm, idx_hbm, out_hbm, idx_v, rows_v, sem):
        wid = lax.axis_index("s") * SC_CORES + lax.axis_index("c")

        @pl.loop(0, n_chunks)
        def _(c):
            j = wid * n_chunks + c
            pltpu.sync_copy(idx_hbm.at[pl.ds(j, 1)], idx_v)
            pltpu.async_copy(tab_hbm.at[idx_v.at[0]], rows_v, sem).wait()
            pltpu.sync_copy(rows_v, out_hbm.at[pl.ds(j * ch, ch)])

    return gather_kernel(table, idx2).reshape(kk, n, sub, lanes)


def _ada_kernel(c_ref, w_ref, b_ref, o_ref):
    c = _silu(c_ref[...]).astype(BF16)
    o_ref[0] = _dot(c, w_ref[0].astype(BF16)) + b_ref[0]


def _ada_all(c_all, w_ada, b_ada):
    depth, d, n = w_ada.shape
    rows = c_all.shape[0]
    tn = 1536
    return pl.pallas_call(
        _ada_kernel,
        grid=(depth, n // tn),
        in_specs=[
            pl.BlockSpec((rows, d), lambda l, j: (0, 0)),
            pl.BlockSpec((1, d, tn), lambda l, j: (l, 0, j)),
            pl.BlockSpec((1, 1, tn), lambda l, j: (l, 0, j)),
        ],
        out_specs=pl.BlockSpec((1, rows, tn), lambda l, j: (l, 0, j)),
        out_shape=jax.ShapeDtypeStruct((depth, rows, n), F32),
        compiler_params=_cparams(("parallel", "parallel")),
        name="ada",
    )(c_all, w_ada, b_ada.reshape(depth, 1, n))


def _in_proj_kernel(x_ref, sh_ref, sc_ref, g_ref, w_ref,
                    u_ref, q1_ref, kv1_ref, q2_ref, kv2_ref, q3_ref, kv3_ref, qc_ref, kvc_ref, gl_ref,
                    st1_ref, st2_ref, st3_ref, stc_ref):
    s, t, d = x_ref.shape
    rows = s * t
    h = _rms(x_ref[...], g_ref[...]) * (1.0 + sc_ref[...]) + sh_ref[...]
    hb = h.reshape(rows, d).astype(BF16)

    def proj(c0, c1):
        return _dot(hb, w_ref[:, c0:c1])

    u_ref[...] = proj(0, POOL_WIDTH).reshape(s, t, POOL_WIDTH)
    off = POOL_WIDTH
    for q_ref, kv_ref, st_ref in ((q1_ref, kv1_ref, st1_ref), (q2_ref, kv2_ref, st2_ref), (q3_ref, kv3_ref, st3_ref)):
        z = proj(off, off + 3 * DIL_W)
        q_ref[...] = (z[:, :DIL_W] * SCALE).astype(BF16)
        kv_ref[...] = z[:, DIL_W:].astype(BF16)
        st_ref[...] = z[rows - st_ref.shape[0]:, DIL_W:]
        off += 3 * DIL_W
    z = proj(off, off + SWA_QW + 2 * SWA_KW)
    qc_ref[...] = (z[:, :SWA_QW] * SCALE).astype(BF16)
    kvc_ref[...] = z[:, SWA_QW:].astype(BF16)
    stc_ref[...] = z[rows - stc_ref.shape[0]:, SWA_QW:]
    off += SWA_QW + 2 * SWA_KW
    gl_ref[...] = proj(off, off + 3 * d)


def _in_proj(x3, ada3, g_pre, w_perm, layer, tm, keeps):
    nseq, tseq, d = x3.shape
    ntok = nseq * tseq
    if tseq >= tm:
        s_blk, t_blk = 1, tm
    else:
        s_blk, t_blk = tm // tseq, tseq
    tps = tseq // t_blk
    n_tiles = ntok // tm
    in_w = w_perm.shape[-1]

    def x_map(i):
        return (i // tps, i % tps, 0)

    def tok_map(i):
        return (i, 0)

    widths = (2 * DIL_W, 2 * DIL_W, 2 * DIL_W, 2 * SWA_KW)
    st_specs, st_shapes = [], []
    for keep, w in zip(keeps, widths):
        keep = min(keep, tseq)
        if keep == tseq and tseq < tm:
            st_shapes.append(jax.ShapeDtypeStruct((ntok, w), F32))
            st_specs.append(pl.BlockSpec((tm, w), tok_map))
            continue
        sb = min(tm, keep)
        bps = keep // sb
        first = (tseq - keep) // tm

        def st_map(i, bps=bps, first=first):
            return ((i // tps) * bps + jnp.maximum(i % tps - first, 0) * (1 if bps > 1 else 0), 0)

        st_shapes.append(jax.ShapeDtypeStruct((nseq * keep, w), F32))
        st_specs.append(pl.BlockSpec((sb, w), st_map))

    out_shapes = [
        jax.ShapeDtypeStruct((nseq, tseq, POOL_WIDTH), F32),
        jax.ShapeDtypeStruct((ntok, DIL_W), BF16), jax.ShapeDtypeStruct((ntok, 2 * DIL_W), BF16),
        jax.ShapeDtypeStruct((ntok, DIL_W), BF16), jax.ShapeDtypeStruct((ntok, 2 * DIL_W), BF16),
        jax.ShapeDtypeStruct((ntok, DIL_W), BF16), jax.ShapeDtypeStruct((ntok, 2 * DIL_W), BF16),
        jax.ShapeDtypeStruct((ntok, SWA_QW), BF16), jax.ShapeDtypeStruct((ntok, 2 * SWA_KW), BF16),
        jax.ShapeDtypeStruct((ntok, 3 * d), F32),
    ] + st_shapes
    out_specs = [
        pl.BlockSpec((s_blk, t_blk, POOL_WIDTH), x_map),
        pl.BlockSpec((tm, DIL_W), tok_map), pl.BlockSpec((tm, 2 * DIL_W), tok_map),
        pl.BlockSpec((tm, DIL_W), tok_map), pl.BlockSpec((tm, 2 * DIL_W), tok_map),
        pl.BlockSpec((tm, DIL_W), tok_map), pl.BlockSpec((tm, 2 * DIL_W), tok_map),
        pl.BlockSpec((tm, SWA_QW), tok_map), pl.BlockSpec((tm, 2 * SWA_KW), tok_map),
        pl.BlockSpec((tm, 3 * d), tok_map),
    ] + st_specs
    return pl.pallas_call(
        _in_proj_kernel,
        grid=(n_tiles,),
        in_specs=[
            pl.BlockSpec((s_blk, t_blk, d), x_map),
            pl.BlockSpec((s_blk, 1, d), lambda i: (i // tps, 0, 0)),
            pl.BlockSpec((s_blk, 1, d), lambda i: (i // tps, 0, 1)),
            pl.BlockSpec((1, 1, d), lambda i: (layer, 0, 0)),
            pl.BlockSpec((d, in_w), lambda i: (0, 0)),
        ],
        out_specs=out_specs,
        out_shape=out_shapes,
        compiler_params=_cparams(("arbitrary",)),
        name="in_proj",
    )(x3, ada3, ada3, g_pre, w_perm)


def _band_attn_kernel(*refs, n_heads, n_kv, nq, with_sink):
    if with_sink:
        sink_ref, q_ref, kv_ref, halo_ref, bm_ref, o_ref = refs
        lse_ref = None
    else:
        q_ref, kv_ref, halo_ref, bm_ref, o_ref, lse_ref = refs
        sink_ref = None
    kw = n_kv * HEAD_DIM
    grp = n_heads // n_kv
    chunk = pl.program_id(2)

    def one_block(r0, kv_prev, first):
        qb = q_ref[0, pl.ds(r0, BLOCK), :]
        kv_cur = kv_ref[0, pl.ds(r0, BLOCK), :]
        for h in range(n_heads):
            kh = h // grp
            ks = slice(kh * HEAD_DIM, (kh + 1) * HEAD_DIM)
            vs = slice(kw + kh * HEAD_DIM, kw + (kh + 1) * HEAD_DIM)
            hs = slice(h * HEAD_DIM, (h + 1) * HEAD_DIM)
            qh = qb[:, hs]
            s_p = _dot_nt(qh, kv_prev[:, ks]) + bm_ref[h, :, :BLOCK]
            s_c = _dot_nt(qh, kv_cur[:, ks]) + bm_ref[h, :, BLOCK:]
            if first is not None:
                s_p = jnp.where(first, NEG_INF, s_p)
            m = jnp.maximum(jnp.max(s_p, axis=-1, keepdims=True), jnp.max(s_c, axis=-1, keepdims=True))
            if with_sink:
                m = jnp.maximum(m, sink_ref[h])
            p_p = jnp.exp(s_p - m)
            p_c = jnp.exp(s_c - m)
            den = jnp.sum(p_p, axis=-1, keepdims=True) + jnp.sum(p_c, axis=-1, keepdims=True)
            if with_sink:
                den = den + jnp.exp(sink_ref[h] - m)
            o = _dot(p_p.astype(BF16), kv_prev[:, vs]) + _dot(p_c.astype(BF16), kv_cur[:, vs])
            o_ref[0, pl.ds(r0, BLOCK), hs] = o / den
            if lse_ref is not None:
                lse_ref[0, pl.ds(r0, BLOCK), hs] = jnp.broadcast_to(m + jnp.log(den), (BLOCK, HEAD_DIM))

    one_block(0, halo_ref[0], chunk == 0)

    def body(j, carry):
        r0 = pl.multiple_of(j * BLOCK, BLOCK)
        one_block(r0, kv_ref[0, pl.ds(r0 - BLOCK, BLOCK), :], None)
        return carry

    if nq > 1:
        lax.fori_loop(1, nq, body, 0)


def _band_attn(q, kv, bm, batch, seq, dil, n_heads, n_kv, sinks=None):
    qw = n_heads * HEAD_DIM
    kvw = 2 * n_kv * HEAD_DIM
    fold = seq // dil
    nq = min(8, fold // BLOCK)
    rows = nq * BLOCK
    n_chunks = fold // rows
    qf = q.reshape(batch, fold, dil * qw)
    kvf = kv.reshape(batch, fold, dil * kvw)
    with_sink = sinks is not None
    in_specs = [
        pl.BlockSpec((1, rows, qw), lambda b, r, c: (b, c, r)),
        pl.BlockSpec((1, rows, kvw), lambda b, r, c: (b, c, r)),
        pl.BlockSpec((1, BLOCK, kvw), lambda b, r, c: (b, jnp.maximum(c * nq - 1, 0), r)),
        pl.BlockSpec((n_heads, BLOCK, 2 * BLOCK), lambda b, r, c: (0, 0, 0)),
    ]
    args = [qf, kvf, kvf, bm]
    o_spec = pl.BlockSpec((1, rows, qw), lambda b, r, c: (b, c, r))
    o_shape = jax.ShapeDtypeStruct((batch, fold, dil * qw), F32)
    if with_sink:
        in_specs = [pl.BlockSpec(memory_space=pltpu.SMEM)] + in_specs
        args = [sinks] + args
        out_specs, out_shape = o_spec, o_shape
    else:
        out_specs, out_shape = [o_spec, o_spec], [o_shape, o_shape]
    res = pl.pallas_call(
        functools.partial(_band_attn_kernel, n_heads=n_heads, n_kv=n_kv, nq=nq, with_sink=with_sink),
        grid=(batch, dil, n_chunks),
        in_specs=in_specs,
        out_specs=out_specs,
        out_shape=out_shape,
        compiler_params=_cparams(("parallel", "parallel", "parallel")),
        name="band_attn",
    )(*args)
    if with_sink:
        return res.reshape(batch * seq, qw)
    return res[0].reshape(batch * seq, qw), res[1].reshape(batch * seq, qw)


def _sample_attn_kernel(*refs, n_heads, n_kv, sb, t_new, with_sink, aliased):
    refs = list(refs)
    sink_ref = refs.pop(0) if with_sink else None
    q_ref, new_ref, cache_ref, bmc_ref, bmn_ref = refs[:5]
    refs = refs[5:]
    if aliased:
        refs = refs[1:]
    o_ref = refs[0]
    lse_ref = None if with_sink else refs[1]
    cout_ref = refs[-1]
    kw = n_kv * HEAD_DIM
    grp = n_heads // n_kv
    n_buf = cache_ref.shape[2]
    qf = q_ref[...].astype(F32)
    for s in range(sb):
        rs = slice(s * t_new, (s + 1) * t_new)
        new = new_ref[rs, :]
        cache = cache_ref[0, s]
        cb = cache.astype(BF16)
        nb = new.astype(BF16)
        qs = qf[rs, :].astype(BF16)
        for h in range(n_heads):
            kh = h // grp
            ks = slice(kh * HEAD_DIM, (kh + 1) * HEAD_DIM)
            vs = slice(kw + kh * HEAD_DIM, kw + (kh + 1) * HEAD_DIM)
            hs = slice(h * HEAD_DIM, (h + 1) * HEAD_DIM)
            qh = qs[:, hs]
            s_c = _dot_nt(qh, cb[:, ks]) + bmc_ref[h]
            s_n = _dot_nt(qh, nb[:, ks]) + bmn_ref[h]
            m = jnp.maximum(jnp.max(s_c, axis=-1, keepdims=True), jnp.max(s_n, axis=-1, keepdims=True))
            if with_sink:
                m = jnp.maximum(m, sink_ref[h])
            p_c = jnp.exp(s_c - m)
            p_n = jnp.exp(s_n - m)
            den = jnp.sum(p_c, axis=-1, keepdims=True) + jnp.sum(p_n, axis=-1, keepdims=True)
            if with_sink:
                den = den + jnp.exp(sink_ref[h] - m)
            o = _dot(p_c.astype(BF16), cb[:, vs]) + _dot(p_n.astype(BF16), nb[:, vs])
            o_ref[rs, hs] = o / den
            if lse_ref is not None:
                lse_ref[rs, hs] = jnp.broadcast_to(m + jnp.log(den), (t_new, HEAD_DIM))
        cout_ref[0, s, : n_buf - t_new, :] = cache[t_new:, :]
        cout_ref[0, s, n_buf - t_new:, :] = new


def _sample_attn(q, new, cache, prev_out, bmc, bmn, layer, n_heads, n_kv, t_new, sinks=None):
    depth, nseq, n_buf, w = cache.shape
    qw = n_heads * HEAD_DIM
    sb = max(1, min(8, 256 // n_buf)) if n_buf < 256 else 1
    sb = max(sb, 16 // t_new)
    with_sink = sinks is not None
    aliased = prev_out is not None
    rows = sb * t_new
    in_specs = [
        pl.BlockSpec((rows, qw), lambda i: (i, 0)),
        pl.BlockSpec((rows, w), lambda i: (i, 0)),
        pl.BlockSpec((1, sb, n_buf, w), lambda i: (layer, i, 0, 0)),
        pl.BlockSpec(bmc.shape, lambda i: (0, 0, 0)),
        pl.BlockSpec(bmn.shape, lambda i: (0, 0, 0)),
    ]
    args = [q, new, cache, bmc, bmn]
    if with_sink:
        in_specs = [pl.BlockSpec(memory_space=pltpu.SMEM)] + in_specs
        args = [sinks] + args
    aliases = {}
    if aliased:
        aliases = {len(args): 1 if with_sink else 2}
        in_specs.append(pl.BlockSpec(memory_space=pl.ANY))
        args.append(prev_out)
    o_spec = pl.BlockSpec((rows, qw), lambda i: (i, 0))
    o_shape = jax.ShapeDtypeStruct((nseq * t_new, qw), F32)
    c_spec = pl.BlockSpec((1, sb, n_buf, w), lambda i: (layer, i, 0, 0))
    c_shape = jax.ShapeDtypeStruct(cache.shape, F32)
    if with_sink:
        out_specs, out_shape = [o_spec, c_spec], [o_shape, c_shape]
    else:
        out_specs, out_shape = [o_spec, o_spec, c_spec], [o_shape, o_shape, c_shape]
    return pl.pallas_call(
        functools.partial(_sample_attn_kernel, n_heads=n_heads, n_kv=n_kv, sb=sb, t_new=t_new,
                          with_sink=with_sink, aliased=aliased),
        grid=(nseq // sb,),
        in_specs=in_specs,
        out_specs=out_specs,
        out_shape=out_shape,
        input_output_aliases=aliases,
        compiler_params=_cparams(("parallel",)),
        name="sample_attn",
    )(*args)


def _post_kernel(x_ref, gt_ref, shf_ref, scf_ref, g2_ref, g3_ref, u_ref, halo_ref,
                 o1_ref, l1_ref, o2_ref, l2_ref, o3_ref, l3_ref, oc_ref, gl_ref,
                 pw_ref, ps_ref, wb_ref, wc_ref, wo_ref, wr_ref, rb_ref,
                 xmid_ref, h2_ref, idx_ref, rank_ref, wk_ref, counts_ref, cnt_ref, *, tps, full_windows):
    s, t, d = x_ref.shape
    rows = s * t
    i = pl.program_id(0)

    u = u_ref[...]
    halo = halo_ref[...]
    if not full_windows:
        halo = jnp.where(i % tps == 0, 0.0, halo)
    ue = jnp.concatenate([halo, u], axis=1)
    if full_windows:
        row = None
    else:
        row = (i % tps) * t + lax.broadcasted_iota(jnp.int32, (1, t, 1), 1)
    parts = []
    for g, win in enumerate(POOL_WINDOWS):
        cs = slice(g * POOL_CH, (g + 1) * POOL_CH)
        acc = ue[:, :, cs]
        base = 0
        span = 1
        while span < win:
            acc = acc[:, span:, :] + acc[:, : acc.shape[1] - span, :]
            base += span
            span *= 2
        tot = acc[:, POOL_HALO - base:, :]
        if full_windows:
            mean = tot / float(win)
        else:
            cnt = jnp.minimum(row + 1, win).astype(F32)
            mean = tot / cnt
        zg = (mean - u[:, :, cs]).reshape(rows, POOL_CH).astype(BF16)
        parts.append(_dot(zg, pw_ref[0, g]))
    a = jnp.concatenate(parts, axis=-1) * ps_ref[0]

    l1, l2, l3 = l1_ref[...], l2_ref[...], l3_ref[...]
    lm = jnp.maximum(jnp.maximum(l1, l2), l3)
    e1, e2, e3 = jnp.exp(l1 - lm), jnp.exp(l2 - lm), jnp.exp(l3 - lm)
    esum = e1 + e2 + e3
    bmix = (e1 / esum) * o1_ref[...] + (e2 / esum) * o2_ref[...] + (e3 / esum) * o3_ref[...]
    b = _dot(bmix.astype(BF16), wb_ref[...])
    c = _dot(oc_ref[...].astype(BF16), wc_ref[...])
    g_a = _sigmoid(gl_ref[:, :d])
    g_b = _sigmoid(gl_ref[:, d:2 * d])
    g_c = _sigmoid(gl_ref[:, 2 * d:])
    mix = _dot((g_a * a + g_b * b + g_c * c).astype(BF16), wo_ref[...])

    x = x_ref[...]
    xm = x + gt_ref[...] * _rms(mix, g2_ref[0]).reshape(s, t, d)
    xmid_ref[...] = xm
    h2f = (_rms(xm, g3_ref[...]) * (1.0 + scf_ref[...]) + shf_ref[...]).reshape(rows, d)
    h2 = h2f.astype(BF16)
    _store_rows(h2_ref, _pack_chunks(h2f), rows)

    scores = _sigmoid(_dot(h2, wr_ref[...]))
    work = scores + rb_ref[0]
    lane = lax.broadcasted_iota(jnp.int32, (rows, N_EXPERTS), 1).astype(F32)
    sel = jnp.zeros((rows, N_EXPERTS), F32)
    picks = []
    for _ in range(TOP_K):
        mx = jnp.max(work, axis=-1, keepdims=True)
        pick = jnp.min(jnp.where(work == mx, lane, float(N_EXPERTS)), axis=-1, keepdims=True)
        hit = lane == pick
        sel = jnp.where(hit, 1.0, sel)
        work = jnp.where(hit, NEG_INF, work)
        picks.append(pick)
    top_s = scores * sel
    wmat = top_s / jnp.sum(top_s, axis=-1, keepdims=True) * ROUTED_SCALE

    @pl.when(i == 0)
    def _():
        cnt_ref[...] = jnp.zeros_like(cnt_ref)

    r_i = lax.broadcasted_iota(jnp.int32, (rows, rows), 0)
    c_i = lax.broadcasted_iota(jnp.int32, (rows, rows), 1)
    below = jnp.where(c_i < r_i, 1.0, 0.0).astype(BF16)
    rank_all = cnt_ref[...] + _dot(below, sel.astype(BF16))
    cnt_ref[...] = cnt_ref[...] + jnp.sum(sel, axis=0, keepdims=True)
    counts_ref[...] = cnt_ref[...].astype(jnp.int32)

    kcol = lax.broadcasted_iota(jnp.int32, (rows, TOP_K), 1)
    idx = jnp.zeros((rows, TOP_K), F32)
    rank = jnp.zeros((rows, TOP_K), F32)
    wk = jnp.zeros((rows, TOP_K), F32)
    for k, pick in enumerate(picks):
        hit = lane == pick
        idx = jnp.where(kcol == k, pick, idx)
        rank = jnp.where(kcol == k, jnp.sum(jnp.where(hit, rank_all, 0.0), axis=-1, keepdims=True), rank)
        wk = jnp.where(kcol == k, jnp.sum(jnp.where(hit, wmat, 0.0), axis=-1, keepdims=True), wk)
    idx_ref[...] = idx.astype(jnp.int32)
    rank_ref[...] = rank.astype(jnp.int32)
    wk_ref[...] = wk


def _post_mixer(x3, ada3, u3, halo3, attn, gl, weights, layer, tm, full_windows):
    nseq, tseq, d = x3.shape
    ntok = nseq * tseq
    if tseq >= tm:
        s_blk, t_blk = 1, tm
    else:
        s_blk, t_blk = tm // tseq, tseq
    tps = tseq // t_blk
    n_tiles = ntok // tm
    (g2, g3, pool_w, pool_scale, w_b_up, w_c_up, w_out, w_router, router_bias) = weights
    o1, l1, o2, l2, o3, l3, oc = attn

    def x_map(i):
        return (i // tps, i % tps, 0)

    def tok_map(i):
        return (i, 0)

    def ada_spec(j):
        return pl.BlockSpec((s_blk, 1, d), lambda i: (i // tps, 0, j))

    if full_windows:
        halo_spec = pl.BlockSpec((s_blk, POOL_HALO, POOL_WIDTH), lambda i: (i, 0, 0))
    else:
        hb = t_blk // POOL_HALO
        halo_spec = pl.BlockSpec((1, POOL_HALO, POOL_WIDTH),
                                 lambda i: (i // tps, jnp.maximum((i % tps) * hb - 1, 0), 0))
    vec = pl.BlockSpec((1, 1, d), lambda i: (layer, 0, 0))
    in_specs = [
        pl.BlockSpec((s_blk, t_blk, d), x_map), ada_spec(2), ada_spec(3), ada_spec(4), vec, vec,
        pl.BlockSpec((s_blk, t_blk, POOL_WIDTH), x_map), halo_spec,
        pl.BlockSpec((tm, DIL_W), tok_map), pl.BlockSpec((tm, DIL_W), tok_map),
        pl.BlockSpec((tm, DIL_W), tok_map), pl.BlockSpec((tm, DIL_W), tok_map),
        pl.BlockSpec((tm, DIL_W), tok_map), pl.BlockSpec((tm, DIL_W), tok_map),
        pl.BlockSpec((tm, SWA_QW), tok_map), pl.BlockSpec((tm, 3 * d), tok_map),
        pl.BlockSpec((1,) + pool_w.shape[1:], lambda i: (layer, 0, 0, 0)),
        vec,
        pl.BlockSpec(w_b_up.shape, lambda i: (0, 0)), pl.BlockSpec(w_c_up.shape, lambda i: (0, 0)),
        pl.BlockSpec(w_out.shape, lambda i: (0, 0)), pl.BlockSpec(w_router.shape, lambda i: (0, 0)),
        pl.BlockSpec((1, 1, N_EXPERTS), lambda i: (layer, 0, 0)),
    ]
    out_shape = [
        jax.ShapeDtypeStruct((nseq, tseq, d), F32),
        jax.ShapeDtypeStruct((ntok * ROW_SUB, ROW_LANES), ROW_DTYPE),
        jax.ShapeDtypeStruct((ntok, TOP_K), jnp.int32),
        jax.ShapeDtypeStruct((ntok, TOP_K), jnp.int32),
        jax.ShapeDtypeStruct((ntok, TOP_K), F32),
        jax.ShapeDtypeStruct((1, N_EXPERTS), jnp.int32),
    ]
    out_specs = [
        pl.BlockSpec((s_blk, t_blk, d), x_map),
        pl.BlockSpec((tm * ROW_SUB, ROW_LANES), tok_map),
        pl.BlockSpec((tm, TOP_K), tok_map),
        pl.BlockSpec((tm, TOP_K), tok_map),
        pl.BlockSpec((tm, TOP_K), tok_map),
        pl.BlockSpec((1, N_EXPERTS), lambda i: (0, 0)),
    ]
    return pl.pallas_call(
        functools.partial(_post_kernel, tps=tps, full_windows=full_windows),
        grid=(n_tiles,),
        in_specs=in_specs,
        out_specs=out_specs,
        out_shape=out_shape,
        scratch_shapes=[pltpu.VMEM((1, N_EXPERTS), F32)],
        compiler_params=_cparams(("arbitrary",)),
        name="post_mixer",
    )(x3, ada3, ada3, ada3, g2, g3, u3, halo3, o1, l1, o2, l2, o3, l3, oc, gl,
      pool_w, pool_scale, w_b_up, w_c_up, w_out, w_router, router_bias)


def _expert_kernel(bexp_ref, nvalid_ref, nused_ref, x_ref, wg_ref, wu_ref, wd_ref, y_ref, wg_s, wu_s, wd_s):
    i = pl.program_id(0)
    bm = x_ref.shape[0] // ROW_SUB

    @pl.when((i == 0) | (bexp_ref[i] != bexp_ref[jnp.maximum(i - 1, 0)]))
    def _():
        wg_s[...] = wg_ref[0, 0].astype(BF16)
        wu_s[...] = wu_ref[0, 0].astype(BF16)
        wd_s[...] = wd_ref[0, 0].astype(BF16)

    @pl.when(i < nused_ref[0])
    def _():
        x = _unpack_chunks(_load_rows(x_ref, bm))
        row = lax.broadcasted_iota(jnp.int32, (bm, 1), 0)
        x = jnp.where(row < nvalid_ref[i], x, 0.0).astype(BF16)
        gate = _dot(x, wg_s[...])
        up = _dot(x, wu_s[...])
        y = _dot((_silu(gate) * up).astype(BF16), wd_s[...])
        _store_rows(y_ref, _pack_chunks(y), bm)

    @pl.when(i >= nused_ref[0])
    def _():
        y_ref[...] = jnp.zeros_like(y_ref)


def _experts(x_rows, block_exp, nvalid, n_used, we_gate, we_up, we_down, layer, bm):
    n_rows = x_rows.shape[0] // ROW_SUB
    d, ff = we_gate.shape[-2:]
    n_blocks = n_rows // bm
    grid_spec = pltpu.PrefetchScalarGridSpec(
        num_scalar_prefetch=3,
        grid=(n_blocks,),
        in_specs=[
            pl.BlockSpec((bm * ROW_SUB, ROW_LANES), lambda i, be, nv, nu: (i, 0)),
            pl.BlockSpec((1, 1, d, ff), lambda i, be, nv, nu: (layer, be[i], 0, 0)),
            pl.BlockSpec((1, 1, d, ff), lambda i, be, nv, nu: (layer, be[i], 0, 0)),
            pl.BlockSpec((1, 1, ff, d), lambda i, be, nv, nu: (layer, be[i], 0, 0)),
        ],
        out_specs=pl.BlockSpec((bm * ROW_SUB, ROW_LANES), lambda i, be, nv, nu: (i, 0)),
        scratch_shapes=[pltpu.VMEM((d, ff), BF16), pltpu.VMEM((d, ff), BF16), pltpu.VMEM((ff, d), BF16)],
    )
    return pl.pallas_call(
        _expert_kernel,
        grid_spec=grid_spec,
        out_shape=jax.ShapeDtypeStruct(x_rows.shape, x_rows.dtype),
        compiler_params=_cparams(("arbitrary",)),
        name="experts",
    )(block_exp, nvalid, n_used, x_rows, we_gate, we_up, we_down)


def _final_kernel(x_ref, gt_ref, g4_ref, h2_ref, yk_ref, wk_ref, wg_ref, wu_ref, wd_ref, o_ref):
    s, t, d = x_ref.shape
    rows = s * t
    h2 = _unpack_chunks(_load_rows(h2_ref, rows)).astype(BF16)
    f = _dot((_silu(_dot(h2, wg_ref[...])) * _dot(h2, wu_ref[...])).astype(BF16), wd_ref[...])
    wk = wk_ref[...]
    for k in range(TOP_K):
        f = f + wk[:, k:k + 1] * _unpack_chunks(_load_rows(yk_ref, rows, lead=k))
    o_ref[...] = x_ref[...] + gt_ref[...] * _rms(f, g4_ref[0]).reshape(s, t, d)


def _final(xmid3, ada3, g4, h2_rows, yk_rows, wk, ws_gate, ws_up, ws_down, layer, tm):
    nseq, tseq, d = xmid3.shape
    ntok = nseq * tseq
    if tseq >= tm:
        s_blk, t_blk = 1, tm
    else:
        s_blk, t_blk = tm // tseq, tseq
    tps = tseq // t_blk

    def x_map(i):
        return (i // tps, i % tps, 0)

    return pl.pallas_call(
        _final_kernel,
        grid=(ntok // tm,),
        in_specs=[
            pl.BlockSpec((s_blk, t_blk, d), x_map),
            pl.BlockSpec((s_blk, 1, d), lambda i: (i // tps, 0, 5)),
            pl.BlockSpec((1, 1, d), lambda i: (layer, 0, 0)),
            pl.BlockSpec((tm * ROW_SUB, ROW_LANES), lambda i: (i, 0)),
            pl.BlockSpec((TOP_K, tm * ROW_SUB, ROW_LANES), lambda i: (0, i, 0)),
            pl.BlockSpec((tm, TOP_K), lambda i: (i, 0)),
            pl.BlockSpec(ws_gate.shape, lambda i: (0, 0)),
            pl.BlockSpec(ws_up.shape, lambda i: (0, 0)),
            pl.BlockSpec(ws_down.shape, lambda i: (0, 0)),
        ],
        out_specs=pl.BlockSpec((s_blk, t_blk, d), x_map),
        out_shape=jax.ShapeDtypeStruct((nseq, tseq, d), F32),
        compiler_params=_cparams(("parallel",)),
        name="final",
    )(xmid3, ada3, g4, h2_rows, yk_rows, wk, ws_gate, ws_up, ws_down)


def _t5_bucket(dist):
    n = np.asarray(dist, dtype=np.int64)
    exact = N_BUCKETS // 2
    log_ratio = np.log(np.maximum(n, 1) / exact) / np.log(MAX_DISTANCE / exact)
    large = np.minimum(exact + (log_ratio * (N_BUCKETS - exact)).astype(np.int64), N_BUCKETS - 1)
    return np.where(n < exact, n, large).astype(np.int32)


def _step_bias(rel_bias, dil, h0, h1):
    buckets = _t5_bucket(np.arange(BLOCK + 1) * dil)
    return jnp.take(rel_bias, buckets, axis=0)[:, h0:h1].T.astype(F32)


def _band_bias(bias_s):
    dist = BLOCK + np.arange(BLOCK)[:, None] - np.arange(2 * BLOCK)[None, :]
    mask = (dist >= 0) & (dist <= BLOCK)
    return jnp.where(mask[None], bias_s[:, np.clip(dist, 0, BLOCK)], NEG_INF)


def _sample_bias(bias_s, n_buf, dil, t_new):
    t = np.arange(t_new)[:, None]
    delta_c = n_buf + t - np.arange(n_buf)[None, :]
    ok_c = (delta_c % dil == 0) & (delta_c // dil <= BLOCK)
    bmc = jnp.where(ok_c[None], bias_s[:, np.clip(delta_c // dil, 0, BLOCK)], NEG_INF)
    delta_n = t - np.arange(t_new)[None, :]
    ok_n = (delta_n >= 0) & (delta_n % dil == 0) & (delta_n // dil <= BLOCK)
    bmn = jnp.where(ok_n[None], bias_s[:, np.clip(delta_n // dil, 0, BLOCK)], NEG_INF)
    return bmc, bmn


def _route_plan(idx, rank, counts, bm):
    n_tok = idx.shape[0]
    n_blocks = -(-(n_tok * TOP_K + N_EXPERTS * (bm - 1)) // bm)
    counts = counts.reshape(N_EXPERTS)
    padded = (counts + bm - 1) // bm * bm
    pad_end = jnp.cumsum(padded)
    pad_start = pad_end - padded
    experts = jnp.arange(N_EXPERTS, dtype=jnp.int32)
    onehot = idx[:, :, None] == experts[None, None, :]
    dest = rank + jnp.sum(jnp.where(onehot, pad_start[None, None, :], 0), axis=-1)
    starts = jnp.arange(n_blocks, dtype=jnp.int32) * bm
    block_exp = jnp.minimum(jnp.sum((starts[:, None] >= pad_end[None, :]).astype(jnp.int32), axis=1), N_EXPERTS - 1)
    hot = block_exp[:, None] == experts[None, :]
    blk_cnt = jnp.sum(jnp.where(hot, counts[None, :], 0), axis=1)
    blk_start = jnp.sum(jnp.where(hot, pad_start[None, :], 0), axis=1)
    nvalid = jnp.clip(blk_cnt - (starts - blk_start), 0, bm).astype(jnp.int32)
    n_used = (pad_end[-1] // bm).astype(jnp.int32).reshape(1)
    return dest.astype(jnp.int32), block_exp.astype(jnp.int32), nvalid, n_used, n_blocks * bm


def _moe_routed(h2_rows, idx, rank, counts, we_gate, we_up, we_down, layer, bm):
    n_tok = idx.shape[0]
    dest, block_exp, nvalid, n_used, n_rows = _route_plan(idx, rank, counts, bm)
    x_sorted = _sc_scatter_rows(h2_rows.reshape(n_tok, ROW_SUB, ROW_LANES), dest, n_rows)
    y_sorted = _experts(x_sorted.reshape(n_rows * ROW_SUB, ROW_LANES), block_exp, nvalid, n_used,
                        we_gate, we_up, we_down, layer, bm)
    yk = _sc_gather_rows(y_sorted.reshape(n_rows, ROW_SUB, ROW_LANES), dest)
    return yk.reshape(TOP_K, n_tok * ROW_SUB, ROW_LANES)


def kernel(x_prompt, x_sample, cache_b1, cache_b2, cache_b3, cache_c, state_pool, c_prompt, c_sample, rel_bias, w_ada, b_ada, g_pre_mix, g_post_mix, g_pre_ffn, g_post_ffn, w_in, pool_w, pool_scale, w_b_up, w_c_up, sinks, w_out, w_router, router_bias, we_gate, we_up, we_down, ws_gate, ws_up, ws_down):
    batch, seq, d = x_prompt.shape
    dec_batch, dec_seq, _ = x_sample.shape
    depth = w_in.shape[0]
    n_p, n_s = batch * seq, dec_batch * dec_seq
    tm_p = 256
    tm_s = 128

    n_seq_all = batch + dec_batch
    pad = -n_seq_all % 8
    c_all = jnp.concatenate([c_prompt, c_sample, jnp.zeros((pad, d), F32)], axis=0)
    ada = _ada_all(c_all, w_ada, b_ada)

    bias_b = [_step_bias(rel_bias, dil, g * DIL_HEADS, (g + 1) * DIL_HEADS) for g, (_, dil) in enumerate(DIL_GROUPS)]
    bias_c = _step_bias(rel_bias, 1, len(DIL_GROUPS) * DIL_HEADS, len(DIL_GROUPS) * DIL_HEADS + SWA_HEADS)
    band_b = [_band_bias(b) for b in bias_b]
    band_c = _band_bias(bias_c)
    caches_b = (cache_b1, cache_b2, cache_b3)
    samp_b = [_sample_bias(bias_b[g], caches_b[g].shape[2], dil, dec_seq) for g, (_, dil) in enumerate(DIL_GROUPS)]
    samp_c = _sample_bias(bias_c, cache_c.shape[2], 1, dec_seq)

    splits = np.cumsum([0, POOL_WIDTH, 3 * DIL_W, 3 * DIL_W, 3 * DIL_W, SWA_QW, SWA_KW, SWA_KW, 3 * d])
    col_slices = [(int(splits[0]), int(splits[1]))]
    for g in range(len(DIL_GROUPS)):
        for part in range(3):
            c0 = int(splits[1 + part]) + g * DIL_W
            col_slices.append((c0, c0 + DIL_W))
    col_slices.append((int(splits[4]), int(splits[8])))

    vec3 = lambda a: a.reshape(depth, 1, -1)
    g1, g2, g3, g4 = vec3(g_pre_mix), vec3(g_post_mix), vec3(g_pre_ffn), vec3(g_post_ffn)
    ps3, rb3 = vec3(pool_scale), vec3(router_bias)
    cb = [c.reshape(c.shape[0], c.shape[1], c.shape[2], -1) for c in (cache_b1, cache_b2, cache_b3, cache_c)]
    pool_halo = jnp.pad(state_pool, ((0, 0), (0, 0), (POOL_HALO - state_pool.shape[2], 0), (0, 0)))

    xp, xs = x_prompt, x_sample
    cache_out = [None, None, None, None]
    st_p = [[], [], [], [], []]
    pool_s = []
    keeps_p = tuple(min(win, seq) for win, _ in DIL_GROUPS) + (min(BLOCK, seq),)
    for l in range(depth):
        w_perm = jnp.concatenate([w_in[l][:, a:b] for a, b in col_slices], axis=1).astype(BF16)
        wb, wc, wo, wr = w_b_up[l].astype(BF16), w_c_up[l].astype(BF16), w_out[l].astype(BF16), w_router[l].astype(BF16)
        wsg, wsu, wsd = ws_gate[l].astype(BF16), ws_up[l].astype(BF16), ws_down[l].astype(BF16)
        pw = pool_w.astype(BF16)
        post_w = (g2, g3, pw, ps3, wb, wc, wo, wr, rb3)
        ada_p = ada[l, :batch].reshape(batch, 1, -1)
        ada_s = ada[l, batch:n_seq_all].reshape(dec_batch, 1, -1)

        (u, q1, kv1, q2, kv2, q3, kv3, qc, kvc, gl, s1, s2, s3, sc) = _in_proj(xp, ada_p, g1, w_perm, l, tm_p, keeps_p)
        attn = []
        for g, ((_, dil), q, kv) in enumerate(zip(DIL_GROUPS, (q1, q2, q3), (kv1, kv2, kv3))):
            attn.extend(_band_attn(q, kv, band_b[g], batch, seq, dil, DIL_HEADS, DIL_HEADS))
        attn.append(_band_attn(qc, kvc, band_c, batch, seq, 1, SWA_HEADS, SWA_KV_HEADS, sinks=sinks[l]))
        xmid, h2, idx, rank, wk, counts = _post_mixer(xp, ada_p, u, u, attn, gl, post_w, l, tm_p, False)
        y_k = _moe_routed(h2, idx, rank, counts, we_gate, we_up, we_down, l, 256)
        xp = _final(xmid, ada_p, g4, h2, y_k, wk, wsg, wsu, wsd, l, tm_p)
        for k, (s, keep) in enumerate(zip((s1, s2, s3, sc), keeps_p)):
            st_p[k].append(s.reshape(batch, keep, 2, -1, HEAD_DIM))
        st_p[4].append(u[:, seq - (POOL_HALO - 1):])

        (u, q1, kv1, q2, kv2, q3, kv3, qc, kvc, gl, s1, s2, s3, sc) = _in_proj(
            xs, ada_s, g1, w_perm, l, tm_s, (dec_seq,) * 4)
        attn = []
        for g, (q, new) in enumerate(zip((q1, q2, q3), (s1, s2, s3))):
            o, lse, cache_out[g] = _sample_attn(q, new, cb[g], cache_out[g], samp_b[g][0], samp_b[g][1], l,
                                                DIL_HEADS, DIL_HEADS, dec_seq)
            attn.extend([o, lse])
        o, cache_out[3] = _sample_attn(qc, sc, cb[3], cache_out[3], samp_c[0], samp_c[1], l,
                                       SWA_HEADS, SWA_KV_HEADS, dec_seq, sinks=sinks[l])
        attn.append(o)
        xmid, h2, idx, rank, wk, counts = _post_mixer(xs, ada_s, u, pool_halo[l], attn, gl, post_w, l, tm_s, True)
        y_k = _moe_routed(h2, idx, rank, counts, we_gate, we_up, we_down, l, 128)
        xs = _final(xmid, ada_s, g4, h2, y_k, wk, wsg, wsu, wsd, l, tm_s)
        pool_s.append(jnp.concatenate([state_pool[l], u], axis=1)[:, -(POOL_HALO - 1):])

    b1_p, b2_p, b3_p, c_p, pool_p = [jnp.stack(s, axis=0) for s in st_p]
    outs_s = [co.reshape(c.shape) for co, c in zip(cache_out, (cache_b1, cache_b2, cache_b3, cache_c))]
    return (xp, xs, b1_p, b2_p, b3_p, c_p, pool_p, outs_s[0], outs_s[1], outs_s[2], outs_s[3], jnp.stack(pool_s, axis=0))
```

```python
import functools

import numpy as np
import jax
import jax.numpy as jnp
from jax import lax
from jax.experimental import pallas as pl
from jax.experimental.pallas import tpu as pltpu
from jax.experimental.pallas import tpu_sc as plsc

F32 = jnp.float32
BF16 = jnp.bfloat16

HEAD_DIM = 64
SCALE = HEAD_DIM ** -0.5
BLOCK = 128
POOL_WINDOWS = (2, 4, 8, 16)
POOL_CH = 128
POOL_WIDTH = len(POOL_WINDOWS) * POOL_CH
POOL_HALO = 16
DIL_GROUPS = ((128, 1), (512, 4), (2048, 16))
DIL_HEADS = 4
DIL_W = DIL_HEADS * HEAD_DIM
SWA_HEADS = 8
SWA_KV_HEADS = 2
SWA_QW = SWA_HEADS * HEAD_DIM
SWA_KW = SWA_KV_HEADS * HEAD_DIM
N_BUCKETS = 32
MAX_DISTANCE = 2048
N_EXPERTS = 64
TOP_K = 8
ROUTED_SCALE = 2.5
EPS = 1e-6
NEG_INF = float("-inf")

V7X_VMEM_BYTES = 64 * 1024 * 1024
VMEM_LIMIT = 56 * 1024 * 1024


def _cparams(sem):
    return pltpu.CompilerParams(dimension_semantics=sem, vmem_limit_bytes=VMEM_LIMIT)


def _rms(x, g):
    ms = jnp.mean(x * x, axis=-1, keepdims=True)
    return x * lax.rsqrt(ms + EPS) * g


def _sigmoid(x):
    return 1.0 / (1.0 + jnp.exp(-x))


def _silu(x):
    return x * _sigmoid(x)


def _dot(a, b):
    return jnp.dot(a, b, preferred_element_type=F32)


def _dot_nt(a, b):
    return lax.dot_general(a, b, (((1,), (1,)), ((), ())), preferred_element_type=F32)


def _tile_split(tseq, tm):
    if tseq >= tm:
        return 1, tm, tseq // tm
    return tm // tseq, tseq, 1


ROW_SUB = 4
ROW_LANES = 128
ROW_DTYPE = jnp.uint32
HI_MASK = 0xFFFF0000


def _pack_chunks(x):
    half = x.shape[1] // 2
    hi = pltpu.bitcast(x[:, :half].astype(BF16).astype(F32), jnp.uint32) & jnp.uint32(HI_MASK)
    lo = pltpu.bitcast(x[:, half:].astype(BF16).astype(F32), jnp.uint32) >> 16
    w = hi | lo
    return [w[:, j * ROW_LANES:(j + 1) * ROW_LANES] for j in range(ROW_SUB)]


def _unpack_chunks(chunks):
    his = [pltpu.bitcast(w & jnp.uint32(HI_MASK), F32) for w in chunks]
    los = [pltpu.bitcast(w << 16, F32) for w in chunks]
    return jnp.concatenate(his + los, axis=1)


def _store_rows(ref, chunks, rows, first=0):
    for j, c in enumerate(chunks):
        ref[pl.ds(first * ROW_SUB + j, rows, stride=ROW_SUB), :] = c


def _load_rows(ref, rows, first=0, lead=None):
    if lead is None:
        return [ref[pl.ds(first * ROW_SUB + j, rows, stride=ROW_SUB), :] for j in range(ROW_SUB)]
    return [ref[lead, pl.ds(first * ROW_SUB + j, rows, stride=ROW_SUB), :] for j in range(ROW_SUB)]


SC_CORES = 2
SC_SUBCORES = 16
SC_WORKERS = SC_CORES * SC_SUBCORES
SC_CHUNK = 64


def _sc_scatter_rows(src, dest, n_out):
    n, sub, lanes = src.shape
    kk = dest.shape[1]
    ch = min(SC_CHUNK, n // SC_WORKERS)
    n_chunks = n // (SC_WORKERS * ch)
    idx3 = dest.reshape(n // ch, ch, kk).transpose(0, 2, 1)
    mesh = plsc.VectorSubcoreMesh(core_axis_name="c", subcore_axis_name="s")

    @functools.partial(
        pl.kernel, mesh=mesh,
        out_type=jax.ShapeDtypeStruct((n_out, sub, lanes), src.dtype),
        scratch_types=[pltpu.VMEM((kk, ch), jnp.int32), pltpu.VMEM((ch, sub, lanes), src.dtype),
                       pltpu.SemaphoreType.DMA],
    )
    def scatter_kernel(src_hbm, idx_hbm, out_hbm, idx_v, rows_v, sem):
        wid = lax.axis_index("s") * SC_CORES + lax.axis_index("c")

        @pl.loop(0, n_chunks)
        def _(c):
            j = wid * n_chunks + c
            pltpu.sync_copy(idx_hbm.at[j], idx_v)
            pltpu.sync_copy(src_hbm.at[pl.ds(j * ch, ch)], rows_v)
            copies = [pltpu.async_copy(rows_v, out_hbm.at[idx_v.at[q]], sem) for q in range(kk)]
            for cp in copies:
                cp.wait()

    return scatter_kernel(src, idx3)


def _sc_gather_rows(table, dest):
    v, sub, lanes = table.shape
    n, kk = dest.shape
    total = n * kk
    ch = min(SC_CHUNK, total // SC_WORKERS)
    n_chunks = total // (SC_WORKERS * ch)
    idx2 = dest.T.reshape(total // ch, ch)
    mesh = plsc.VectorSubcoreMesh(core_axis_name="c", subcore_axis_name="s")

    @functools.partial(
        pl.kernel, mesh=mesh,
        out_type=jax.ShapeDtypeStruct((total, sub, lanes), table.dtype),
        scratch_types=[pltpu.VMEM((1, ch), jnp.int32), pltpu.VMEM((ch, sub, lanes), table.dtype),
                       pltpu.SemaphoreType.DMA],
    )
    def gather_kernel(tab_hbm, idx_hbm, out_hbm, idx_v, rows_v, sem):
        wid = lax.axis_index("s") * SC_CORES + lax.axis_index("c")

        @pl.loop(0, n_chunks)
        def _(c):
            j = wid * n_chunks + c
            pltpu.sync_copy(idx_hbm.at[pl.ds(j, 1)], idx_v)
            pltpu.async_copy(tab_hbm.at[idx_v.at[0]], rows_v, sem).wait()
            pltpu.sync_copy(rows_v, out_hbm.at[pl.ds(j * ch, ch)])

    return gather_kernel(table, idx2).reshape(kk, n, sub, lanes)


def _ada_kernel(c_ref, w_ref, b_ref, o_ref):
    c = _silu(c_ref[...]).astype(BF16)
    o_ref[0] = _dot(c, w_ref[0].astype(BF16)) + b_ref[0]


def _ada_all(c_all, w_ada, b_ada):
    depth, d, n = w_ada.shape
    rows = c_all.shape[0]
    tn = 1536
    return pl.pallas_call(
        _ada_kernel,
        grid=(depth, n // tn),
        in_specs=[
            pl.BlockSpec((rows, d), lambda l, j: (0, 0)),
            pl.BlockSpec((1, d, tn), lambda l, j: (l, 0, j)),
            pl.BlockSpec((1, 1, tn), lambda l, j: (l, 0, j)),
        ],
        out_specs=pl.BlockSpec((1, rows, tn), lambda l, j: (l, 0, j)),
        out_shape=jax.ShapeDtypeStruct((depth, rows, n), F32),
        compiler_params=_cparams(("parallel", "parallel")),
        name="ada",
    )(c_all, w_ada, b_ada.reshape(depth, 1, n))


def _in_proj_kernel(x_ref, sh_ref, sc_ref, g_ref, w_ref, *refs, dils, fold):
    s, t, d = x_ref.shape
    rows = s * t
    h = _rms(x_ref[...], g_ref[...]) * (1.0 + sc_ref[...]) + sh_ref[...]
    hb = h.reshape(rows, d).astype(BF16)
    n_g = len(dils)
    if fold:
        u_ref = refs[0]
        q_refs = refs[1:1 + 2 * n_g:2] + (refs[1 + 2 * n_g],)
        kv_refs = refs[2:2 + 2 * n_g:2] + (refs[2 + 2 * n_g],)
        gl_ref = refs[3 + 2 * n_g]
        st_refs = refs[4 + 2 * n_g:5 + 3 * n_g]
        zs_ref = refs[5 + 3 * n_g]
    else:
        u_ref = refs[0]
        q_refs = refs[1:2 + n_g]
        kv_refs = (None,) * (n_g + 1)
        gl_ref = refs[2 + n_g]
        st_refs = refs[3 + n_g:4 + 2 * n_g]
        zs_ref = None

    def proj(c0, c1):
        return _dot(hb, w_ref[:, c0:c1])

    u_ref[...] = proj(0, POOL_WIDTH).reshape(s, t, POOL_WIDTH)
    off = POOL_WIDTH
    for g, dil in enumerate(tuple(dils) + (1,)):
        qw = DIL_W if g < n_g else SWA_QW
        kvw = 2 * DIL_W if g < n_g else 2 * SWA_KW
        z = proj(off, off + qw + kvw)
        off += qw + kvw
        zq = z[:, :qw] * SCALE
        zkv = z[:, qw:]
        st = st_refs[g]
        st[...] = zkv[rows - st.shape[-1]:, :].T
        if not fold:
            q_refs[g][...] = zq
        elif dil == 1:
            q_refs[g][0, 0] = zq.astype(BF16)
            kv_refs[g][0, 0] = zkv.astype(BF16)
        else:
            zf = jnp.concatenate([zq, zkv], axis=1)
            n_c = zf.shape[1] // BLOCK
            for c in range(n_c):
                zs_ref[c] = zf[:, c * BLOCK:(c + 1) * BLOCK]
            n = rows // dil
            for r in range(dil):
                part = jnp.concatenate([zs_ref[c, pl.ds(r, n, stride=dil), :] for c in range(n_c)], axis=1)
                q_refs[g][0, r] = part[:, :qw].astype(BF16)
                kv_refs[g][0, r] = part[:, qw:].astype(BF16)
    gl_ref[...] = proj(off, off + 3 * d)


def _in_proj(x3, ada3, g_pre, w_perm, layer, tm, keeps, fold):
    nseq, tseq, d = x3.shape
    ntok = nseq * tseq
    s_blk, t_blk, tps = _tile_split(tseq, tm)
    n_tiles = ntok // tm
    in_w = w_perm.shape[-1]
    dils = tuple(dil for _, dil in DIL_GROUPS)
    widths = [(DIL_W, 2 * DIL_W)] * len(dils) + [(SWA_QW, 2 * SWA_KW)]

    def x_map(i):
        return (i // tps, i % tps, 0)

    def tok_map(i):
        return (i, 0)

    out_shapes = [jax.ShapeDtypeStruct((nseq, tseq, POOL_WIDTH), F32)]
    out_specs = [pl.BlockSpec((s_blk, t_blk, POOL_WIDTH), x_map)]
    for (qw, kvw), dil in zip(widths, dils + (1,)):
        if fold:
            for w in (qw, kvw):
                out_shapes.append(jax.ShapeDtypeStruct((nseq, dil, tseq // dil, w), BF16))
                out_specs.append(pl.BlockSpec((1, dil, tm // dil, w), lambda i: (i // tps, 0, i % tps, 0)))
        else:
            out_shapes.append(jax.ShapeDtypeStruct((ntok, qw), F32))
            out_specs.append(pl.BlockSpec((tm, qw), tok_map))
    out_shapes.append(jax.ShapeDtypeStruct((ntok, 3 * d), F32))
    out_specs.append(pl.BlockSpec((tm, 3 * d), tok_map))
    for keep, (_, kvw) in zip(keeps, widths):
        if fold:
            sb = min(tm, keep)
            bps = keep // sb
            first = (tseq - keep) // tm

            def st_map(i, bps=bps, first=first):
                return (i // tps, 0, jnp.maximum(i % tps - first, 0) * (1 if bps > 1 else 0))

            out_shapes.append(jax.ShapeDtypeStruct((nseq, kvw, keep), F32))
            out_specs.append(pl.BlockSpec((None, kvw, sb), st_map))
        else:
            out_shapes.append(jax.ShapeDtypeStruct((kvw, ntok), F32))
            out_specs.append(pl.BlockSpec((kvw, tm), lambda i: (0, i)))
    scratch = [pltpu.VMEM((3 * DIL_W // BLOCK, tm, BLOCK), F32)] if fold else []
    return pl.pallas_call(
        functools.partial(_in_proj_kernel, dils=dils, fold=fold),
        grid=(n_tiles,),
        in_specs=[
            pl.BlockSpec((s_blk, t_blk, d), x_map),
            pl.BlockSpec((s_blk, 1, d), lambda i: (i // tps, 0, 0)),
            pl.BlockSpec((s_blk, 1, d), lambda i: (i // tps, 0, 1)),
            pl.BlockSpec((1, 1, d), lambda i: (layer, 0, 0)),
            pl.BlockSpec((d, in_w), lambda i: (0, 0)),
        ],
        out_specs=out_specs,
        out_shape=out_shapes,
        scratch_shapes=scratch,
        compiler_params=_cparams(("arbitrary",)),
        name="in_proj",
    )(x3, ada3, ada3, g_pre, w_perm)


def _band_attn_kernel(*refs, n_heads, n_kv, nq, with_sink):
    if with_sink:
        sink_ref, q_ref, kv_ref, halo_ref, bm_ref, o_ref = refs
        lse_ref = None
    else:
        q_ref, kv_ref, halo_ref, bm_ref, o_ref, lse_ref = refs
        sink_ref = None
    kw = n_kv * HEAD_DIM
    grp = n_heads // n_kv
    chunk = pl.program_id(2)

    def one_block(r0, kv_prev, first):
        qb = q_ref[pl.ds(r0, BLOCK), :]
        kv_cur = kv_ref[pl.ds(r0, BLOCK), :]
        for h in range(n_heads):
            kh = h // grp
            ks = slice(kh * HEAD_DIM, (kh + 1) * HEAD_DIM)
            vs = slice(kw + kh * HEAD_DIM, kw + (kh + 1) * HEAD_DIM)
            hs = slice(h * HEAD_DIM, (h + 1) * HEAD_DIM)
            qh = qb[:, hs]
            s_p = _dot_nt(qh, kv_prev[:, ks]) + bm_ref[h, :, :BLOCK]
            s_c = _dot_nt(qh, kv_cur[:, ks]) + bm_ref[h, :, BLOCK:]
            if first is not None:
                s_p = jnp.where(first, NEG_INF, s_p)
            m = jnp.maximum(jnp.max(s_p, axis=-1, keepdims=True), jnp.max(s_c, axis=-1, keepdims=True))
            if with_sink:
                m = jnp.maximum(m, sink_ref[h])
            p_p = jnp.exp(s_p - m)
            p_c = jnp.exp(s_c - m)
            den = jnp.sum(p_p, axis=-1, keepdims=True) + jnp.sum(p_c, axis=-1, keepdims=True)
            if with_sink:
                den = den + jnp.exp(sink_ref[h] - m)
            o = _dot(p_p.astype(BF16), kv_prev[:, vs]) + _dot(p_c.astype(BF16), kv_cur[:, vs])
            o_ref[pl.ds(r0, BLOCK), hs] = o / den
            if lse_ref is not None:
                lse_ref[pl.ds(r0, BLOCK), hs] = jnp.broadcast_to(m + jnp.log(den), (BLOCK, HEAD_DIM))

    one_block(0, halo_ref[...], chunk == 0)

    def body(j, carry):
        r0 = pl.multiple_of(j * BLOCK, BLOCK)
        one_block(r0, kv_ref[pl.ds(r0 - BLOCK, BLOCK), :], None)
        return carry

    if nq > 1:
        lax.fori_loop(1, nq, body, 0)


def _band_attn(q, kv, bm, n_heads, n_kv, sinks=None):
    batch, dil, fold, qw = q.shape
    kvw = kv.shape[-1]
    nq = min(8, fold // BLOCK)
    rows = nq * BLOCK
    n_chunks = fold // rows
    with_sink = sinks is not None
    in_specs = [
        pl.BlockSpec((None, None, rows, qw), lambda b, r, c: (b, r, c, 0)),
        pl.BlockSpec((None, None, rows, kvw), lambda b, r, c: (b, r, c, 0)),
        pl.BlockSpec((None, None, BLOCK, kvw), lambda b, r, c: (b, r, jnp.maximum(c * nq - 1, 0), 0)),
        pl.BlockSpec((n_heads, BLOCK, 2 * BLOCK), lambda b, r, c: (0, 0, 0)),
    ]
    args = [q, kv, kv, bm]
    o_spec = pl.BlockSpec((None, None, rows, qw), lambda b, r, c: (b, r, c, 0))
    o_shape = jax.ShapeDtypeStruct((batch, dil, fold, qw), F32)
    if with_sink:
        in_specs = [pl.BlockSpec(memory_space=pltpu.SMEM)] + in_specs
        args = [sinks] + args
        out_specs, out_shape = o_spec, o_shape
    else:
        out_specs, out_shape = [o_spec, o_spec], [o_shape, o_shape]
    return pl.pallas_call(
        functools.partial(_band_attn_kernel, n_heads=n_heads, n_kv=n_kv, nq=nq, with_sink=with_sink),
        grid=(batch, dil, n_chunks),
        in_specs=in_specs,
        out_specs=out_specs,
        out_shape=out_shape,
        compiler_params=_cparams(("parallel", "parallel", "parallel")),
        name="band_attn",
    )(*args)


def _sample_attn_kernel(*refs, n_heads, n_kv, sb, t_new, with_sink, aliased):
    refs = list(refs)
    q_ref, newt_ref, cache_ref, bmc_ref, bmn_ref = refs[:5]
    refs = refs[5:]
    sink_ref = refs.pop(0) if with_sink else None
    if aliased:
        refs.pop(0)
    o_ref = refs.pop(0)
    lse_ref = None if with_sink else refs.pop(0)
    cout_ref = refs.pop(0)
    kw = n_kv * HEAD_DIM
    grp = n_heads // n_kv
    w, n_buf = cache_ref.shape[2:]
    first_new = BLOCK - t_new
    per_blk = BLOCK // t_new
    i = pl.program_id(0)
    lane = lax.broadcasted_iota(jnp.int32, (w, BLOCK), 1)

    def one_seq(s, carry):
        r0 = pl.multiple_of(s * t_new, t_new)
        qs = q_ref[pl.ds(r0, t_new), :]
        q_rows = []
        for h in range(n_heads):
            piece = qs[:, h * HEAD_DIM:(h + 1) * HEAD_DIM]
            parts = [piece if k == h // grp else jnp.zeros_like(piece) for k in range(n_kv)]
            q_rows.append(jnp.concatenate(parts, axis=1))
        qbd = jnp.concatenate(q_rows, axis=0).astype(BF16)
        cache = cache_ref[0, s]
        off = ((i * sb + s) % per_blk) * t_new
        placed = pltpu.roll(newt_ref[...], first_new - off, axis=1)
        s_c = _dot(qbd, cache[:kw].astype(BF16)) + bmc_ref[...]
        s_n = _dot(qbd, placed[:kw].astype(BF16)) + bmn_ref[...]
        m = jnp.maximum(jnp.max(s_c, axis=-1, keepdims=True), jnp.max(s_n, axis=-1, keepdims=True))
        if with_sink:
            m = jnp.maximum(m, sink_ref[:, :1])
        p_c = jnp.exp(s_c - m)
        p_n = jnp.exp(s_n - m)
        den = jnp.sum(p_c, axis=-1, keepdims=True) + jnp.sum(p_n, axis=-1, keepdims=True)
        if with_sink:
            den = den + jnp.exp(sink_ref[:, :1] - m)
        o_all = (_dot_nt(p_c.astype(BF16), cache[kw:].astype(BF16))
                 + _dot_nt(p_n.astype(BF16), placed[kw:].astype(BF16))) / den
        outs = []
        for h in range(n_heads):
            kh = h // grp
            outs.append(o_all[h * t_new:(h + 1) * t_new, kh * HEAD_DIM:(kh + 1) * HEAD_DIM])
        o_ref[pl.ds(r0, t_new), :] = jnp.concatenate(outs, axis=1)
        if lse_ref is not None:
            lse = m + jnp.log(den)
            lse_ref[pl.ds(r0, t_new), :] = jnp.concatenate(
                [jnp.broadcast_to(lse[h * t_new:(h + 1) * t_new], (t_new, HEAD_DIM)) for h in range(n_heads)], axis=1)
        rolled = pltpu.roll(cache, n_buf - t_new, axis=1)
        if n_buf > BLOCK:
            cout_ref[0, s, :, : n_buf - BLOCK] = rolled[:, : n_buf - BLOCK]
        cout_ref[0, s, :, n_buf - BLOCK:] = jnp.where(lane >= first_new, placed, rolled[:, n_buf - BLOCK:])
        return carry

    if sb == 1:
        one_seq(0, 0)
    else:
        lax.fori_loop(0, sb, one_seq, 0)


def _sample_attn(q, newt, cache_t, prev_out, bmc, bmn, layer, n_heads, n_kv, t_new, sink_rows=None):
    depth, nseq, w, n_buf = cache_t.shape
    qw = n_heads * HEAD_DIM
    sb = max(1, min(BLOCK // t_new, 2048 // n_buf))
    with_sink = sink_rows is not None
    aliased = prev_out is not None
    rows = sb * t_new
    per_blk = BLOCK // t_new
    in_specs = [
        pl.BlockSpec((rows, qw), lambda i: (i, 0)),
        pl.BlockSpec((w, BLOCK), lambda i: (0, (i * sb) // per_blk)),
        pl.BlockSpec((1, sb, w, n_buf), lambda i: (layer, i, 0, 0)),
        pl.BlockSpec(bmc.shape, lambda i: (0, 0)),
        pl.BlockSpec(bmn.shape, lambda i: (0, 0)),
    ]
    args = [q, newt, cache_t, bmc, bmn]
    if with_sink:
        in_specs.append(pl.BlockSpec(sink_rows.shape, lambda i: (0, 0)))
        args.append(sink_rows)
    aliases = {}
    if aliased:
        aliases = {len(args): 1 if with_sink else 2}
        in_specs.append(pl.BlockSpec(memory_space=pl.ANY))
        args.append(prev_out)
    o_spec = pl.BlockSpec((rows, qw), lambda i: (i, 0))
    o_shape = jax.ShapeDtypeStruct((nseq * t_new, qw), F32)
    c_spec = pl.BlockSpec((1, sb, w, n_buf), lambda i: (layer, i, 0, 0))
    c_shape = jax.ShapeDtypeStruct(cache_t.shape, F32)
    if with_sink:
        out_specs, out_shape = [o_spec, c_spec], [o_shape, c_shape]
    else:
        out_specs, out_shape = [o_spec, o_spec, c_spec], [o_shape, o_shape, c_shape]
    return pl.pallas_call(
        functools.partial(_sample_attn_kernel, n_heads=n_heads, n_kv=n_kv, sb=sb, t_new=t_new,
                          with_sink=with_sink, aliased=aliased),
        grid=(nseq // sb,),
        in_specs=in_specs,
        out_specs=out_specs,
        out_shape=out_shape,
        input_output_aliases=aliases,
        compiler_params=_cparams(("parallel",)),
        name="sample_attn",
    )(*args)


def _post_kernel(x_ref, gt_ref, shf_ref, scf_ref, g2_ref, g3_ref, u_ref, halo_ref,
                 o1_ref, l1_ref, o2_ref, l2_ref, o3_ref, l3_ref, oc_ref, gl_ref,
                 pw_ref, ps_ref, wb_ref, wc_ref, wo_ref, wr_ref, rb_ref,
                 xmid_ref, h2_ref, idx_ref, rank_ref, wk_ref, counts_ref, cnt_ref, *unfold_refs,
                 tps, full_windows, dils):
    s, t, d = x_ref.shape
    rows = s * t
    i = pl.program_id(0)

    u = u_ref[...]
    halo = halo_ref[...]
    if not full_windows:
        halo = jnp.where(i % tps == 0, 0.0, halo)
    ue = jnp.concatenate([halo, u], axis=1)
    if full_windows:
        row = None
    else:
        row = (i % tps) * t + lax.broadcasted_iota(jnp.int32, (1, t, 1), 1)
    parts = []
    for g, win in enumerate(POOL_WINDOWS):
        cs = slice(g * POOL_CH, (g + 1) * POOL_CH)
        acc = ue[:, :, cs]
        base = 0
        span = 1
        while span < win:
            acc = acc[:, span:, :] + acc[:, : acc.shape[1] - span, :]
            base += span
            span *= 2
        tot = acc[:, POOL_HALO - base:, :]
        if full_windows:
            mean = tot / float(win)
        else:
            cnt = jnp.minimum(row + 1, win).astype(F32)
            mean = tot / cnt
        zg = (mean - u[:, :, cs]).reshape(rows, POOL_CH).astype(BF16)
        parts.append(_dot(zg, pw_ref[0, g]))
    a = jnp.concatenate(parts, axis=-1) * ps_ref[0]

    scratch = list(unfold_refs)

    def token_order(ref, dil):
        if dil == 1:
            return ref[...]
        scr = scratch.pop(0)
        n = rows // dil
        n_c = scr.shape[0]
        for r in range(dil):
            part = ref[0, r]
            for c in range(n_c):
                scr[c, pl.ds(r, n, stride=dil), :] = part[:, c * BLOCK:(c + 1) * BLOCK]
        return jnp.concatenate([scr[c] for c in range(n_c)], axis=1)

    o1, l1 = token_order(o1_ref, dils[0]), token_order(l1_ref, dils[0])
    o2, l2 = token_order(o2_ref, dils[1]), token_order(l2_ref, dils[1])
    o3, l3 = token_order(o3_ref, dils[2]), token_order(l3_ref, dils[2])

    lm = jnp.maximum(jnp.maximum(l1, l2), l3)
    e1, e2, e3 = jnp.exp(l1 - lm), jnp.exp(l2 - lm), jnp.exp(l3 - lm)
    esum = e1 + e2 + e3
    bmix = (e1 / esum) * o1 + (e2 / esum) * o2 + (e3 / esum) * o3
    b = _dot(bmix.astype(BF16), wb_ref[...])
    c = _dot(oc_ref[...].astype(BF16), wc_ref[...])
    g_a = _sigmoid(gl_ref[:, :d])
    g_b = _sigmoid(gl_ref[:, d:2 * d])
    g_c = _sigmoid(gl_ref[:, 2 * d:])
    mix = _dot((g_a * a + g_b * b + g_c * c).astype(BF16), wo_ref[...])

    x = x_ref[...]
    xm = x + gt_ref[...] * _rms(mix, g2_ref[0]).reshape(s, t, d)
    xmid_ref[...] = xm
    h2f = (_rms(xm, g3_ref[...]) * (1.0 + scf_ref[...]) + shf_ref[...]).reshape(rows, d)
    h2 = h2f.astype(BF16)
    _store_rows(h2_ref, _pack_chunks(h2f), rows)

    scores = _sigmoid(_dot(h2, wr_ref[...]))
    work = scores + rb_ref[0]
    lane = lax.broadcasted_iota(jnp.int32, (rows, N_EXPERTS), 1).astype(F32)
    sel = jnp.zeros((rows, N_EXPERTS), F32)
    picks = []
    for _ in range(TOP_K):
        mx = jnp.max(work, axis=-1, keepdims=True)
        pick = jnp.min(jnp.where(work == mx, lane, float(N_EXPERTS)), axis=-1, keepdims=True)
        hit = lane == pick
        sel = jnp.where(hit, 1.0, sel)
        work = jnp.where(hit, NEG_INF, work)
        picks.append(pick)
    top_s = scores * sel
    wmat = top_s / jnp.sum(top_s, axis=-1, keepdims=True) * ROUTED_SCALE

    @pl.when(i == 0)
    def _():
        cnt_ref[...] = jnp.zeros_like(cnt_ref)

    r_i = lax.broadcasted_iota(jnp.int32, (rows, rows), 0)
    c_i = lax.broadcasted_iota(jnp.int32, (rows, rows), 1)
    below = jnp.where(c_i < r_i, 1.0, 0.0).astype(BF16)
    rank_all = cnt_ref[...] + _dot(below, sel.astype(BF16))
    cnt_ref[...] = cnt_ref[...] + jnp.sum(sel, axis=0, keepdims=True)
    counts_ref[...] = cnt_ref[...].astype(jnp.int32)

    kcol = lax.broadcasted_iota(jnp.int32, (rows, TOP_K), 1)
    idx = jnp.zeros((rows, TOP_K), F32)
    rank = jnp.zeros((rows, TOP_K), F32)
    wk = jnp.zeros((rows, TOP_K), F32)
    for k, pick in enumerate(picks):
        hit = lane == pick
        idx = jnp.where(kcol == k, pick, idx)
        rank = jnp.where(kcol == k, jnp.sum(jnp.where(hit, rank_all, 0.0), axis=-1, keepdims=True), rank)
        wk = jnp.where(kcol == k, jnp.sum(jnp.where(hit, wmat, 0.0), axis=-1, keepdims=True), wk)
    idx_ref[...] = idx.astype(jnp.int32)
    rank_ref[...] = rank.astype(jnp.int32)
    wk_ref[...] = wk


def _post_mixer(x3, ada3, u3, halo3, attn, gl, weights, layer, tm, full_windows, dils):
    nseq, tseq, d = x3.shape
    ntok = nseq * tseq
    s_blk, t_blk, tps = _tile_split(tseq, tm)
    n_tiles = ntok // tm
    (g2, g3, pool_w, pool_scale, w_b_up, w_c_up, w_out, w_router, router_bias) = weights

    def x_map(i):
        return (i // tps, i % tps, 0)

    def tok_map(i):
        return (i, 0)

    def ada_spec(j):
        return pl.BlockSpec((s_blk, 1, d), lambda i: (i // tps, 0, j))

    if full_windows:
        halo_spec = pl.BlockSpec((s_blk, POOL_HALO, POOL_WIDTH), lambda i: (i, 0, 0))
    else:
        hb = t_blk // POOL_HALO
        halo_spec = pl.BlockSpec((1, POOL_HALO, POOL_WIDTH),
                                 lambda i: (i // tps, jnp.maximum((i % tps) * hb - 1, 0), 0))
    vec = pl.BlockSpec((1, 1, d), lambda i: (layer, 0, 0))
    attn_specs = []
    for g, dil in enumerate(dils):
        if dil == 1:
            spec = pl.BlockSpec((tm, DIL_W), tok_map)
        else:
            spec = pl.BlockSpec((1, dil, tm // dil, DIL_W), lambda i: (i // tps, 0, i % tps, 0))
        attn_specs += [spec, spec]
    in_specs = [
        pl.BlockSpec((s_blk, t_blk, d), x_map), ada_spec(2), ada_spec(3), ada_spec(4), vec, vec,
        pl.BlockSpec((s_blk, t_blk, POOL_WIDTH), x_map), halo_spec,
    ] + attn_specs + [
        pl.BlockSpec((tm, SWA_QW), tok_map), pl.BlockSpec((tm, 3 * d), tok_map),
        pl.BlockSpec((1,) + pool_w.shape[1:], lambda i: (layer, 0, 0, 0)),
        vec,
        pl.BlockSpec(w_b_up.shape, lambda i: (0, 0)), pl.BlockSpec(w_c_up.shape, lambda i: (0, 0)),
        pl.BlockSpec(w_out.shape, lambda i: (0, 0)), pl.BlockSpec(w_router.shape, lambda i: (0, 0)),
        pl.BlockSpec((1, 1, N_EXPERTS), lambda i: (layer, 0, 0)),
    ]
    out_shape = [
        jax.ShapeDtypeStruct((nseq, tseq, d), F32),
        jax.ShapeDtypeStruct((ntok * ROW_SUB, ROW_LANES), ROW_DTYPE),
        jax.ShapeDtypeStruct((ntok, TOP_K), jnp.int32),
        jax.ShapeDtypeStruct((ntok, TOP_K), jnp.int32),
        jax.ShapeDtypeStruct((ntok, TOP_K), F32),
        jax.ShapeDtypeStruct((1, N_EXPERTS), jnp.int32),
    ]
    out_specs = [
        pl.BlockSpec((s_blk, t_blk, d), x_map),
        pl.BlockSpec((tm * ROW_SUB, ROW_LANES), tok_map),
        pl.BlockSpec((tm, TOP_K), tok_map),
        pl.BlockSpec((tm, TOP_K), tok_map),
        pl.BlockSpec((tm, TOP_K), tok_map),
        pl.BlockSpec((1, N_EXPERTS), lambda i: (0, 0)),
    ]
    n_unfold = 2 * sum(1 for dil in dils if dil > 1)
    scratch = [pltpu.VMEM((1, N_EXPERTS), F32)] + [pltpu.VMEM((DIL_W // BLOCK, tm, BLOCK), F32)] * n_unfold
    return pl.pallas_call(
        functools.partial(_post_kernel, tps=tps, full_windows=full_windows, dils=tuple(dils)),
        grid=(n_tiles,),
        in_specs=in_specs,
        out_specs=out_specs,
        out_shape=out_shape,
        scratch_shapes=scratch,
        compiler_params=_cparams(("arbitrary",)),
        name="post_mixer",
    )(x3, ada3, ada3, ada3, g2, g3, u3, halo3, *attn, gl,
      pool_w, pool_scale, w_b_up, w_c_up, w_out, w_router, router_bias)


EXPERT_ROWS = 256


def _expert_kernel(bexp_ref, nvalid_ref, nused_ref, x_ref, wg_ref, wu_ref, wd_ref, y_ref, wg_s, wu_s, wd_s):
    i = pl.program_id(0)
    bm = x_ref.shape[0] // ROW_SUB
    sub = min(bm, EXPERT_ROWS)

    @pl.when((i == 0) | (bexp_ref[i] != bexp_ref[jnp.maximum(i - 1, 0)]))
    def _():
        wg_s[...] = wg_ref[0, 0].astype(BF16)
        wu_s[...] = wu_ref[0, 0].astype(BF16)
        wd_s[...] = wd_ref[0, 0].astype(BF16)

    @pl.when(i < nused_ref[0])
    def _():
        for c in range(bm // sub):
            x = _unpack_chunks(_load_rows(x_ref, sub, first=c * sub))
            row = c * sub + lax.broadcasted_iota(jnp.int32, (sub, 1), 0)
            x = jnp.where(row < nvalid_ref[i], x, 0.0).astype(BF16)
            gate = _dot(x, wg_s[...])
            up = _dot(x, wu_s[...])
            y = _dot((_silu(gate) * up).astype(BF16), wd_s[...])
            _store_rows(y_ref, _pack_chunks(y), sub, first=c * sub)

    @pl.when(i >= nused_ref[0])
    def _():
        y_ref[...] = jnp.zeros_like(y_ref)


def _experts(x_rows, block_exp, nvalid, n_used, we_gate, we_up, we_down, layer, bm):
    n_rows = x_rows.shape[0] // ROW_SUB
    d, ff = we_gate.shape[-2:]
    n_blocks = n_rows // bm
    grid_spec = pltpu.PrefetchScalarGridSpec(
        num_scalar_prefetch=3,
        grid=(n_blocks,),
        in_specs=[
            pl.BlockSpec((bm * ROW_SUB, ROW_LANES), lambda i, be, nv, nu: (i, 0)),
            pl.BlockSpec((1, 1, d, ff), lambda i, be, nv, nu: (layer, be[i], 0, 0)),
            pl.BlockSpec((1, 1, d, ff), lambda i, be, nv, nu: (layer, be[i], 0, 0)),
            pl.BlockSpec((1, 1, ff, d), lambda i, be, nv, nu: (layer, be[i], 0, 0)),
        ],
        out_specs=pl.BlockSpec((bm * ROW_SUB, ROW_LANES), lambda i, be, nv, nu: (i, 0)),
        scratch_shapes=[pltpu.VMEM((d, ff), BF16), pltpu.VMEM((d, ff), BF16), pltpu.VMEM((ff, d), BF16)],
    )
    return pl.pallas_call(
        _expert_kernel,
        grid_spec=grid_spec,
        out_shape=jax.ShapeDtypeStruct(x_rows.shape, x_rows.dtype),
        compiler_params=_cparams(("arbitrary",)),
        name="experts",
    )(block_exp, nvalid, n_used, x_rows, we_gate, we_up, we_down)


def _final_kernel(x_ref, gt_ref, g4_ref, h2_ref, yk_ref, wk_ref, wg_ref, wu_ref, wd_ref, o_ref):
    s, t, d = x_ref.shape
    rows = s * t
    h2 = _unpack_chunks(_load_rows(h2_ref, rows)).astype(BF16)
    f = _dot((_silu(_dot(h2, wg_ref[...])) * _dot(h2, wu_ref[...])).astype(BF16), wd_ref[...])
    wk = wk_ref[...]
    for k in range(TOP_K):
        f = f + wk[:, k:k + 1] * _unpack_chunks(_load_rows(yk_ref, rows, lead=k))
    o_ref[...] = x_ref[...] + gt_ref[...] * _rms(f, g4_ref[0]).reshape(s, t, d)


def _final(xmid3, ada3, g4, h2_rows, yk_rows, wk, ws_gate, ws_up, ws_down, layer, tm):
    nseq, tseq, d = xmid3.shape
    ntok = nseq * tseq
    s_blk, t_blk, tps = _tile_split(tseq, tm)

    def x_map(i):
        return (i // tps, i % tps, 0)

    return pl.pallas_call(
        _final_kernel,
        grid=(ntok // tm,),
        in_specs=[
            pl.BlockSpec((s_blk, t_blk, d), x_map),
            pl.BlockSpec((s_blk, 1, d), lambda i: (i // tps, 0, 5)),
            pl.BlockSpec((1, 1, d), lambda i: (layer, 0, 0)),
            pl.BlockSpec((tm * ROW_SUB, ROW_LANES), lambda i: (i, 0)),
            pl.BlockSpec((TOP_K, tm * ROW_SUB, ROW_LANES), lambda i: (0, i, 0)),
            pl.BlockSpec((tm, TOP_K), lambda i: (i, 0)),
            pl.BlockSpec(ws_gate.shape, lambda i: (0, 0)),
            pl.BlockSpec(ws_up.shape, lambda i: (0, 0)),
            pl.BlockSpec(ws_down.shape, lambda i: (0, 0)),
        ],
        out_specs=pl.BlockSpec((s_blk, t_blk, d), x_map),
        out_shape=jax.ShapeDtypeStruct((nseq, tseq, d), F32),
        compiler_params=_cparams(("parallel",)),
        name="final",
    )(xmid3, ada3, g4, h2_rows, yk_rows, wk, ws_gate, ws_up, ws_down)


def _t5_bucket(dist):
    n = np.asarray(dist, dtype=np.int64)
    exact = N_BUCKETS // 2
    log_ratio = np.log(np.maximum(n, 1) / exact) / np.log(MAX_DISTANCE / exact)
    large = np.minimum(exact + (log_ratio * (N_BUCKETS - exact)).astype(np.int64), N_BUCKETS - 1)
    return np.where(n < exact, n, large).astype(np.int32)


def _bias_table(rel_bias, h0, h1, steps, dil):
    steps = np.asarray(steps)
    buckets = np.where(steps >= 0, _t5_bucket(np.maximum(steps, 0) * dil), -1)
    onehot = buckets[..., None] == np.arange(N_BUCKETS)
    table = rel_bias[:, h0:h1].T.astype(F32).reshape((h1 - h0,) + (1,) * steps.ndim + (N_BUCKETS,))
    val = jnp.sum(jnp.where(onehot[None], table, 0.0), axis=-1)
    return jnp.where((steps >= 0)[None], val, NEG_INF)


def _band_steps():
    dist = BLOCK + np.arange(BLOCK)[:, None] - np.arange(2 * BLOCK)[None, :]
    return np.where((dist >= 0) & (dist <= BLOCK), dist, -1)


def _sample_steps(n_buf, dil, t_new):
    t = np.arange(t_new)[:, None]
    delta_c = n_buf + t - np.arange(n_buf)[None, :]
    ok_c = (delta_c % dil == 0) & (delta_c // dil <= BLOCK)
    lane = np.arange(BLOCK)[None, :]
    delta_n = t - (lane - (BLOCK - t_new))
    ok_n = (lane >= BLOCK - t_new) & (delta_n >= 0) & (delta_n % dil == 0) & (delta_n // dil <= BLOCK)
    return np.where(ok_c, delta_c // dil, -1), np.where(ok_n, delta_n // dil, -1)


def _route_plan(idx, rank, counts, bm):
    n_tok = idx.shape[0]
    n_blocks = -(-(n_tok * TOP_K + N_EXPERTS * (bm - 1)) // bm)
    counts = counts.reshape(N_EXPERTS)
    padded = (counts + bm - 1) // bm * bm
    pad_end = jnp.cumsum(padded)
    pad_start = pad_end - padded
    experts = jnp.arange(N_EXPERTS, dtype=jnp.int32)
    onehot = idx[:, :, None] == experts[None, None, :]
    dest = rank + jnp.sum(jnp.where(onehot, pad_start[None, None, :], 0), axis=-1)
    starts = jnp.arange(n_blocks, dtype=jnp.int32) * bm
    block_exp = jnp.minimum(jnp.sum((starts[:, None] >= pad_end[None, :]).astype(jnp.int32), axis=1), N_EXPERTS - 1)
    hot = block_exp[:, None] == experts[None, :]
    blk_cnt = jnp.sum(jnp.where(hot, counts[None, :], 0), axis=1)
    blk_start = jnp.sum(jnp.where(hot, pad_start[None, :], 0), axis=1)
    nvalid = jnp.clip(blk_cnt - (starts - blk_start), 0, bm).astype(jnp.int32)
    n_used = (pad_end[-1] // bm).astype(jnp.int32).reshape(1)
    return dest.astype(jnp.int32), block_exp.astype(jnp.int32), nvalid, n_used, n_blocks * bm


def _moe_routed(h2_rows, idx, rank, counts, we_gate, we_up, we_down, layer, bm):
    n_tok = idx.shape[0]
    dest, block_exp, nvalid, n_used, n_rows = _route_plan(idx, rank, counts, bm)
    x_sorted = _sc_scatter_rows(h2_rows.reshape(n_tok, ROW_SUB, ROW_LANES), dest, n_rows)
    y_sorted = _experts(x_sorted.reshape(n_rows * ROW_SUB, ROW_LANES), block_exp, nvalid, n_used,
                        we_gate, we_up, we_down, layer, bm)
    yk = _sc_gather_rows(y_sorted.reshape(n_rows, ROW_SUB, ROW_LANES), dest)
    return yk.reshape(TOP_K, n_tok * ROW_SUB, ROW_LANES)


def _to_time_minor(c):
    depth, nseq, n_buf = c.shape[:3]
    return jnp.transpose(c, (0, 1, 3, 4, 5, 2)).reshape(depth, nseq, -1, n_buf)


def _from_time_minor(ct, n_heads):
    lead = ct.shape[:-2]
    rows = ct.shape[-1]
    nl = len(lead)
    x = ct.reshape(lead + (2, n_heads, HEAD_DIM, rows))
    return jnp.transpose(x, tuple(range(nl)) + (nl + 3, nl, nl + 1, nl + 2))


def kernel(x_prompt, x_sample, cache_b1, cache_b2, cache_b3, cache_c, state_pool, c_prompt, c_sample, rel_bias, w_ada, b_ada, g_pre_mix, g_post_mix, g_pre_ffn, g_post_ffn, w_in, pool_w, pool_scale, w_b_up, w_c_up, sinks, w_out, w_router, router_bias, we_gate, we_up, we_down, ws_gate, ws_up, ws_down):
    batch, seq, d = x_prompt.shape
    dec_batch, dec_seq, _ = x_sample.shape
    depth = w_in.shape[0]
    tm_p = 256
    tm_s = 128
    n_dil = len(DIL_GROUPS)
    dils = tuple(dil for _, dil in DIL_GROUPS)

    n_seq_all = batch + dec_batch
    pad = -n_seq_all % 8
    c_all = jnp.concatenate([c_prompt, c_sample, jnp.zeros((pad, d), F32)], axis=0)
    ada = _ada_all(c_all, w_ada, b_ada)

    head0 = [g * DIL_HEADS for g in range(n_dil)] + [n_dil * DIL_HEADS]
    head1 = [(g + 1) * DIL_HEADS for g in range(n_dil)] + [n_dil * DIL_HEADS + SWA_HEADS]
    band = [_bias_table(rel_bias, head0[g], head1[g], _band_steps(), dil) for g, dil in enumerate(dils + (1,))]
    caches = (cache_b1, cache_b2, cache_b3, cache_c)
    samp = []
    for g, dil in enumerate(dils + (1,)):
        n_buf = caches[g].shape[2]
        sc_steps, sn_steps = _sample_steps(n_buf, dil, dec_seq)
        rows = (head1[g] - head0[g]) * dec_seq
        samp.append((_bias_table(rel_bias, head0[g], head1[g], sc_steps, dil).reshape(rows, n_buf),
                     _bias_table(rel_bias, head0[g], head1[g], sn_steps, dil).reshape(rows, BLOCK)))

    splits = np.cumsum([0, POOL_WIDTH, 3 * DIL_W, 3 * DIL_W, 3 * DIL_W, SWA_QW, SWA_KW, SWA_KW, 3 * d])
    col_slices = [(int(splits[0]), int(splits[1]))]
    for g in range(n_dil):
        for part in range(3):
            c0 = int(splits[1 + part]) + g * DIL_W
            col_slices.append((c0, c0 + DIL_W))
    col_slices.append((int(splits[4]), int(splits[8])))

    vec3 = lambda a: a.reshape(depth, 1, -1)
    g1, g2, g3, g4 = vec3(g_pre_mix), vec3(g_post_mix), vec3(g_pre_ffn), vec3(g_post_ffn)
    ps3, rb3 = vec3(pool_scale), vec3(router_bias)
    cache_t = [_to_time_minor(c) for c in caches]
    pool_halo = jnp.pad(state_pool, ((0, 0), (0, 0), (POOL_HALO - state_pool.shape[2], 0), (0, 0)))
    pw = pool_w.astype(BF16)

    xp, xs = x_prompt, x_sample
    cache_out = [None] * (n_dil + 1)
    st_p = [[] for _ in range(n_dil + 2)]
    pool_s = []
    keeps_p = tuple(min(win, seq) for win, _ in DIL_GROUPS) + (min(BLOCK, seq),)
    n_heads = (DIL_HEADS,) * n_dil + (SWA_HEADS,)
    n_kvs = (DIL_HEADS,) * n_dil + (SWA_KV_HEADS,)
    for l in range(depth):
        w_perm = jnp.concatenate([w_in[l][:, a:b] for a, b in col_slices], axis=1).astype(BF16)
        wb, wc, wo, wr = w_b_up[l].astype(BF16), w_c_up[l].astype(BF16), w_out[l].astype(BF16), w_router[l].astype(BF16)
        wsg, wsu, wsd = ws_gate[l].astype(BF16), ws_up[l].astype(BF16), ws_down[l].astype(BF16)
        post_w = (g2, g3, pw, ps3, wb, wc, wo, wr, rb3)
        ada_p = ada[l, :batch].reshape(batch, 1, -1)
        ada_s = ada[l, batch:n_seq_all].reshape(dec_batch, 1, -1)
        sink_rows = jnp.broadcast_to(jnp.repeat(sinks[l].astype(F32), dec_seq)[:, None], (SWA_HEADS * dec_seq, BLOCK))

        outs = _in_proj(xp, ada_p, g1, w_perm, l, tm_p, keeps_p, True)
        u, gl, states = outs[0], outs[3 + 2 * n_dil], outs[4 + 2 * n_dil:]
        attn = []
        for g in range(n_dil):
            o, lse = _band_attn(outs[1 + 2 * g], outs[2 + 2 * g], band[g], DIL_HEADS, DIL_HEADS)
            if dils[g] == 1:
                o, lse = o.reshape(batch * seq, DIL_W), lse.reshape(batch * seq, DIL_W)
            attn += [o, lse]
        oc = _band_attn(outs[1 + 2 * n_dil], outs[2 + 2 * n_dil], band[n_dil], SWA_HEADS, SWA_KV_HEADS, sinks=sinks[l])
        attn.append(oc.reshape(batch * seq, SWA_QW))
        xmid, h2, idx, rank, wk, counts = _post_mixer(xp, ada_p, u, u, attn, gl, post_w, l, tm_p, False, dils)
        y_k = _moe_routed(h2, idx, rank, counts, we_gate, we_up, we_down, l, 512)
        xp = _final(xmid, ada_p, g4, h2, y_k, wk, wsg, wsu, wsd, l, tm_p)
        for k in range(n_dil + 1):
            st_p[k].append(_from_time_minor(states[k], n_kvs[k]))
        st_p[n_dil + 1].append(u[:, seq - (POOL_HALO - 1):])

        outs = _in_proj(xs, ada_s, g1, w_perm, l, tm_s, (dec_seq,) * (n_dil + 1), False)
        u, gl, states = outs[0], outs[2 + n_dil], outs[3 + n_dil:]
        attn = []
        for g in range(n_dil + 1):
            res = _sample_attn(outs[1 + g], states[g], cache_t[g], cache_out[g], samp[g][0], samp[g][1], l,
                               n_heads[g], n_kvs[g], dec_seq, sink_rows=sink_rows if g == n_dil else None)
            attn += list(res[:-1])
            cache_out[g] = res[-1]
        xmid, h2, idx, rank, wk, counts = _post_mixer(xs, ada_s, u, pool_halo[l], attn, gl, post_w, l, tm_s, True,
                                                      (1,) * n_dil)
        y_k = _moe_routed(h2, idx, rank, counts, we_gate, we_up, we_down, l, 128)
        xs = _final(xmid, ada_s, g4, h2, y_k, wk, wsg, wsu, wsd, l, tm_s)
        pool_s.append(jnp.concatenate([state_pool[l], u], axis=1)[:, -(POOL_HALO - 1):])

    b1_p, b2_p, b3_p, c_p, pool_p = [jnp.stack(s, axis=0) for s in st_p]
    outs_s = [_from_time_minor(co, n_kvs[g]) for g, co in enumerate(cache_out)]
    return (xp, xs, b1_p, b2_p, b3_p, c_p, pool_p, outs_s[0], outs_s[1], outs_s[2], outs_s[3], jnp.stack(pool_s, axis=0))
```

```python
import functools

import numpy as np
import jax
import jax.numpy as jnp
from jax import lax
from jax.experimental import pallas as pl
from jax.experimental.pallas import tpu as pltpu
from jax.experimental.pallas import tpu_sc as plsc

F32 = jnp.float32
BF16 = jnp.bfloat16

HEAD_DIM = 64
SCALE = HEAD_DIM ** -0.5
BLOCK = 128
POOL_WINDOWS = (2, 4, 8, 16)
POOL_CH = 128
POOL_WIDTH = len(POOL_WINDOWS) * POOL_CH
POOL_HALO = 16
DIL_GROUPS = ((128, 1), (512, 4), (2048, 16))
DIL_HEADS = 4
DIL_W = DIL_HEADS * HEAD_DIM
SWA_HEADS = 8
SWA_KV_HEADS = 2
SWA_QW = SWA_HEADS * HEAD_DIM
SWA_KW = SWA_KV_HEADS * HEAD_DIM
N_BUCKETS = 32
MAX_DISTANCE = 2048
N_EXPERTS = 64
TOP_K = 8
ROUTED_SCALE = 2.5
EPS = 1e-6
NEG_INF = float("-inf")

V7X_VMEM_BYTES = 64 * 1024 * 1024
VMEM_LIMIT = 56 * 1024 * 1024


def _cparams(sem):
    return pltpu.CompilerParams(dimension_semantics=sem, vmem_limit_bytes=VMEM_LIMIT)


def _rms(x, g):
    ms = jnp.mean(x * x, axis=-1, keepdims=True)
    return x * lax.rsqrt(ms + EPS) * g


def _sigmoid(x):
    return 1.0 / (1.0 + jnp.exp(-x))


def _silu(x):
    return x * _sigmoid(x)


def _dot(a, b):
    return jnp.dot(a, b, preferred_element_type=F32)


def _dot_nt(a, b):
    return lax.dot_general(a, b, (((1,), (1,)), ((), ())), preferred_element_type=F32)


def _tile_split(tseq, tm):
    if tseq >= tm:
        return 1, tm, tseq // tm
    return tm // tseq, tseq, 1


ROW_SUB = 4
ROW_LANES = 128
ROW_DTYPE = jnp.uint32
HI_MASK = 0xFFFF0000


def _pack_chunks(x):
    half = x.shape[1] // 2
    hi = pltpu.bitcast(x[:, :half].astype(BF16).astype(F32), jnp.uint32) & jnp.uint32(HI_MASK)
    lo = pltpu.bitcast(x[:, half:].astype(BF16).astype(F32), jnp.uint32) >> 16
    w = hi | lo
    return [w[:, j * ROW_LANES:(j + 1) * ROW_LANES] for j in range(ROW_SUB)]


def _unpack_chunks(chunks):
    his = [pltpu.bitcast(w & jnp.uint32(HI_MASK), F32) for w in chunks]
    los = [pltpu.bitcast(w << 16, F32) for w in chunks]
    return jnp.concatenate(his + los, axis=1)


def _store_rows(ref, chunks, rows, first=0):
    for j, c in enumerate(chunks):
        ref[pl.ds(first * ROW_SUB + j, rows, stride=ROW_SUB), :] = c


def _load_rows(ref, rows, first=0, lead=None):
    if lead is None:
        return [ref[pl.ds(first * ROW_SUB + j, rows, stride=ROW_SUB), :] for j in range(ROW_SUB)]
    return [ref[lead, pl.ds(first * ROW_SUB + j, rows, stride=ROW_SUB), :] for j in range(ROW_SUB)]


SC_CORES = 2
SC_SUBCORES = 16
SC_WORKERS = SC_CORES * SC_SUBCORES
SC_CHUNK = 64


def _sc_scatter_rows(src, dest, n_out):
    n, sub, lanes = src.shape
    kk = dest.shape[1]
    ch = min(SC_CHUNK, n // SC_WORKERS)
    n_chunks = n // (SC_WORKERS * ch)
    idx3 = dest.reshape(n // ch, ch, kk).transpose(0, 2, 1)
    mesh = plsc.VectorSubcoreMesh(core_axis_name="c", subcore_axis_name="s")

    @functools.partial(
        pl.kernel, mesh=mesh,
        out_type=jax.ShapeDtypeStruct((n_out, sub, lanes), src.dtype),
        scratch_types=[pltpu.VMEM((kk, ch), jnp.int32), pltpu.VMEM((ch, sub, lanes), src.dtype),
                       pltpu.SemaphoreType.DMA],
    )
    def scatter_kernel(src_hbm, idx_hbm, out_hbm, idx_v, rows_v, sem):
        wid = lax.axis_index("s") * SC_CORES + lax.axis_index("c")

        @pl.loop(0, n_chunks)
        def _(c):
            j = wid * n_chunks + c
            pltpu.sync_copy(idx_hbm.at[j], idx_v)
            pltpu.sync_copy(src_hbm.at[pl.ds(j * ch, ch)], rows_v)
            copies = [pltpu.async_copy(rows_v, out_hbm.at[idx_v.at[q]], sem) for q in range(kk)]
            for cp in copies:
                cp.wait()

    return scatter_kernel(src, idx3)


def _sc_gather_rows(table, dest):
    v, sub, lanes = table.shape
    n, kk = dest.shape
    total = n * kk
    ch = min(SC_CHUNK, total // SC_WORKERS)
    n_chunks = total // (SC_WORKERS * ch)
    idx2 = dest.T.reshape(total // ch, ch)
    mesh = plsc.VectorSubcoreMesh(core_axis_name="c", subcore_axis_name="s")

    @functools.partial(
        pl.kernel, mesh=mesh,
        out_type=jax.ShapeDtypeStruct((total, sub, lanes), table.dtype),
        scratch_types=[pltpu.VMEM((1, ch), jnp.int32), pltpu.VMEM((ch, sub, lanes), table.dtype),
                       pltpu.SemaphoreType.DMA],
    )
    def gather_kernel(tab_hbm, idx_hbm, out_hbm, idx_v, rows_v, sem):
        wid = lax.axis_index("s") * SC_CORES + lax.axis_index("c")

        @pl.loop(0, n_chunks)
        def _(c):
            j = wid * n_chunks + c
            pltpu.sync_copy(idx_hbm.at[pl.ds(j, 1)], idx_v)
            pltpu.async_copy(tab_hbm.at[idx_v.at[0]], rows_v, sem).wait()
            pltpu.sync_copy(rows_v, out_hbm.at[pl.ds(j * ch, ch)])

    return gather_kernel(table, idx2).reshape(kk, n, sub, lanes)


def _ada_kernel(c_ref, w_ref, b_ref, o_ref):
    c = _silu(c_ref[...]).astype(BF16)
    o_ref[0] = _dot(c, w_ref[0].astype(BF16)) + b_ref[0]


def _ada_all(c_all, w_ada, b_ada):
    depth, d, n = w_ada.shape
    rows = c_all.shape[0]
    tn = 1536
    return pl.pallas_call(
        _ada_kernel,
        grid=(depth, n // tn),
        in_specs=[
            pl.BlockSpec((rows, d), lambda l, j: (0, 0)),
            pl.BlockSpec((1, d, tn), lambda l, j: (l, 0, j)),
            pl.BlockSpec((1, 1, tn), lambda l, j: (l, 0, j)),
        ],
        out_specs=pl.BlockSpec((1, rows, tn), lambda l, j: (l, 0, j)),
        out_shape=jax.ShapeDtypeStruct((depth, rows, n), F32),
        compiler_params=_cparams(("parallel", "parallel")),
        name="ada",
    )(c_all, w_ada, b_ada.reshape(depth, 1, n))


def _in_proj_kernel(x_ref, sh_ref, sc_ref, g_ref, w_ref, *refs, dils, fold):
    s, t, d = x_ref.shape
    rows = s * t
    h = _rms(x_ref[...], g_ref[...]) * (1.0 + sc_ref[...]) + sh_ref[...]
    hb = h.reshape(rows, d).astype(BF16)
    n_g = len(dils)
    if fold:
        u_ref = refs[0]
        q_refs = refs[1:1 + 2 * n_g:2] + (refs[1 + 2 * n_g],)
        kv_refs = refs[2:2 + 2 * n_g:2] + (refs[2 + 2 * n_g],)
        gl_ref = refs[3 + 2 * n_g]
        st_refs = refs[4 + 2 * n_g:5 + 3 * n_g]
        zs_ref = refs[5 + 3 * n_g]
    else:
        u_ref = refs[0]
        q_refs = refs[1:2 + n_g]
        kv_refs = (None,) * (n_g + 1)
        gl_ref = refs[2 + n_g]
        st_refs = refs[3 + n_g:4 + 2 * n_g]
        zs_ref = None

    def proj(c0, c1):
        return _dot(hb, w_ref[:, c0:c1])

    u_ref[...] = proj(0, POOL_WIDTH).reshape(s, t, POOL_WIDTH)
    off = POOL_WIDTH
    for g, dil in enumerate(tuple(dils) + (1,)):
        qw = DIL_W if g < n_g else SWA_QW
        kvw = 2 * DIL_W if g < n_g else 2 * SWA_KW
        z = proj(off, off + qw + kvw)
        off += qw + kvw
        zq = z[:, :qw] * SCALE
        zkv = z[:, qw:]
        st = st_refs[g]
        st[...] = zkv[rows - st.shape[-1]:, :].T
        if not fold:
            q_refs[g][...] = zq
        elif dil == 1:
            q_refs[g][0, 0] = zq.astype(BF16)
            kv_refs[g][0, 0] = zkv.astype(BF16)
        else:
            zf = jnp.concatenate([zq, zkv], axis=1)
            n_c = zf.shape[1] // BLOCK
            for c in range(n_c):
                zs_ref[c] = zf[:, c * BLOCK:(c + 1) * BLOCK]
            n = rows // dil
            for r in range(dil):
                part = jnp.concatenate([zs_ref[c, pl.ds(r, n, stride=dil), :] for c in range(n_c)], axis=1)
                q_refs[g][0, r] = part[:, :qw].astype(BF16)
                kv_refs[g][0, r] = part[:, qw:].astype(BF16)
    gl_ref[...] = proj(off, off + 3 * d)


def _in_proj(x3, ada3, g_pre, w_perm, layer, tm, keeps, fold):
    nseq, tseq, d = x3.shape
    ntok = nseq * tseq
    s_blk, t_blk, tps = _tile_split(tseq, tm)
    n_tiles = ntok // tm
    in_w = w_perm.shape[-1]
    dils = tuple(dil for _, dil in DIL_GROUPS)
    widths = [(DIL_W, 2 * DIL_W)] * len(dils) + [(SWA_QW, 2 * SWA_KW)]

    def x_map(i):
        return (i // tps, i % tps, 0)

    def tok_map(i):
        return (i, 0)

    out_shapes = [jax.ShapeDtypeStruct((nseq, tseq, POOL_WIDTH), F32)]
    out_specs = [pl.BlockSpec((s_blk, t_blk, POOL_WIDTH), x_map)]
    for (qw, kvw), dil in zip(widths, dils + (1,)):
        if fold:
            for w in (qw, kvw):
                out_shapes.append(jax.ShapeDtypeStruct((nseq, dil, tseq // dil, w), BF16))
                out_specs.append(pl.BlockSpec((1, dil, tm // dil, w), lambda i: (i // tps, 0, i % tps, 0)))
        else:
            out_shapes.append(jax.ShapeDtypeStruct((ntok, qw), F32))
            out_specs.append(pl.BlockSpec((tm, qw), tok_map))
    out_shapes.append(jax.ShapeDtypeStruct((ntok, 3 * d), F32))
    out_specs.append(pl.BlockSpec((tm, 3 * d), tok_map))
    for keep, (_, kvw) in zip(keeps, widths):
        if fold:
            sb = min(tm, keep)
            bps = keep // sb
            first = (tseq - keep) // tm

            def st_map(i, bps=bps, first=first):
                return (i // tps, 0, jnp.maximum(i % tps - first, 0) * (1 if bps > 1 else 0))

            out_shapes.append(jax.ShapeDtypeStruct((nseq, kvw, keep), F32))
            out_specs.append(pl.BlockSpec((None, kvw, sb), st_map))
        else:
            out_shapes.append(jax.ShapeDtypeStruct((kvw, ntok), F32))
            out_specs.append(pl.BlockSpec((kvw, tm), lambda i: (0, i)))
    scratch = [pltpu.VMEM((3 * DIL_W // BLOCK, tm, BLOCK), F32)] if fold else []
    return pl.pallas_call(
        functools.partial(_in_proj_kernel, dils=dils, fold=fold),
        grid=(n_tiles,),
        in_specs=[
            pl.BlockSpec((s_blk, t_blk, d), x_map),
            pl.BlockSpec((s_blk, 1, d), lambda i: (i // tps, 0, 0)),
            pl.BlockSpec((s_blk, 1, d), lambda i: (i // tps, 0, 1)),
            pl.BlockSpec((1, 1, d), lambda i: (layer, 0, 0)),
            pl.BlockSpec((d, in_w), lambda i: (0, 0)),
        ],
        out_specs=out_specs,
        out_shape=out_shapes,
        scratch_shapes=scratch,
        compiler_params=_cparams(("arbitrary",)),
        name="in_proj",
    )(x3, ada3, ada3, g_pre, w_perm)


PAIR = 2 * HEAD_DIM


def _band_attn_kernel(*refs, n_heads, n_kv, nq, with_sink):
    if with_sink:
        sink_ref, q_ref, kv_ref, halo_ref, bm_ref, mask_ref, o_ref = refs
        lse_ref = None
    else:
        q_ref, kv_ref, halo_ref, bm_ref, mask_ref, o_ref, lse_ref = refs
        sink_ref = None
    kw = n_kv * HEAD_DIM
    grp = n_heads // n_kv
    chunk = pl.program_id(2)
    lo, hi = mask_ref[0], mask_ref[1]
    low_lanes = lax.broadcasted_iota(jnp.int32, (BLOCK, PAIR), 1) < HEAD_DIM
    col = lax.broadcasted_iota(jnp.int32, (BLOCK, 4 * BLOCK), 1)
    prev_penalty = jnp.where(col % (2 * BLOCK) < BLOCK, NEG_INF, 0.0)

    def swap_halves(x):
        return jnp.concatenate([x[:, HEAD_DIM:], x[:, :HEAD_DIM]], axis=1)

    def pair_sources(kv_blk, p):
        if grp == 1:
            k = kv_blk[:, p * PAIR:(p + 1) * PAIR]
            v = kv_blk[:, kw + p * PAIR:kw + (p + 1) * PAIR]
            return (k, k), (v, v)
        kh = (2 * p) // grp
        c0 = (kh // 2) * PAIR
        k = kv_blk[:, c0:c0 + PAIR]
        v = kv_blk[:, kw + c0:kw + c0 + PAIR]
        ks, vs = swap_halves(k), swap_halves(v)
        return ((k, ks), (v, vs)) if kh % 2 == 0 else ((ks, k), (vs, v))

    def one_block(r0, kv_prev, first):
        qb = q_ref[pl.ds(r0, BLOCK), :]
        kv_cur = kv_ref[pl.ds(r0, BLOCK), :]
        for p in range(n_heads // 2):
            (kpe, kpo), (vpe, vpo) = pair_sources(kv_prev, p)
            (kce, kco), (vce, vco) = pair_sources(kv_cur, p)
            k_blk = jnp.concatenate([kpe * lo, kce * lo, kpo * hi, kco * hi], axis=0)
            v_blk = jnp.concatenate([
                jnp.concatenate([vpe * lo, lo], axis=1), jnp.concatenate([vce * lo, lo], axis=1),
                jnp.concatenate([vpo * hi, hi], axis=1), jnp.concatenate([vco * hi, hi], axis=1)], axis=0)
            s = _dot_nt(qb[:, p * PAIR:(p + 1) * PAIR], k_blk) + bm_ref[p]
            if first is not None:
                s = s + jnp.where(first, prev_penalty, 0.0)
            s_e, s_o = s[:, :2 * BLOCK], s[:, 2 * BLOCK:]
            m_e = jnp.max(jnp.maximum(s_e[:, :BLOCK], s_e[:, BLOCK:]), axis=-1, keepdims=True)
            m_o = jnp.max(jnp.maximum(s_o[:, :BLOCK], s_o[:, BLOCK:]), axis=-1, keepdims=True)
            if with_sink:
                m_e = jnp.maximum(m_e, sink_ref[2 * p])
                m_o = jnp.maximum(m_o, sink_ref[2 * p + 1])
            pr = jnp.concatenate([jnp.exp(s_e - m_e), jnp.exp(s_o - m_o)], axis=1).astype(BF16)
            res = _dot(pr, v_blk)
            den = res[:, PAIR:]
            if with_sink:
                den = den + jnp.where(low_lanes, jnp.exp(sink_ref[2 * p] - m_e), jnp.exp(sink_ref[2 * p + 1] - m_o))
            o_ref[pl.ds(r0, BLOCK), p * PAIR:(p + 1) * PAIR] = res[:, :PAIR] / den
            if lse_ref is not None:
                lse_ref[pl.ds(r0, BLOCK), p * PAIR:(p + 1) * PAIR] = jnp.where(low_lanes, m_e, m_o) + jnp.log(den)

    one_block(0, halo_ref[...], chunk == 0)

    def body(j, carry):
        r0 = pl.multiple_of(j * BLOCK, BLOCK)
        one_block(r0, kv_ref[pl.ds(r0 - BLOCK, BLOCK), :], None)
        return carry

    if nq > 1:
        lax.fori_loop(1, nq, body, 0)


def _band_attn(q, kv, bm, n_heads, n_kv, sinks=None):
    batch, dil, fold, qw = q.shape
    kvw = kv.shape[-1]
    nq = min(8, fold // BLOCK)
    rows = nq * BLOCK
    n_chunks = fold // rows
    with_sink = sinks is not None
    lanes_low = np.arange(PAIR) < HEAD_DIM
    masks = jnp.asarray(np.broadcast_to(np.stack([lanes_low, ~lanes_low])[:, None, :], (2, BLOCK, PAIR)), F32).astype(BF16)
    in_specs = [
        pl.BlockSpec((None, None, rows, qw), lambda b, r, c: (b, r, c, 0)),
        pl.BlockSpec((None, None, rows, kvw), lambda b, r, c: (b, r, c, 0)),
        pl.BlockSpec((None, None, BLOCK, kvw), lambda b, r, c: (b, r, jnp.maximum(c * nq - 1, 0), 0)),
        pl.BlockSpec(bm.shape, lambda b, r, c: (0, 0, 0)),
        pl.BlockSpec(masks.shape, lambda b, r, c: (0, 0, 0)),
    ]
    args = [q, kv, kv, bm, masks]
    o_spec = pl.BlockSpec((None, None, rows, qw), lambda b, r, c: (b, r, c, 0))
    o_shape = jax.ShapeDtypeStruct((batch, dil, fold, qw), F32)
    if with_sink:
        in_specs = [pl.BlockSpec(memory_space=pltpu.SMEM)] + in_specs
        args = [sinks] + args
        out_specs, out_shape = o_spec, o_shape
    else:
        out_specs, out_shape = [o_spec, o_spec], [o_shape, o_shape]
    return pl.pallas_call(
        functools.partial(_band_attn_kernel, n_heads=n_heads, n_kv=n_kv, nq=nq, with_sink=with_sink),
        grid=(batch, dil, n_chunks),
        in_specs=in_specs,
        out_specs=out_specs,
        out_shape=out_shape,
        compiler_params=_cparams(("parallel", "parallel", "parallel")),
        name="band_attn",
    )(*args)


SAMPLE_SEQ_UNROLL = 4


def _sample_attn_kernel(*refs, n_heads, n_kv, sb, t_new, with_sink, aliased):
    refs = list(refs)
    q_ref, newt_ref, cache_ref, bmc_ref, bmn_ref = refs[:5]
    refs = refs[5:]
    sink_ref = refs.pop(0) if with_sink else None
    if aliased:
        refs.pop(0)
    o_ref = refs.pop(0)
    lse_ref = None if with_sink else refs.pop(0)
    cout_ref = refs.pop(0)
    kw = n_kv * HEAD_DIM
    grp = n_heads // n_kv
    w, n_buf = cache_ref.shape[2:]
    first_new = BLOCK - t_new
    per_blk = BLOCK // t_new
    i = pl.program_id(0)
    lane = lax.broadcasted_iota(jnp.int32, (w, BLOCK), 1)

    def one_seq(s, carry):
        r0 = s * t_new if isinstance(s, int) else pl.multiple_of(s * t_new, t_new)
        qs = q_ref[pl.ds(r0, t_new), :]
        q_rows = []
        for h in range(n_heads):
            piece = qs[:, h * HEAD_DIM:(h + 1) * HEAD_DIM]
            parts = [piece if k == h // grp else jnp.zeros_like(piece) for k in range(n_kv)]
            q_rows.append(jnp.concatenate(parts, axis=1))
        qbd = jnp.concatenate(q_rows, axis=0).astype(BF16)
        cache = cache_ref[0, s]
        off = ((i * sb + s) % per_blk) * t_new
        placed = pltpu.roll(newt_ref[...], first_new - off, axis=1)
        s_c = _dot(qbd, cache[:kw].astype(BF16)) + bmc_ref[...]
        s_n = _dot(qbd, placed[:kw].astype(BF16)) + bmn_ref[...]
        m = jnp.maximum(jnp.max(s_c, axis=-1, keepdims=True), jnp.max(s_n, axis=-1, keepdims=True))
        if with_sink:
            m = jnp.maximum(m, sink_ref[:, :1])
        p_c = jnp.exp(s_c - m)
        p_n = jnp.exp(s_n - m)
        den = jnp.sum(p_c, axis=-1, keepdims=True) + jnp.sum(p_n, axis=-1, keepdims=True)
        if with_sink:
            den = den + jnp.exp(sink_ref[:, :1] - m)
        o_all = (_dot_nt(p_c.astype(BF16), cache[kw:].astype(BF16))
                 + _dot_nt(p_n.astype(BF16), placed[kw:].astype(BF16))) / den
        outs = []
        for h in range(n_heads):
            kh = h // grp
            outs.append(o_all[h * t_new:(h + 1) * t_new, kh * HEAD_DIM:(kh + 1) * HEAD_DIM])
        o_ref[pl.ds(r0, t_new), :] = jnp.concatenate(outs, axis=1)
        if lse_ref is not None:
            lse = m + jnp.log(den)
            lse_ref[pl.ds(r0, t_new), :] = jnp.concatenate(
                [jnp.broadcast_to(lse[h * t_new:(h + 1) * t_new], (t_new, HEAD_DIM)) for h in range(n_heads)], axis=1)
        rolled = pltpu.roll(cache, n_buf - t_new, axis=1)
        if n_buf > BLOCK:
            cout_ref[0, s, :, : n_buf - BLOCK] = rolled[:, : n_buf - BLOCK]
        cout_ref[0, s, :, n_buf - BLOCK:] = jnp.where(lane >= first_new, placed, rolled[:, n_buf - BLOCK:])
        return carry

    group = min(sb, SAMPLE_SEQ_UNROLL)
    if sb == group:
        for s in range(sb):
            one_seq(s, 0)
    else:
        def body(it, carry):
            for u in range(group):
                one_seq(it * group + u, carry)
            return carry

        lax.fori_loop(0, sb // group, body, 0)


def _sample_attn(q, newt, cache_t, prev_out, bmc, bmn, layer, n_heads, n_kv, t_new, sink_rows=None):
    depth, nseq, w, n_buf = cache_t.shape
    qw = n_heads * HEAD_DIM
    sb = max(1, min(BLOCK // t_new, 2048 // n_buf))
    with_sink = sink_rows is not None
    aliased = prev_out is not None
    rows = sb * t_new
    per_blk = BLOCK // t_new
    in_specs = [
        pl.BlockSpec((rows, qw), lambda i: (i, 0)),
        pl.BlockSpec((w, BLOCK), lambda i: (0, (i * sb) // per_blk)),
        pl.BlockSpec((1, sb, w, n_buf), lambda i: (layer, i, 0, 0)),
        pl.BlockSpec(bmc.shape, lambda i: (0, 0)),
        pl.BlockSpec(bmn.shape, lambda i: (0, 0)),
    ]
    args = [q, newt, cache_t, bmc, bmn]
    if with_sink:
        in_specs.append(pl.BlockSpec(sink_rows.shape, lambda i: (0, 0)))
        args.append(sink_rows)
    aliases = {}
    if aliased:
        aliases = {len(args): 1 if with_sink else 2}
        in_specs.append(pl.BlockSpec(memory_space=pl.ANY))
        args.append(prev_out)
    o_spec = pl.BlockSpec((rows, qw), lambda i: (i, 0))
    o_shape = jax.ShapeDtypeStruct((nseq * t_new, qw), F32)
    c_spec = pl.BlockSpec((1, sb, w, n_buf), lambda i: (layer, i, 0, 0))
    c_shape = jax.ShapeDtypeStruct(cache_t.shape, F32)
    if with_sink:
        out_specs, out_shape = [o_spec, c_spec], [o_shape, c_shape]
    else:
        out_specs, out_shape = [o_spec, o_spec, c_spec], [o_shape, o_shape, c_shape]
    return pl.pallas_call(
        functools.partial(_sample_attn_kernel, n_heads=n_heads, n_kv=n_kv, sb=sb, t_new=t_new,
                          with_sink=with_sink, aliased=aliased),
        grid=(nseq // sb,),
        in_specs=in_specs,
        out_specs=out_specs,
        out_shape=out_shape,
        input_output_aliases=aliases,
        compiler_params=_cparams(("parallel",)),
        name="sample_attn",
    )(*args)


def _post_kernel(x_ref, gt_ref, shf_ref, scf_ref, g2_ref, g3_ref, u_ref, halo_ref,
                 o1_ref, l1_ref, o2_ref, l2_ref, o3_ref, l3_ref, oc_ref, gl_ref,
                 pw_ref, ps_ref, wb_ref, wc_ref, wo_ref, wr_ref, rb_ref,
                 xmid_ref, h2_ref, idx_ref, rank_ref, wk_ref, counts_ref, cnt_ref, *unfold_refs,
                 tps, full_windows, dils):
    s, t, d = x_ref.shape
    rows = s * t
    i = pl.program_id(0)

    u = u_ref[...]
    halo = halo_ref[...]
    if not full_windows:
        halo = jnp.where(i % tps == 0, 0.0, halo)
    ue = jnp.concatenate([halo, u], axis=1)
    if full_windows:
        row = None
    else:
        row = (i % tps) * t + lax.broadcasted_iota(jnp.int32, (1, t, 1), 1)
    parts = []
    for g, win in enumerate(POOL_WINDOWS):
        cs = slice(g * POOL_CH, (g + 1) * POOL_CH)
        acc = ue[:, :, cs]
        base = 0
        span = 1
        while span < win:
            acc = acc[:, span:, :] + acc[:, : acc.shape[1] - span, :]
            base += span
            span *= 2
        tot = acc[:, POOL_HALO - base:, :]
        if full_windows:
            mean = tot / float(win)
        else:
            cnt = jnp.minimum(row + 1, win).astype(F32)
            mean = tot / cnt
        zg = (mean - u[:, :, cs]).reshape(rows, POOL_CH).astype(BF16)
        parts.append(_dot(zg, pw_ref[0, g]))
    a = jnp.concatenate(parts, axis=-1) * ps_ref[0]

    scratch = list(unfold_refs)

    def token_order(ref, dil):
        if dil == 1:
            return ref[...]
        scr = scratch.pop(0)
        n = rows // dil
        n_c = scr.shape[0]
        for r in range(dil):
            part = ref[0, r]
            for c in range(n_c):
                scr[c, pl.ds(r, n, stride=dil), :] = part[:, c * BLOCK:(c + 1) * BLOCK]
        return jnp.concatenate([scr[c] for c in range(n_c)], axis=1)

    o1, l1 = token_order(o1_ref, dils[0]), token_order(l1_ref, dils[0])
    o2, l2 = token_order(o2_ref, dils[1]), token_order(l2_ref, dils[1])
    o3, l3 = token_order(o3_ref, dils[2]), token_order(l3_ref, dils[2])

    lm = jnp.maximum(jnp.maximum(l1, l2), l3)
    e1, e2, e3 = jnp.exp(l1 - lm), jnp.exp(l2 - lm), jnp.exp(l3 - lm)
    esum = e1 + e2 + e3
    bmix = (e1 / esum) * o1 + (e2 / esum) * o2 + (e3 / esum) * o3
    b = _dot(bmix.astype(BF16), wb_ref[...])
    c = _dot(oc_ref[...].astype(BF16), wc_ref[...])
    g_a = _sigmoid(gl_ref[:, :d])
    g_b = _sigmoid(gl_ref[:, d:2 * d])
    g_c = _sigmoid(gl_ref[:, 2 * d:])
    mix = _dot((g_a * a + g_b * b + g_c * c).astype(BF16), wo_ref[...])

    x = x_ref[...]
    xm = x + gt_ref[...] * _rms(mix, g2_ref[0]).reshape(s, t, d)
    xmid_ref[...] = xm
    h2f = (_rms(xm, g3_ref[...]) * (1.0 + scf_ref[...]) + shf_ref[...]).reshape(rows, d)
    h2 = h2f.astype(BF16)
    _store_rows(h2_ref, _pack_chunks(h2f), rows)

    scores = _sigmoid(_dot(h2, wr_ref[...]))
    work = scores + rb_ref[0]
    lane = lax.broadcasted_iota(jnp.int32, (rows, N_EXPERTS), 1).astype(F32)
    sel = jnp.zeros((rows, N_EXPERTS), F32)
    picks = []
    for _ in range(TOP_K):
        mx = jnp.max(work, axis=-1, keepdims=True)
        pick = jnp.min(jnp.where(work == mx, lane, float(N_EXPERTS)), axis=-1, keepdims=True)
        hit = lane == pick
        sel = jnp.where(hit, 1.0, sel)
        work = jnp.where(hit, NEG_INF, work)
        picks.append(pick)
    top_s = scores * sel
    wmat = top_s / jnp.sum(top_s, axis=-1, keepdims=True) * ROUTED_SCALE

    @pl.when(i == 0)
    def _():
        cnt_ref[...] = jnp.zeros_like(cnt_ref)

    r_i = lax.broadcasted_iota(jnp.int32, (rows, rows), 0)
    c_i = lax.broadcasted_iota(jnp.int32, (rows, rows), 1)
    below = jnp.where(c_i < r_i, 1.0, 0.0).astype(BF16)
    rank_all = cnt_ref[...] + _dot(below, sel.astype(BF16))
    cnt_ref[...] = cnt_ref[...] + jnp.sum(sel, axis=0, keepdims=True)
    counts_ref[...] = cnt_ref[...].astype(jnp.int32)

    kcol = lax.broadcasted_iota(jnp.int32, (rows, TOP_K), 1)
    idx = jnp.zeros((rows, TOP_K), F32)
    rank = jnp.zeros((rows, TOP_K), F32)
    wk = jnp.zeros((rows, TOP_K), F32)
    for k, pick in enumerate(picks):
        hit = lane == pick
        idx = jnp.where(kcol == k, pick, idx)
        rank = jnp.where(kcol == k, jnp.sum(jnp.where(hit, rank_all, 0.0), axis=-1, keepdims=True), rank)
        wk = jnp.where(kcol == k, jnp.sum(jnp.where(hit, wmat, 0.0), axis=-1, keepdims=True), wk)
    idx_ref[...] = idx.astype(jnp.int32)
    rank_ref[...] = rank.astype(jnp.int32)
    wk_ref[...] = wk


def _post_mixer(x3, ada3, u3, halo3, attn, gl, weights, layer, tm, full_windows, dils):
    nseq, tseq, d = x3.shape
    ntok = nseq * tseq
    s_blk, t_blk, tps = _tile_split(tseq, tm)
    n_tiles = ntok // tm
    (g2, g3, pool_w, pool_scale, w_b_up, w_c_up, w_out, w_router, router_bias) = weights

    def x_map(i):
        return (i // tps, i % tps, 0)

    def tok_map(i):
        return (i, 0)

    def ada_spec(j):
        return pl.BlockSpec((s_blk, 1, d), lambda i: (i // tps, 0, j))

    if full_windows:
        halo_spec = pl.BlockSpec((s_blk, POOL_HALO, POOL_WIDTH), lambda i: (i, 0, 0))
    else:
        hb = t_blk // POOL_HALO
        halo_spec = pl.BlockSpec((1, POOL_HALO, POOL_WIDTH),
                                 lambda i: (i // tps, jnp.maximum((i % tps) * hb - 1, 0), 0))
    vec = pl.BlockSpec((1, 1, d), lambda i: (layer, 0, 0))
    attn_specs = []
    for g, dil in enumerate(dils):
        if dil == 1:
            spec = pl.BlockSpec((tm, DIL_W), tok_map)
        else:
            spec = pl.BlockSpec((1, dil, tm // dil, DIL_W), lambda i: (i // tps, 0, i % tps, 0))
        attn_specs += [spec, spec]
    in_specs = [
        pl.BlockSpec((s_blk, t_blk, d), x_map), ada_spec(2), ada_spec(3), ada_spec(4), vec, vec,
        pl.BlockSpec((s_blk, t_blk, POOL_WIDTH), x_map), halo_spec,
    ] + attn_specs + [
        pl.BlockSpec((tm, SWA_QW), tok_map), pl.BlockSpec((tm, 3 * d), tok_map),
        pl.BlockSpec((1,) + pool_w.shape[1:], lambda i: (layer, 0, 0, 0)),
        vec,
        pl.BlockSpec(w_b_up.shape, lambda i: (0, 0)), pl.BlockSpec(w_c_up.shape, lambda i: (0, 0)),
        pl.BlockSpec(w_out.shape, lambda i: (0, 0)), pl.BlockSpec(w_router.shape, lambda i: (0, 0)),
        pl.BlockSpec((1, 1, N_EXPERTS), lambda i: (layer, 0, 0)),
    ]
    out_shape = [
        jax.ShapeDtypeStruct((nseq, tseq, d), F32),
        jax.ShapeDtypeStruct((ntok * ROW_SUB, ROW_LANES), ROW_DTYPE),
        jax.ShapeDtypeStruct((ntok, TOP_K), jnp.int32),
        jax.ShapeDtypeStruct((ntok, TOP_K), jnp.int32),
        jax.ShapeDtypeStruct((ntok, TOP_K), F32),
        jax.ShapeDtypeStruct((1, N_EXPERTS), jnp.int32),
    ]
    out_specs = [
        pl.BlockSpec((s_blk, t_blk, d), x_map),
        pl.BlockSpec((tm * ROW_SUB, ROW_LANES), tok_map),
        pl.BlockSpec((tm, TOP_K), tok_map),
        pl.BlockSpec((tm, TOP_K), tok_map),
        pl.BlockSpec((tm, TOP_K), tok_map),
        pl.BlockSpec((1, N_EXPERTS), lambda i: (0, 0)),
    ]
    n_unfold = 2 * sum(1 for dil in dils if dil > 1)
    scratch = [pltpu.VMEM((1, N_EXPERTS), F32)] + [pltpu.VMEM((DIL_W // BLOCK, tm, BLOCK), F32)] * n_unfold
    return pl.pallas_call(
        functools.partial(_post_kernel, tps=tps, full_windows=full_windows, dils=tuple(dils)),
        grid=(n_tiles,),
        in_specs=in_specs,
        out_specs=out_specs,
        out_shape=out_shape,
        scratch_shapes=scratch,
        compiler_params=_cparams(("arbitrary",)),
        name="post_mixer",
    )(x3, ada3, ada3, ada3, g2, g3, u3, halo3, *attn, gl,
      pool_w, pool_scale, w_b_up, w_c_up, w_out, w_router, router_bias)


EXPERT_ROWS = 256


def _expert_kernel(bexp_ref, nvalid_ref, nused_ref, x_ref, wg_ref, wu_ref, wd_ref, y_ref, wg_s, wu_s, wd_s):
    i = pl.program_id(0)
    bm = x_ref.shape[0] // ROW_SUB
    sub = min(bm, EXPERT_ROWS)

    @pl.when((i == 0) | (bexp_ref[i] != bexp_ref[jnp.maximum(i - 1, 0)]))
    def _():
        wg_s[...] = wg_ref[0, 0].astype(BF16)
        wu_s[...] = wu_ref[0, 0].astype(BF16)
        wd_s[...] = wd_ref[0, 0].astype(BF16)

    @pl.when(i < nused_ref[0])
    def _():
        for c in range(bm // sub):
            x = _unpack_chunks(_load_rows(x_ref, sub, first=c * sub))
            row = c * sub + lax.broadcasted_iota(jnp.int32, (sub, 1), 0)
            x = jnp.where(row < nvalid_ref[i], x, 0.0).astype(BF16)
            gate = _dot(x, wg_s[...])
            up = _dot(x, wu_s[...])
            y = _dot((_silu(gate) * up).astype(BF16), wd_s[...])
            _store_rows(y_ref, _pack_chunks(y), sub, first=c * sub)

    @pl.when(i >= nused_ref[0])
    def _():
        y_ref[...] = jnp.zeros_like(y_ref)


def _experts(x_rows, block_exp, nvalid, n_used, we_gate, we_up, we_down, layer, bm):
    n_rows = x_rows.shape[0] // ROW_SUB
    d, ff = we_gate.shape[-2:]
    n_blocks = n_rows // bm
    grid_spec = pltpu.PrefetchScalarGridSpec(
        num_scalar_prefetch=3,
        grid=(n_blocks,),
        in_specs=[
            pl.BlockSpec((bm * ROW_SUB, ROW_LANES), lambda i, be, nv, nu: (i, 0)),
            pl.BlockSpec((1, 1, d, ff), lambda i, be, nv, nu: (layer, be[i], 0, 0)),
            pl.BlockSpec((1, 1, d, ff), lambda i, be, nv, nu: (layer, be[i], 0, 0)),
            pl.BlockSpec((1, 1, ff, d), lambda i, be, nv, nu: (layer, be[i], 0, 0)),
        ],
        out_specs=pl.BlockSpec((bm * ROW_SUB, ROW_LANES), lambda i, be, nv, nu: (i, 0)),
        scratch_shapes=[pltpu.VMEM((d, ff), BF16), pltpu.VMEM((d, ff), BF16), pltpu.VMEM((ff, d), BF16)],
    )
    return pl.pallas_call(
        _expert_kernel,
        grid_spec=grid_spec,
        out_shape=jax.ShapeDtypeStruct(x_rows.shape, x_rows.dtype),
        compiler_params=_cparams(("arbitrary",)),
        name="experts",
    )(block_exp, nvalid, n_used, x_rows, we_gate, we_up, we_down)


def _final_kernel(x_ref, gt_ref, g4_ref, h2_ref, yk_ref, wk_ref, wg_ref, wu_ref, wd_ref, o_ref):
    s, t, d = x_ref.shape
    rows = s * t
    h2 = _unpack_chunks(_load_rows(h2_ref, rows)).astype(BF16)
    f = _dot((_silu(_dot(h2, wg_ref[...])) * _dot(h2, wu_ref[...])).astype(BF16), wd_ref[...])
    wk = wk_ref[...]
    for k in range(TOP_K):
        f = f + wk[:, k:k + 1] * _unpack_chunks(_load_rows(yk_ref, rows, lead=k))
    o_ref[...] = x_ref[...] + gt_ref[...] * _rms(f, g4_ref[0]).reshape(s, t, d)


def _final(xmid3, ada3, g4, h2_rows, yk_rows, wk, ws_gate, ws_up, ws_down, layer, tm):
    nseq, tseq, d = xmid3.shape
    ntok = nseq * tseq
    s_blk, t_blk, tps = _tile_split(tseq, tm)

    def x_map(i):
        return (i // tps, i % tps, 0)

    return pl.pallas_call(
        _final_kernel,
        grid=(ntok // tm,),
        in_specs=[
            pl.BlockSpec((s_blk, t_blk, d), x_map),
            pl.BlockSpec((s_blk, 1, d), lambda i: (i // tps, 0, 5)),
            pl.BlockSpec((1, 1, d), lambda i: (layer, 0, 0)),
            pl.BlockSpec((tm * ROW_SUB, ROW_LANES), lambda i: (i, 0)),
            pl.BlockSpec((TOP_K, tm * ROW_SUB, ROW_LANES), lambda i: (0, i, 0)),
            pl.BlockSpec((tm, TOP_K), lambda i: (i, 0)),
            pl.BlockSpec(ws_gate.shape, lambda i: (0, 0)),
            pl.BlockSpec(ws_up.shape, lambda i: (0, 0)),
            pl.BlockSpec(ws_down.shape, lambda i: (0, 0)),
        ],
        out_specs=pl.BlockSpec((s_blk, t_blk, d), x_map),
        out_shape=jax.ShapeDtypeStruct((nseq, tseq, d), F32),
        compiler_params=_cparams(("parallel",)),
        name="final",
    )(xmid3, ada3, g4, h2_rows, yk_rows, wk, ws_gate, ws_up, ws_down)


def _t5_bucket(dist):
    n = np.asarray(dist, dtype=np.int64)
    exact = N_BUCKETS // 2
    log_ratio = np.log(np.maximum(n, 1) / exact) / np.log(MAX_DISTANCE / exact)
    large = np.minimum(exact + (log_ratio * (N_BUCKETS - exact)).astype(np.int64), N_BUCKETS - 1)
    return np.where(n < exact, n, large).astype(np.int32)


def _bias_table(rel_bias, h0, h1, steps, dil):
    steps = np.asarray(steps)
    buckets = np.where(steps >= 0, _t5_bucket(np.maximum(steps, 0) * dil), -1)
    onehot = buckets[..., None] == np.arange(N_BUCKETS)
    table = rel_bias[:, h0:h1].T.astype(F32).reshape((h1 - h0,) + (1,) * steps.ndim + (N_BUCKETS,))
    val = jnp.sum(jnp.where(onehot[None], table, 0.0), axis=-1)
    return jnp.where((steps >= 0)[None], val, NEG_INF)


def _band_steps():
    dist = BLOCK + np.arange(BLOCK)[:, None] - np.arange(2 * BLOCK)[None, :]
    return np.where((dist >= 0) & (dist <= BLOCK), dist, -1)


def _sample_steps(n_buf, dil, t_new):
    t = np.arange(t_new)[:, None]
    delta_c = n_buf + t - np.arange(n_buf)[None, :]
    ok_c = (delta_c % dil == 0) & (delta_c // dil <= BLOCK)
    lane = np.arange(BLOCK)[None, :]
    delta_n = t - (lane - (BLOCK - t_new))
    ok_n = (lane >= BLOCK - t_new) & (delta_n >= 0) & (delta_n % dil == 0) & (delta_n // dil <= BLOCK)
    return np.where(ok_c, delta_c // dil, -1), np.where(ok_n, delta_n // dil, -1)


def _route_plan(idx, rank, counts, bm):
    n_tok = idx.shape[0]
    n_blocks = -(-(n_tok * TOP_K + N_EXPERTS * (bm - 1)) // bm)
    counts = counts.reshape(N_EXPERTS)
    padded = (counts + bm - 1) // bm * bm
    pad_end = jnp.cumsum(padded)
    pad_start = pad_end - padded
    experts = jnp.arange(N_EXPERTS, dtype=jnp.int32)
    onehot = idx[:, :, None] == experts[None, None, :]
    dest = rank + jnp.sum(jnp.where(onehot, pad_start[None, None, :], 0), axis=-1)
    starts = jnp.arange(n_blocks, dtype=jnp.int32) * bm
    block_exp = jnp.minimum(jnp.sum((starts[:, None] >= pad_end[None, :]).astype(jnp.int32), axis=1), N_EXPERTS - 1)
    hot = block_exp[:, None] == experts[None, :]
    blk_cnt = jnp.sum(jnp.where(hot, counts[None, :], 0), axis=1)
    blk_start = jnp.sum(jnp.where(hot, pad_start[None, :], 0), axis=1)
    nvalid = jnp.clip(blk_cnt - (starts - blk_start), 0, bm).astype(jnp.int32)
    n_used = (pad_end[-1] // bm).astype(jnp.int32).reshape(1)
    return dest.astype(jnp.int32), block_exp.astype(jnp.int32), nvalid, n_used, n_blocks * bm


def _moe_routed(h2_rows, idx, rank, counts, we_gate, we_up, we_down, layer, bm):
    n_tok = idx.shape[0]
    dest, block_exp, nvalid, n_used, n_rows = _route_plan(idx, rank, counts, bm)
    x_sorted = _sc_scatter_rows(h2_rows.reshape(n_tok, ROW_SUB, ROW_LANES), dest, n_rows)
    y_sorted = _experts(x_sorted.reshape(n_rows * ROW_SUB, ROW_LANES), block_exp, nvalid, n_used,
                        we_gate, we_up, we_down, layer, bm)
    yk = _sc_gather_rows(y_sorted.reshape(n_rows, ROW_SUB, ROW_LANES), dest)
    return yk.reshape(TOP_K, n_tok * ROW_SUB, ROW_LANES)


def _to_time_minor(c):
    depth, nseq, n_buf = c.shape[:3]
    return jnp.transpose(c, (0, 1, 3, 4, 5, 2)).reshape(depth, nseq, -1, n_buf)


def _from_time_minor(ct, n_heads):
    lead = ct.shape[:-2]
    rows = ct.shape[-1]
    nl = len(lead)
    x = ct.reshape(lead + (2, n_heads, HEAD_DIM, rows))
    return jnp.transpose(x, tuple(range(nl)) + (nl + 3, nl, nl + 1, nl + 2))


def kernel(x_prompt, x_sample, cache_b1, cache_b2, cache_b3, cache_c, state_pool, c_prompt, c_sample, rel_bias, w_ada, b_ada, g_pre_mix, g_post_mix, g_pre_ffn, g_post_ffn, w_in, pool_w, pool_scale, w_b_up, w_c_up, sinks, w_out, w_router, router_bias, we_gate, we_up, we_down, ws_gate, ws_up, ws_down):
    batch, seq, d = x_prompt.shape
    dec_batch, dec_seq, _ = x_sample.shape
    depth = w_in.shape[0]
    tm_p = 256
    tm_s = 128
    n_dil = len(DIL_GROUPS)
    dils = tuple(dil for _, dil in DIL_GROUPS)

    n_seq_all = batch + dec_batch
    pad = -n_seq_all % 8
    c_all = jnp.concatenate([c_prompt, c_sample, jnp.zeros((pad, d), F32)], axis=0)
    ada = _ada_all(c_all, w_ada, b_ada)

    head0 = [g * DIL_HEADS for g in range(n_dil)] + [n_dil * DIL_HEADS]
    head1 = [(g + 1) * DIL_HEADS for g in range(n_dil)] + [n_dil * DIL_HEADS + SWA_HEADS]
    band = []
    for g, dil in enumerate(dils + (1,)):
        tab = _bias_table(rel_bias, head0[g], head1[g], _band_steps(), dil)
        n_h = head1[g] - head0[g]
        band.append(tab.reshape(n_h // 2, 2, BLOCK, 2 * BLOCK).transpose(0, 2, 1, 3).reshape(n_h // 2, BLOCK, 4 * BLOCK))
    caches = (cache_b1, cache_b2, cache_b3, cache_c)
    samp = []
    for g, dil in enumerate(dils + (1,)):
        n_buf = caches[g].shape[2]
        sc_steps, sn_steps = _sample_steps(n_buf, dil, dec_seq)
        rows = (head1[g] - head0[g]) * dec_seq
        samp.append((_bias_table(rel_bias, head0[g], head1[g], sc_steps, dil).reshape(rows, n_buf),
                     _bias_table(rel_bias, head0[g], head1[g], sn_steps, dil).reshape(rows, BLOCK)))

    splits = np.cumsum([0, POOL_WIDTH, 3 * DIL_W, 3 * DIL_W, 3 * DIL_W, SWA_QW, SWA_KW, SWA_KW, 3 * d])
    col_slices = [(int(splits[0]), int(splits[1]))]
    for g in range(n_dil):
        for part in range(3):
            c0 = int(splits[1 + part]) + g * DIL_W
            col_slices.append((c0, c0 + DIL_W))
    col_slices.append((int(splits[4]), int(splits[8])))

    vec3 = lambda a: a.reshape(depth, 1, -1)
    g1, g2, g3, g4 = vec3(g_pre_mix), vec3(g_post_mix), vec3(g_pre_ffn), vec3(g_post_ffn)
    ps3, rb3 = vec3(pool_scale), vec3(router_bias)
    cache_t = [_to_time_minor(c) for c in caches]
    pool_halo = jnp.pad(state_pool, ((0, 0), (0, 0), (POOL_HALO - state_pool.shape[2], 0), (0, 0)))
    pw = pool_w.astype(BF16)

    xp, xs = x_prompt, x_sample
    cache_out = [None] * (n_dil + 1)
    st_p = [[] for _ in range(n_dil + 2)]
    pool_s = []
    keeps_p = tuple(min(win, seq) for win, _ in DIL_GROUPS) + (min(BLOCK, seq),)
    n_heads = (DIL_HEADS,) * n_dil + (SWA_HEADS,)
    n_kvs = (DIL_HEADS,) * n_dil + (SWA_KV_HEADS,)
    for l in range(depth):
        w_perm = jnp.concatenate([w_in[l][:, a:b] for a, b in col_slices], axis=1).astype(BF16)
        wb, wc, wo, wr = w_b_up[l].astype(BF16), w_c_up[l].astype(BF16), w_out[l].astype(BF16), w_router[l].astype(BF16)
        wsg, wsu, wsd = ws_gate[l].astype(BF16), ws_up[l].astype(BF16), ws_down[l].astype(BF16)
        post_w = (g2, g3, pw, ps3, wb, wc, wo, wr, rb3)
        ada_p = ada[l, :batch].reshape(batch, 1, -1)
        ada_s = ada[l, batch:n_seq_all].reshape(dec_batch, 1, -1)
        sink_rows = jnp.broadcast_to(jnp.repeat(sinks[l].astype(F32), dec_seq)[:, None], (SWA_HEADS * dec_seq, BLOCK))

        outs = _in_proj(xp, ada_p, g1, w_perm, l, tm_p, keeps_p, True)
        u, gl, states = outs[0], outs[3 + 2 * n_dil], outs[4 + 2 * n_dil:]
        attn = []
        for g in range(n_dil):
            o, lse = _band_attn(outs[1 + 2 * g], outs[2 + 2 * g], band[g], DIL_HEADS, DIL_HEADS)
            if dils[g] == 1:
                o, lse = o.reshape(batch * seq, DIL_W), lse.reshape(batch * seq, DIL_W)
            attn += [o, lse]
        oc = _band_attn(outs[1 + 2 * n_dil], outs[2 + 2 * n_dil], band[n_dil], SWA_HEADS, SWA_KV_HEADS, sinks=sinks[l])
        attn.append(oc.reshape(batch * seq, SWA_QW))
        xmid, h2, idx, rank, wk, counts = _post_mixer(xp, ada_p, u, u, attn, gl, post_w, l, tm_p, False, dils)
        y_k = _moe_routed(h2, idx, rank, counts, we_gate, we_up, we_down, l, 512)
        xp = _final(xmid, ada_p, g4, h2, y_k, wk, wsg, wsu, wsd, l, tm_p)
        for k in range(n_dil + 1):
            st_p[k].append(_from_time_minor(states[k], n_kvs[k]))
        st_p[n_dil + 1].append(u[:, seq - (POOL_HALO - 1):])

        outs = _in_proj(xs, ada_s, g1, w_perm, l, tm_s, (dec_seq,) * (n_dil + 1), False)
        u, gl, states = outs[0], outs[2 + n_dil], outs[3 + n_dil:]
        attn = []
        for g in range(n_dil + 1):
            res = _sample_attn(outs[1 + g], states[g], cache_t[g], cache_out[g], samp[g][0], samp[g][1], l,
                               n_heads[g], n_kvs[g], dec_seq, sink_rows=sink_rows if g == n_dil else None)
            attn += list(res[:-1])
            cache_out[g] = res[-1]
        xmid, h2, idx, rank, wk, counts = _post_mixer(xs, ada_s, u, pool_halo[l], attn, gl, post_w, l, tm_s, True,
                                                      (1,) * n_dil)
        y_k = _moe_routed(h2, idx, rank, counts, we_gate, we_up, we_down, l, 128)
        xs = _final(xmid, ada_s, g4, h2, y_k, wk, wsg, wsu, wsd, l, tm_s)
        pool_s.append(jnp.concatenate([state_pool[l], u], axis=1)[:, -(POOL_HALO - 1):])

    b1_p, b2_p, b3_p, c_p, pool_p = [jnp.stack(s, axis=0) for s in st_p]
    outs_s = [_from_time_minor(co, n_kvs[g]) for g, co in enumerate(cache_out)]
    return (xp, xs, b1_p, b2_p, b3_p, c_p, pool_p, outs_s[0], outs_s[1], outs_s[2], outs_s[3], jnp.stack(pool_s, axis=0))
```

```python
import functools

import numpy as np
import jax
import jax.numpy as jnp
from jax import lax
from jax.experimental import pallas as pl
from jax.experimental.pallas import tpu as pltpu
from jax.experimental.pallas import tpu_sc as plsc

F32 = jnp.float32
BF16 = jnp.bfloat16

HEAD_DIM = 64
SCALE = HEAD_DIM ** -0.5
BLOCK = 128
POOL_WINDOWS = (2, 4, 8, 16)
POOL_CH = 128
POOL_WIDTH = len(POOL_WINDOWS) * POOL_CH
POOL_HALO = 16
DIL_GROUPS = ((128, 1), (512, 4), (2048, 16))
DIL_HEADS = 4
DIL_W = DIL_HEADS * HEAD_DIM
SWA_HEADS = 8
SWA_KV_HEADS = 2
SWA_QW = SWA_HEADS * HEAD_DIM
SWA_KW = SWA_KV_HEADS * HEAD_DIM
N_BUCKETS = 32
MAX_DISTANCE = 2048
N_EXPERTS = 64
TOP_K = 8
ROUTED_SCALE = 2.5
EPS = 1e-6
NEG_INF = float("-inf")

V7X_VMEM_BYTES = 64 * 1024 * 1024
VMEM_LIMIT = 56 * 1024 * 1024


def _cparams(sem):
    return pltpu.CompilerParams(dimension_semantics=sem, vmem_limit_bytes=VMEM_LIMIT)


def _rms(x, g):
    ms = jnp.mean(x * x, axis=-1, keepdims=True)
    return x * lax.rsqrt(ms + EPS) * g


def _sigmoid(x):
    return 1.0 / (1.0 + jnp.exp(-x))


def _silu(x):
    return x * _sigmoid(x)


def _dot(a, b):
    return jnp.dot(a, b, preferred_element_type=F32)


def _dot_nt(a, b):
    return lax.dot_general(a, b, (((1,), (1,)), ((), ())), preferred_element_type=F32)


def _tile_split(tseq, tm):
    if tseq >= tm:
        return 1, tm, tseq // tm
    return tm // tseq, tseq, 1


ROW_SUB = 4
ROW_LANES = 128
ROW_DTYPE = jnp.uint32
HI_MASK = 0xFFFF0000


def _pack_chunks(x):
    half = x.shape[1] // 2
    hi = pltpu.bitcast(x[:, :half].astype(BF16).astype(F32), jnp.uint32) & jnp.uint32(HI_MASK)
    lo = pltpu.bitcast(x[:, half:].astype(BF16).astype(F32), jnp.uint32) >> 16
    w = hi | lo
    return [w[:, j * ROW_LANES:(j + 1) * ROW_LANES] for j in range(ROW_SUB)]


def _unpack_chunks(chunks):
    his = [pltpu.bitcast(w & jnp.uint32(HI_MASK), F32) for w in chunks]
    los = [pltpu.bitcast(w << 16, F32) for w in chunks]
    return jnp.concatenate(his + los, axis=1)


def _store_rows(ref, chunks, rows, first=0):
    for j, c in enumerate(chunks):
        ref[pl.ds(first * ROW_SUB + j, rows, stride=ROW_SUB), :] = c


def _load_rows(ref, rows, first=0, lead=None):
    if lead is None:
        return [ref[pl.ds(first * ROW_SUB + j, rows, stride=ROW_SUB), :] for j in range(ROW_SUB)]
    return [ref[lead, pl.ds(first * ROW_SUB + j, rows, stride=ROW_SUB), :] for j in range(ROW_SUB)]


SC_CORES = 2
SC_SUBCORES = 16
SC_WORKERS = SC_CORES * SC_SUBCORES
SC_CHUNK = 64


def _sc_scatter_rows(src, dest, n_out):
    n, sub, lanes = src.shape
    kk = dest.shape[0]
    ch = min(SC_CHUNK, n // SC_WORKERS)
    n_chunks = n // (SC_WORKERS * ch)
    idx3 = dest.reshape(kk, n // ch, ch).transpose(1, 0, 2)
    mesh = plsc.VectorSubcoreMesh(core_axis_name="c", subcore_axis_name="s")

    @functools.partial(
        pl.kernel, mesh=mesh,
        out_type=jax.ShapeDtypeStruct((n_out, sub, lanes), src.dtype),
        scratch_types=[pltpu.VMEM((kk, ch), jnp.int32), pltpu.VMEM((ch, sub, lanes), src.dtype),
                       pltpu.SemaphoreType.DMA],
    )
    def scatter_kernel(src_hbm, idx_hbm, out_hbm, idx_v, rows_v, sem):
        wid = lax.axis_index("s") * SC_CORES + lax.axis_index("c")

        @pl.loop(0, n_chunks)
        def _(c):
            j = wid * n_chunks + c
            pltpu.sync_copy(idx_hbm.at[j], idx_v)
            pltpu.sync_copy(src_hbm.at[pl.ds(j * ch, ch)], rows_v)
            copies = [pltpu.async_copy(rows_v, out_hbm.at[idx_v.at[q]], sem) for q in range(kk)]
            for cp in copies:
                cp.wait()

    return scatter_kernel(src, idx3)


def _sc_gather_rows(table, dest):
    v, sub, lanes = table.shape
    kk, n = dest.shape
    total = n * kk
    ch = min(SC_CHUNK, total // SC_WORKERS)
    n_chunks = total // (SC_WORKERS * ch)
    idx2 = dest.reshape(total // ch, ch)
    mesh = plsc.VectorSubcoreMesh(core_axis_name="c", subcore_axis_name="s")

    @functools.partial(
        pl.kernel, mesh=mesh,
        out_type=jax.ShapeDtypeStruct((total, sub, lanes), table.dtype),
        scratch_types=[pltpu.VMEM((1, ch), jnp.int32), pltpu.VMEM((ch, sub, lanes), table.dtype),
                       pltpu.SemaphoreType.DMA],
    )
    def gather_kernel(tab_hbm, idx_hbm, out_hbm, idx_v, rows_v, sem):
        wid = lax.axis_index("s") * SC_CORES + lax.axis_index("c")

        @pl.loop(0, n_chunks)
        def _(c):
            j = wid * n_chunks + c
            pltpu.sync_copy(idx_hbm.at[pl.ds(j, 1)], idx_v)
            pltpu.async_copy(tab_hbm.at[idx_v.at[0]], rows_v, sem).wait()
            pltpu.sync_copy(rows_v, out_hbm.at[pl.ds(j * ch, ch)])

    return gather_kernel(table, idx2).reshape(kk, n, sub, lanes)


def _ada_kernel(c_ref, w_ref, b_ref, o_ref):
    c = _silu(c_ref[...]).astype(BF16)
    o_ref[0] = _dot(c, w_ref[0].astype(BF16)) + b_ref[0]


def _ada_all(c_all, w_ada, b_ada):
    depth, d, n = w_ada.shape
    rows = c_all.shape[0]
    tn = 1536
    return pl.pallas_call(
        _ada_kernel,
        grid=(depth, n // tn),
        in_specs=[
            pl.BlockSpec((rows, d), lambda l, j: (0, 0)),
            pl.BlockSpec((1, d, tn), lambda l, j: (l, 0, j)),
            pl.BlockSpec((1, 1, tn), lambda l, j: (l, 0, j)),
        ],
        out_specs=pl.BlockSpec((1, rows, tn), lambda l, j: (l, 0, j)),
        out_shape=jax.ShapeDtypeStruct((depth, rows, n), F32),
        compiler_params=_cparams(("parallel", "parallel")),
        name="ada",
    )(c_all, w_ada, b_ada.reshape(depth, 1, n))


def _in_proj_kernel(x_ref, sh_ref, sc_ref, g_ref, w_ref, *refs, dils, fold):
    s, t, d = x_ref.shape
    rows = s * t
    h = _rms(x_ref[...], g_ref[...]) * (1.0 + sc_ref[...]) + sh_ref[...]
    hb = h.reshape(rows, d).astype(BF16)
    n_g = len(dils)
    if fold:
        u_ref = refs[0]
        q_refs = refs[1:1 + 2 * n_g:2] + (refs[1 + 2 * n_g],)
        kv_refs = refs[2:2 + 2 * n_g:2] + (refs[2 + 2 * n_g],)
        gl_ref = refs[3 + 2 * n_g]
        st_refs = refs[4 + 2 * n_g:5 + 3 * n_g]
        zs_ref = refs[5 + 3 * n_g]
    else:
        u_ref = refs[0]
        q_refs = refs[1:2 + n_g]
        kv_refs = (None,) * (n_g + 1)
        gl_ref = refs[2 + n_g]
        st_refs = refs[3 + n_g:4 + 2 * n_g]
        zs_ref = None

    def proj(c0, c1):
        return _dot(hb, w_ref[:, c0:c1])

    u_ref[...] = proj(0, POOL_WIDTH).reshape(s, t, POOL_WIDTH)
    off = POOL_WIDTH
    for g, dil in enumerate(tuple(dils) + (1,)):
        qw = DIL_W if g < n_g else SWA_QW
        kvw = 2 * DIL_W if g < n_g else 2 * SWA_KW
        z = proj(off, off + qw + kvw)
        off += qw + kvw
        zq = z[:, :qw] * SCALE
        zkv = z[:, qw:]
        st = st_refs[g]
        st[...] = zkv[rows - st.shape[-1]:, :].T
        if not fold:
            q_refs[g][...] = zq
        elif dil == 1:
            q_refs[g][0, 0] = zq.astype(BF16)
            kv_refs[g][0, 0] = zkv.astype(BF16)
        else:
            zf = jnp.concatenate([zq, zkv], axis=1)
            n_c = zf.shape[1] // BLOCK
            for c in range(n_c):
                zs_ref[c] = zf[:, c * BLOCK:(c + 1) * BLOCK]
            n = rows // dil
            for r in range(dil):
                part = jnp.concatenate([zs_ref[c, pl.ds(r, n, stride=dil), :] for c in range(n_c)], axis=1)
                q_refs[g][0, r] = part[:, :qw].astype(BF16)
                kv_refs[g][0, r] = part[:, qw:].astype(BF16)
    gl_ref[...] = proj(off, off + 3 * d)


def _in_proj(x3, ada3, g_pre, w_perm, layer, tm, keeps, fold):
    nseq, tseq, d = x3.shape
    ntok = nseq * tseq
    s_blk, t_blk, tps = _tile_split(tseq, tm)
    n_tiles = ntok // tm
    in_w = w_perm.shape[-1]
    dils = tuple(dil for _, dil in DIL_GROUPS)
    widths = [(DIL_W, 2 * DIL_W)] * len(dils) + [(SWA_QW, 2 * SWA_KW)]

    def x_map(i):
        return (i // tps, i % tps, 0)

    def tok_map(i):
        return (i, 0)

    out_shapes = [jax.ShapeDtypeStruct((nseq, tseq, POOL_WIDTH), F32)]
    out_specs = [pl.BlockSpec((s_blk, t_blk, POOL_WIDTH), x_map)]
    for (qw, kvw), dil in zip(widths, dils + (1,)):
        if fold:
            for w in (qw, kvw):
                out_shapes.append(jax.ShapeDtypeStruct((nseq, dil, tseq // dil, w), BF16))
                out_specs.append(pl.BlockSpec((1, dil, tm // dil, w), lambda i: (i // tps, 0, i % tps, 0)))
        else:
            out_shapes.append(jax.ShapeDtypeStruct((ntok, qw), F32))
            out_specs.append(pl.BlockSpec((tm, qw), tok_map))
    out_shapes.append(jax.ShapeDtypeStruct((ntok, 3 * d), F32))
    out_specs.append(pl.BlockSpec((tm, 3 * d), tok_map))
    for keep, (_, kvw) in zip(keeps, widths):
        if fold:
            sb = min(tm, keep)
            bps = keep // sb
            first = (tseq - keep) // tm

            def st_map(i, bps=bps, first=first):
                return (i // tps, 0, jnp.maximum(i % tps - first, 0) * (1 if bps > 1 else 0))

            out_shapes.append(jax.ShapeDtypeStruct((nseq, kvw, keep), F32))
            out_specs.append(pl.BlockSpec((None, kvw, sb), st_map))
        else:
            out_shapes.append(jax.ShapeDtypeStruct((kvw, ntok), F32))
            out_specs.append(pl.BlockSpec((kvw, tm), lambda i: (0, i)))
    scratch = [pltpu.VMEM((3 * DIL_W // BLOCK, tm, BLOCK), F32)] if fold else []
    return pl.pallas_call(
        functools.partial(_in_proj_kernel, dils=dils, fold=fold),
        grid=(n_tiles,),
        in_specs=[
            pl.BlockSpec((s_blk, t_blk, d), x_map),
            pl.BlockSpec((s_blk, 1, d), lambda i: (i // tps, 0, 0)),
            pl.BlockSpec((s_blk, 1, d), lambda i: (i // tps, 0, 1)),
            pl.BlockSpec((1, 1, d), lambda i: (layer, 0, 0)),
            pl.BlockSpec((d, in_w), lambda i: (0, 0)),
        ],
        out_specs=out_specs,
        out_shape=out_shapes,
        scratch_shapes=scratch,
        compiler_params=_cparams(("arbitrary",)),
        name="in_proj",
    )(x3, ada3, ada3, g_pre, w_perm)


PAIR = 2 * HEAD_DIM


def _band_attn_kernel(*refs, n_heads, n_kv, nq, with_sink):
    if with_sink:
        sink_ref, q_ref, kv_ref, halo_ref, bm_ref, mask_ref, o_ref = refs
        lse_ref = None
    else:
        q_ref, kv_ref, halo_ref, bm_ref, mask_ref, o_ref, lse_ref = refs
        sink_ref = None
    kw = n_kv * HEAD_DIM
    grp = n_heads // n_kv
    chunk = pl.program_id(2)
    lo, hi = mask_ref[0], mask_ref[1]
    low_lanes = lax.broadcasted_iota(jnp.int32, (BLOCK, PAIR), 1) < HEAD_DIM
    col = lax.broadcasted_iota(jnp.int32, (BLOCK, 4 * BLOCK), 1)
    prev_penalty = jnp.where(col % (2 * BLOCK) < BLOCK, NEG_INF, 0.0)

    def swap_halves(x):
        return jnp.concatenate([x[:, HEAD_DIM:], x[:, :HEAD_DIM]], axis=1)

    def pair_sources(kv_blk, p):
        if grp == 1:
            k = kv_blk[:, p * PAIR:(p + 1) * PAIR]
            v = kv_blk[:, kw + p * PAIR:kw + (p + 1) * PAIR]
            return (k, k), (v, v)
        kh = (2 * p) // grp
        c0 = (kh // 2) * PAIR
        k = kv_blk[:, c0:c0 + PAIR]
        v = kv_blk[:, kw + c0:kw + c0 + PAIR]
        ks, vs = swap_halves(k), swap_halves(v)
        return ((k, ks), (v, vs)) if kh % 2 == 0 else ((ks, k), (vs, v))

    def one_block(r0, kv_prev, first):
        qb = q_ref[pl.ds(r0, BLOCK), :]
        kv_cur = kv_ref[pl.ds(r0, BLOCK), :]
        for p in range(n_heads // 2):
            (kpe, kpo), (vpe, vpo) = pair_sources(kv_prev, p)
            (kce, kco), (vce, vco) = pair_sources(kv_cur, p)
            k_blk = jnp.concatenate([kpe * lo, kce * lo, kpo * hi, kco * hi], axis=0)
            v_blk = jnp.concatenate([
                jnp.concatenate([vpe * lo, lo], axis=1), jnp.concatenate([vce * lo, lo], axis=1),
                jnp.concatenate([vpo * hi, hi], axis=1), jnp.concatenate([vco * hi, hi], axis=1)], axis=0)
            s = _dot_nt(qb[:, p * PAIR:(p + 1) * PAIR], k_blk) + bm_ref[p]
            if first is not None:
                s = s + jnp.where(first, prev_penalty, 0.0)
            s_e, s_o = s[:, :2 * BLOCK], s[:, 2 * BLOCK:]
            m_e = jnp.max(jnp.maximum(s_e[:, :BLOCK], s_e[:, BLOCK:]), axis=-1, keepdims=True)
            m_o = jnp.max(jnp.maximum(s_o[:, :BLOCK], s_o[:, BLOCK:]), axis=-1, keepdims=True)
            if with_sink:
                m_e = jnp.maximum(m_e, sink_ref[2 * p])
                m_o = jnp.maximum(m_o, sink_ref[2 * p + 1])
            pr = jnp.concatenate([jnp.exp(s_e - m_e), jnp.exp(s_o - m_o)], axis=1).astype(BF16)
            res = _dot(pr, v_blk)
            den = res[:, PAIR:]
            if with_sink:
                den = den + jnp.where(low_lanes, jnp.exp(sink_ref[2 * p] - m_e), jnp.exp(sink_ref[2 * p + 1] - m_o))
            o_ref[pl.ds(r0, BLOCK), p * PAIR:(p + 1) * PAIR] = res[:, :PAIR] / den
            if lse_ref is not None:
                lse_ref[pl.ds(r0, BLOCK), p * PAIR:(p + 1) * PAIR] = jnp.where(low_lanes, m_e, m_o) + jnp.log(den)

    one_block(0, halo_ref[...], chunk == 0)

    def body(j, carry):
        r0 = pl.multiple_of(j * BLOCK, BLOCK)
        one_block(r0, kv_ref[pl.ds(r0 - BLOCK, BLOCK), :], None)
        return carry

    if nq > 1:
        lax.fori_loop(1, nq, body, 0)


def _band_attn(q, kv, bm, n_heads, n_kv, sinks=None):
    batch, dil, fold, qw = q.shape
    kvw = kv.shape[-1]
    nq = min(8, fold // BLOCK)
    rows = nq * BLOCK
    n_chunks = fold // rows
    with_sink = sinks is not None
    lanes_low = np.arange(PAIR) < HEAD_DIM
    masks = jnp.asarray(np.broadcast_to(np.stack([lanes_low, ~lanes_low])[:, None, :], (2, BLOCK, PAIR)), F32).astype(BF16)
    in_specs = [
        pl.BlockSpec((None, None, rows, qw), lambda b, r, c: (b, r, c, 0)),
        pl.BlockSpec((None, None, rows, kvw), lambda b, r, c: (b, r, c, 0)),
        pl.BlockSpec((None, None, BLOCK, kvw), lambda b, r, c: (b, r, jnp.maximum(c * nq - 1, 0), 0)),
        pl.BlockSpec(bm.shape, lambda b, r, c: (0, 0, 0)),
        pl.BlockSpec(masks.shape, lambda b, r, c: (0, 0, 0)),
    ]
    args = [q, kv, kv, bm, masks]
    o_spec = pl.BlockSpec((None, None, rows, qw), lambda b, r, c: (b, r, c, 0))
    o_shape = jax.ShapeDtypeStruct((batch, dil, fold, qw), F32)
    if with_sink:
        in_specs = [pl.BlockSpec(memory_space=pltpu.SMEM)] + in_specs
        args = [sinks] + args
        out_specs, out_shape = o_spec, o_shape
    else:
        out_specs, out_shape = [o_spec, o_spec], [o_shape, o_shape]
    return pl.pallas_call(
        functools.partial(_band_attn_kernel, n_heads=n_heads, n_kv=n_kv, nq=nq, with_sink=with_sink),
        grid=(batch, dil, n_chunks),
        in_specs=in_specs,
        out_specs=out_specs,
        out_shape=out_shape,
        compiler_params=_cparams(("parallel", "parallel", "parallel")),
        name="band_attn",
    )(*args)


SAMPLE_SEQ_UNROLL = 4


def _sample_attn_kernel(*refs, n_heads, n_kv, sb, t_new, with_sink, aliased):
    refs = list(refs)
    q_ref, newt_ref, cache_ref, bmc_ref, bmn_ref = refs[:5]
    refs = refs[5:]
    sink_ref = refs.pop(0) if with_sink else None
    if aliased:
        refs.pop(0)
    o_ref = refs.pop(0)
    lse_ref = None if with_sink else refs.pop(0)
    cout_ref = refs.pop(0)
    kw = n_kv * HEAD_DIM
    grp = n_heads // n_kv
    w, n_buf = cache_ref.shape[2:]
    first_new = BLOCK - t_new
    per_blk = BLOCK // t_new
    i = pl.program_id(0)
    lane = lax.broadcasted_iota(jnp.int32, (w, BLOCK), 1)

    def one_seq(s, carry):
        r0 = s * t_new if isinstance(s, int) else pl.multiple_of(s * t_new, t_new)
        qs = q_ref[pl.ds(r0, t_new), :]
        q_rows = []
        for h in range(n_heads):
            piece = qs[:, h * HEAD_DIM:(h + 1) * HEAD_DIM]
            parts = [piece if k == h // grp else jnp.zeros_like(piece) for k in range(n_kv)]
            q_rows.append(jnp.concatenate(parts, axis=1))
        qbd = jnp.concatenate(q_rows, axis=0).astype(BF16)
        cache = cache_ref[0, s]
        off = ((i * sb + s) % per_blk) * t_new
        placed = pltpu.roll(newt_ref[...], first_new - off, axis=1)
        s_c = _dot(qbd, cache[:kw].astype(BF16)) + bmc_ref[...]
        s_n = _dot(qbd, placed[:kw].astype(BF16)) + bmn_ref[...]
        m = jnp.maximum(jnp.max(s_c, axis=-1, keepdims=True), jnp.max(s_n, axis=-1, keepdims=True))
        if with_sink:
            m = jnp.maximum(m, sink_ref[:, :1])
        p_c = jnp.exp(s_c - m)
        p_n = jnp.exp(s_n - m)
        den = jnp.sum(p_c, axis=-1, keepdims=True) + jnp.sum(p_n, axis=-1, keepdims=True)
        if with_sink:
            den = den + jnp.exp(sink_ref[:, :1] - m)
        o_all = (_dot_nt(p_c.astype(BF16), cache[kw:].astype(BF16))
                 + _dot_nt(p_n.astype(BF16), placed[kw:].astype(BF16))) / den
        outs = []
        for h in range(n_heads):
            kh = h // grp
            outs.append(o_all[h * t_new:(h + 1) * t_new, kh * HEAD_DIM:(kh + 1) * HEAD_DIM])
        o_ref[pl.ds(r0, t_new), :] = jnp.concatenate(outs, axis=1)
        if lse_ref is not None:
            lse = m + jnp.log(den)
            lse_ref[pl.ds(r0, t_new), :] = jnp.concatenate(
                [jnp.broadcast_to(lse[h * t_new:(h + 1) * t_new], (t_new, HEAD_DIM)) for h in range(n_heads)], axis=1)
        rolled = pltpu.roll(cache, n_buf - t_new, axis=1)
        if n_buf > BLOCK:
            cout_ref[0, s, :, : n_buf - BLOCK] = rolled[:, : n_buf - BLOCK]
        cout_ref[0, s, :, n_buf - BLOCK:] = jnp.where(lane >= first_new, placed, rolled[:, n_buf - BLOCK:])
        return carry

    group = min(sb, SAMPLE_SEQ_UNROLL)
    if sb == group:
        for s in range(sb):
            one_seq(s, 0)
    else:
        def body(it, carry):
            for u in range(group):
                one_seq(it * group + u, carry)
            return carry

        lax.fori_loop(0, sb // group, body, 0)


def _sample_attn(q, newt, cache_t, prev_out, bmc, bmn, layer, n_heads, n_kv, t_new, sink_rows=None):
    depth, nseq, w, n_buf = cache_t.shape
    qw = n_heads * HEAD_DIM
    sb = max(1, min(BLOCK // t_new, 2048 // n_buf))
    with_sink = sink_rows is not None
    aliased = prev_out is not None
    rows = sb * t_new
    per_blk = BLOCK // t_new
    in_specs = [
        pl.BlockSpec((rows, qw), lambda i: (i, 0)),
        pl.BlockSpec((w, BLOCK), lambda i: (0, (i * sb) // per_blk)),
        pl.BlockSpec((1, sb, w, n_buf), lambda i: (layer, i, 0, 0)),
        pl.BlockSpec(bmc.shape, lambda i: (0, 0)),
        pl.BlockSpec(bmn.shape, lambda i: (0, 0)),
    ]
    args = [q, newt, cache_t, bmc, bmn]
    if with_sink:
        in_specs.append(pl.BlockSpec(sink_rows.shape, lambda i: (0, 0)))
        args.append(sink_rows)
    aliases = {}
    if aliased:
        aliases = {len(args): 1 if with_sink else 2}
        in_specs.append(pl.BlockSpec(memory_space=pl.ANY))
        args.append(prev_out)
    o_spec = pl.BlockSpec((rows, qw), lambda i: (i, 0))
    o_shape = jax.ShapeDtypeStruct((nseq * t_new, qw), F32)
    c_spec = pl.BlockSpec((1, sb, w, n_buf), lambda i: (layer, i, 0, 0))
    c_shape = jax.ShapeDtypeStruct(cache_t.shape, F32)
    if with_sink:
        out_specs, out_shape = [o_spec, c_spec], [o_shape, c_shape]
    else:
        out_specs, out_shape = [o_spec, o_spec, c_spec], [o_shape, o_shape, c_shape]
    return pl.pallas_call(
        functools.partial(_sample_attn_kernel, n_heads=n_heads, n_kv=n_kv, sb=sb, t_new=t_new,
                          with_sink=with_sink, aliased=aliased),
        grid=(nseq // sb,),
        in_specs=in_specs,
        out_specs=out_specs,
        out_shape=out_shape,
        input_output_aliases=aliases,
        compiler_params=_cparams(("parallel",)),
        name="sample_attn",
    )(*args)


def _post_kernel(x_ref, gt_ref, shf_ref, scf_ref, g2_ref, g3_ref, u_ref, halo_ref,
                 o1_ref, l1_ref, o2_ref, l2_ref, o3_ref, l3_ref, oc_ref, gl_ref,
                 pw_ref, ps_ref, wb_ref, wc_ref, wo_ref, wrt_ref, rb_ref,
                 xmid_ref, h2_ref, idx_ref, rank_ref, wk_ref, counts_ref, wfull_ref, cnt_ref, *unfold_refs,
                 tps, full_windows, dils):
    s, t, d = x_ref.shape
    rows = s * t
    i = pl.program_id(0)

    u = u_ref[...]
    halo = halo_ref[...]
    if not full_windows:
        halo = jnp.where(i % tps == 0, 0.0, halo)
    ue = jnp.concatenate([halo, u], axis=1)
    if full_windows:
        row = None
    else:
        row = (i % tps) * t + lax.broadcasted_iota(jnp.int32, (1, t, 1), 1)
    parts = []
    for g, win in enumerate(POOL_WINDOWS):
        cs = slice(g * POOL_CH, (g + 1) * POOL_CH)
        acc = ue[:, :, cs]
        base = 0
        span = 1
        while span < win:
            acc = acc[:, span:, :] + acc[:, : acc.shape[1] - span, :]
            base += span
            span *= 2
        tot = acc[:, POOL_HALO - base:, :]
        if full_windows:
            mean = tot / float(win)
        else:
            cnt = jnp.minimum(row + 1, win).astype(F32)
            mean = tot / cnt
        zg = (mean - u[:, :, cs]).reshape(rows, POOL_CH).astype(BF16)
        parts.append(_dot(zg, pw_ref[0, g]))
    a = jnp.concatenate(parts, axis=-1) * ps_ref[0]

    scratch = list(unfold_refs)

    def token_order(ref, dil):
        if dil == 1:
            return ref[...]
        scr = scratch.pop(0)
        n = rows // dil
        n_c = scr.shape[0]
        for r in range(dil):
            part = ref[0, r]
            for c in range(n_c):
                scr[c, pl.ds(r, n, stride=dil), :] = part[:, c * BLOCK:(c + 1) * BLOCK]
        return jnp.concatenate([scr[c] for c in range(n_c)], axis=1)

    o1, l1 = token_order(o1_ref, dils[0]), token_order(l1_ref, dils[0])
    o2, l2 = token_order(o2_ref, dils[1]), token_order(l2_ref, dils[1])
    o3, l3 = token_order(o3_ref, dils[2]), token_order(l3_ref, dils[2])

    lm = jnp.maximum(jnp.maximum(l1, l2), l3)
    e1, e2, e3 = jnp.exp(l1 - lm), jnp.exp(l2 - lm), jnp.exp(l3 - lm)
    esum = e1 + e2 + e3
    bmix = (e1 / esum) * o1 + (e2 / esum) * o2 + (e3 / esum) * o3
    b = _dot(bmix.astype(BF16), wb_ref[...])
    c = _dot(oc_ref[...].astype(BF16), wc_ref[...])
    g_a = _sigmoid(gl_ref[:, :d])
    g_b = _sigmoid(gl_ref[:, d:2 * d])
    g_c = _sigmoid(gl_ref[:, 2 * d:])
    mix = _dot((g_a * a + g_b * b + g_c * c).astype(BF16), wo_ref[...])

    x = x_ref[...]
    xm = x + gt_ref[...] * _rms(mix, g2_ref[0]).reshape(s, t, d)
    xmid_ref[...] = xm
    h2f = (_rms(xm, g3_ref[...]) * (1.0 + scf_ref[...]) + shf_ref[...]).reshape(rows, d)
    h2 = h2f.astype(BF16)
    _store_rows(h2_ref, _pack_chunks(h2f), rows)

    scores = _sigmoid(_dot_nt(wrt_ref[...], h2))
    work = scores + rb_ref[0]
    e_iota = lax.broadcasted_iota(jnp.int32, (N_EXPERTS, rows), 0).astype(F32)
    sel = jnp.zeros((N_EXPERTS, rows), F32)
    picks = []
    for _ in range(TOP_K):
        mx = jnp.max(work, axis=0, keepdims=True)
        pick = jnp.min(jnp.where(work == mx, e_iota, float(N_EXPERTS)), axis=0, keepdims=True)
        hit = e_iota == pick
        sel = jnp.where(hit, 1.0, sel)
        work = jnp.where(hit, NEG_INF, work)
        picks.append(pick)
    top_s = scores * sel
    wmat = top_s / jnp.sum(top_s, axis=0, keepdims=True) * ROUTED_SCALE
    wfull_ref[...] = wmat

    @pl.when(i == 0)
    def _():
        cnt_ref[...] = jnp.zeros_like(cnt_ref)

    r_i = lax.broadcasted_iota(jnp.int32, (rows, rows), 0)
    c_i = lax.broadcasted_iota(jnp.int32, (rows, rows), 1)
    before = jnp.where(r_i < c_i, 1.0, 0.0).astype(BF16)
    rank_all = cnt_ref[...] + _dot(sel.astype(BF16), before)
    cnt_ref[...] = cnt_ref[...] + jnp.sum(sel, axis=1, keepdims=True)
    counts_ref[...] = cnt_ref[...].astype(jnp.int32)

    krow = lax.broadcasted_iota(jnp.int32, (TOP_K, rows), 0)
    idx = jnp.zeros((TOP_K, rows), F32)
    rank = jnp.zeros((TOP_K, rows), F32)
    wk = jnp.zeros((TOP_K, rows), F32)
    for k, pick in enumerate(picks):
        hit = e_iota == pick
        idx = jnp.where(krow == k, pick, idx)
        rank = jnp.where(krow == k, jnp.sum(jnp.where(hit, rank_all, 0.0), axis=0, keepdims=True), rank)
        wk = jnp.where(krow == k, jnp.sum(jnp.where(hit, wmat, 0.0), axis=0, keepdims=True), wk)
    idx_ref[...] = idx.astype(jnp.int32)
    rank_ref[...] = rank.astype(jnp.int32)
    eye = jnp.where(r_i == c_i, 1.0, 0.0).astype(BF16)
    hi = wk.astype(BF16)
    rest = wk - hi.astype(F32)
    mid = rest.astype(BF16)
    lo = (rest - mid.astype(F32)).astype(BF16)
    wk_ref[...] = _dot_nt(eye, hi) + _dot_nt(eye, mid) + _dot_nt(eye, lo)


def _post_mixer(x3, ada3, u3, halo3, attn, gl, weights, layer, tm, full_windows, dils):
    nseq, tseq, d = x3.shape
    ntok = nseq * tseq
    s_blk, t_blk, tps = _tile_split(tseq, tm)
    n_tiles = ntok // tm
    (g2, g3, pool_w, pool_scale, w_b_up, w_c_up, w_out, w_router, router_bias) = weights

    def x_map(i):
        return (i // tps, i % tps, 0)

    def tok_map(i):
        return (i, 0)

    def ada_spec(j):
        return pl.BlockSpec((s_blk, 1, d), lambda i: (i // tps, 0, j))

    if full_windows:
        halo_spec = pl.BlockSpec((s_blk, POOL_HALO, POOL_WIDTH), lambda i: (i, 0, 0))
    else:
        hb = t_blk // POOL_HALO
        halo_spec = pl.BlockSpec((1, POOL_HALO, POOL_WIDTH),
                                 lambda i: (i // tps, jnp.maximum((i % tps) * hb - 1, 0), 0))
    vec = pl.BlockSpec((1, 1, d), lambda i: (layer, 0, 0))
    attn_specs = []
    for g, dil in enumerate(dils):
        if dil == 1:
            spec = pl.BlockSpec((tm, DIL_W), tok_map)
        else:
            spec = pl.BlockSpec((1, dil, tm // dil, DIL_W), lambda i: (i // tps, 0, i % tps, 0))
        attn_specs += [spec, spec]
    in_specs = [
        pl.BlockSpec((s_blk, t_blk, d), x_map), ada_spec(2), ada_spec(3), ada_spec(4), vec, vec,
        pl.BlockSpec((s_blk, t_blk, POOL_WIDTH), x_map), halo_spec,
    ] + attn_specs + [
        pl.BlockSpec((tm, SWA_QW), tok_map), pl.BlockSpec((tm, 3 * d), tok_map),
        pl.BlockSpec((1,) + pool_w.shape[1:], lambda i: (layer, 0, 0, 0)),
        vec,
        pl.BlockSpec(w_b_up.shape, lambda i: (0, 0)), pl.BlockSpec(w_c_up.shape, lambda i: (0, 0)),
        pl.BlockSpec(w_out.shape, lambda i: (0, 0)), pl.BlockSpec(w_router.shape, lambda i: (0, 0)),
        pl.BlockSpec((1, N_EXPERTS, 1), lambda i: (layer, 0, 0)),
    ]
    out_shape = [
        jax.ShapeDtypeStruct((nseq, tseq, d), F32),
        jax.ShapeDtypeStruct((ntok * ROW_SUB, ROW_LANES), ROW_DTYPE),
        jax.ShapeDtypeStruct((TOP_K, ntok), jnp.int32),
        jax.ShapeDtypeStruct((TOP_K, ntok), jnp.int32),
        jax.ShapeDtypeStruct((ntok, TOP_K), F32),
        jax.ShapeDtypeStruct((N_EXPERTS, 1), jnp.int32),
        jax.ShapeDtypeStruct((N_EXPERTS, ntok), F32),
    ]
    out_specs = [
        pl.BlockSpec((s_blk, t_blk, d), x_map),
        pl.BlockSpec((tm * ROW_SUB, ROW_LANES), tok_map),
        pl.BlockSpec((TOP_K, tm), lambda i: (0, i)),
        pl.BlockSpec((TOP_K, tm), lambda i: (0, i)),
        pl.BlockSpec((tm, TOP_K), tok_map),
        pl.BlockSpec((N_EXPERTS, 1), lambda i: (0, 0)),
        pl.BlockSpec((N_EXPERTS, tm), lambda i: (0, i)),
    ]
    n_unfold = 2 * sum(1 for dil in dils if dil > 1)
    scratch = [pltpu.VMEM((N_EXPERTS, 1), F32)] + [pltpu.VMEM((DIL_W // BLOCK, tm, BLOCK), F32)] * n_unfold
    return pl.pallas_call(
        functools.partial(_post_kernel, tps=tps, full_windows=full_windows, dils=tuple(dils)),
        grid=(n_tiles,),
        in_specs=in_specs,
        out_specs=out_specs,
        out_shape=out_shape,
        scratch_shapes=scratch,
        compiler_params=_cparams(("arbitrary",)),
        name="post_mixer",
    )(x3, ada3, ada3, ada3, g2, g3, u3, halo3, *attn, gl,
      pool_w, pool_scale, w_b_up, w_c_up, w_out, w_router, router_bias)


EXPERT_ROWS = 256


def _expert_kernel(bexp_ref, nvalid_ref, nused_ref, x_ref, wg_ref, wu_ref, wd_ref, y_ref, wg_s, wu_s, wd_s, h_s):
    i = pl.program_id(0)
    bm = x_ref.shape[0] // ROW_SUB
    sub = min(bm, EXPERT_ROWS)
    last = bexp_ref.shape[0] - 1
    up_blk = jnp.minimum(i, last)
    dn_blk = jnp.maximum(i - 1, 0)

    @pl.when((i == 0) | (bexp_ref[up_blk] != bexp_ref[jnp.maximum(up_blk - 1, 0)]))
    def _():
        wg_s[...] = wg_ref[0, 0].astype(BF16)
        wu_s[...] = wu_ref[0, 0].astype(BF16)

    @pl.when((i <= 1) | (bexp_ref[dn_blk] != bexp_ref[jnp.maximum(dn_blk - 1, 0)]))
    def _():
        wd_s[...] = wd_ref[0, 0].astype(BF16)

    @pl.when(i == 0)
    def _():
        h_s[...] = jnp.zeros_like(h_s)

    @pl.when(i <= nused_ref[0])
    def _():
        for c in range(bm // sub):
            y = _dot(h_s[c * sub:(c + 1) * sub, :], wd_s[...])
            _store_rows(y_ref, _pack_chunks(y), sub, first=c * sub)
        for c in range(bm // sub):
            x = _unpack_chunks(_load_rows(x_ref, sub, first=c * sub))
            row = c * sub + lax.broadcasted_iota(jnp.int32, (sub, 1), 0)
            x = jnp.where(row < nvalid_ref[up_blk], x, 0.0).astype(BF16)
            gate = _dot(x, wg_s[...])
            up = _dot(x, wu_s[...])
            h_s[c * sub:(c + 1) * sub, :] = (_silu(gate) * up).astype(BF16)

    @pl.when(i > nused_ref[0])
    def _():
        y_ref[...] = jnp.zeros_like(y_ref)


def _experts(x_rows, block_exp, nvalid, n_used, we_gate, we_up, we_down, layer, bm):
    n_rows = x_rows.shape[0] // ROW_SUB
    d, ff = we_gate.shape[-2:]
    n_blocks = n_rows // bm
    last = n_blocks - 1

    def up_map(i, be, nv, nu):
        return (layer, be[jnp.minimum(i, last)], 0, 0)

    def down_map(i, be, nv, nu):
        return (layer, be[jnp.maximum(i - 1, 0)], 0, 0)

    grid_spec = pltpu.PrefetchScalarGridSpec(
        num_scalar_prefetch=3,
        grid=(n_blocks + 1,),
        in_specs=[
            pl.BlockSpec((bm * ROW_SUB, ROW_LANES), lambda i, be, nv, nu: (jnp.minimum(i, last), 0)),
            pl.BlockSpec((1, 1, d, ff), up_map),
            pl.BlockSpec((1, 1, d, ff), up_map),
            pl.BlockSpec((1, 1, ff, d), down_map),
        ],
        out_specs=pl.BlockSpec((bm * ROW_SUB, ROW_LANES), lambda i, be, nv, nu: (jnp.maximum(i - 1, 0), 0)),
        scratch_shapes=[pltpu.VMEM((d, ff), BF16), pltpu.VMEM((d, ff), BF16), pltpu.VMEM((ff, d), BF16),
                        pltpu.VMEM((bm, ff), BF16)],
    )
    return pl.pallas_call(
        _expert_kernel,
        grid_spec=grid_spec,
        out_shape=jax.ShapeDtypeStruct(x_rows.shape, x_rows.dtype),
        compiler_params=_cparams(("arbitrary",)),
        name="experts",
    )(block_exp, nvalid, n_used, x_rows, we_gate, we_up, we_down)


def _final_kernel(x_ref, gt_ref, g4_ref, h2_ref, *refs, dense):
    s, t, d = x_ref.shape
    rows = s * t
    if dense:
        yr_ref, wg_ref, wu_ref, wd_ref, o_ref = refs
    else:
        yk_ref, wk_ref, wg_ref, wu_ref, wd_ref, o_ref = refs
    h2 = _unpack_chunks(_load_rows(h2_ref, rows)).astype(BF16)
    f = _dot((_silu(_dot(h2, wg_ref[...])) * _dot(h2, wu_ref[...])).astype(BF16), wd_ref[...])
    if dense:
        f = f + yr_ref[...]
    else:
        wk = wk_ref[...]
        for k in range(TOP_K):
            f = f + wk[:, k:k + 1] * _unpack_chunks(_load_rows(yk_ref, rows, lead=k))
    o_ref[...] = x_ref[...] + gt_ref[...] * _rms(f, g4_ref[0]).reshape(s, t, d)


def _final(xmid3, ada3, g4, h2_rows, routed, ws_gate, ws_up, ws_down, layer, tm):
    nseq, tseq, d = xmid3.shape
    ntok = nseq * tseq
    s_blk, t_blk, tps = _tile_split(tseq, tm)
    dense = not isinstance(routed, tuple)

    def x_map(i):
        return (i // tps, i % tps, 0)

    if dense:
        routed_specs = [pl.BlockSpec((tm, d), lambda i: (i, 0))]
        routed = (routed,)
    else:
        routed_specs = [pl.BlockSpec((TOP_K, tm * ROW_SUB, ROW_LANES), lambda i: (0, i, 0)),
                        pl.BlockSpec((tm, TOP_K), lambda i: (i, 0))]
    return pl.pallas_call(
        functools.partial(_final_kernel, dense=dense),
        grid=(ntok // tm,),
        in_specs=[
            pl.BlockSpec((s_blk, t_blk, d), x_map),
            pl.BlockSpec((s_blk, 1, d), lambda i: (i // tps, 0, 5)),
            pl.BlockSpec((1, 1, d), lambda i: (layer, 0, 0)),
            pl.BlockSpec((tm * ROW_SUB, ROW_LANES), lambda i: (i, 0)),
        ] + routed_specs + [
            pl.BlockSpec(ws_gate.shape, lambda i: (0, 0)),
            pl.BlockSpec(ws_up.shape, lambda i: (0, 0)),
            pl.BlockSpec(ws_down.shape, lambda i: (0, 0)),
        ],
        out_specs=pl.BlockSpec((s_blk, t_blk, d), x_map),
        out_shape=jax.ShapeDtypeStruct((nseq, tseq, d), F32),
        compiler_params=_cparams(("parallel",)),
        name="final",
    )(xmid3, ada3, g4, h2_rows, *routed, ws_gate, ws_up, ws_down)


def _dense_moe_kernel(h2_ref, wm_ref, wg_ref, wu_ref, wd_ref, y_ref, x_s, eye_s):
    e = pl.program_id(0)
    rows = y_ref.shape[0]

    @pl.when(e == 0)
    def _():
        x_s[...] = _unpack_chunks(_load_rows(h2_ref, rows)).astype(BF16)
        r_i = lax.broadcasted_iota(jnp.int32, (rows, rows), 0)
        c_i = lax.broadcasted_iota(jnp.int32, (rows, rows), 1)
        eye_s[...] = jnp.where(r_i == c_i, 1.0, 0.0).astype(BF16)
        y_ref[...] = jnp.zeros_like(y_ref)

    x = x_s[...]
    gate = _dot(x, wg_ref[0, 0].astype(BF16))
    up = _dot(x, wu_ref[0, 0].astype(BF16))
    y = _dot((_silu(gate) * up).astype(BF16), wd_ref[0, 0].astype(BF16))
    w_row = wm_ref[0]
    hi = w_row.astype(BF16).astype(F32)
    mid = (w_row - hi).astype(BF16).astype(F32)
    lo = w_row - hi - mid
    piece = lax.broadcasted_iota(jnp.int32, (8, rows), 0)
    parts = jnp.where(piece == 0, hi, jnp.where(piece == 1, mid, jnp.where(piece == 2, lo, 0.0))).astype(BF16)
    w_col = jnp.sum(_dot_nt(eye_s[...], parts), axis=1, keepdims=True)
    y_ref[...] += w_col * y


def _dense_moe(h2_rows, wfull, we_gate, we_up, we_down, layer):
    n_exp, ntok = wfull.shape
    d, ff = we_gate.shape[-2:]
    return pl.pallas_call(
        _dense_moe_kernel,
        grid=(n_exp,),
        in_specs=[
            pl.BlockSpec(h2_rows.shape, lambda e: (0, 0)),
            pl.BlockSpec((1, 1, ntok), lambda e: (e, 0, 0)),
            pl.BlockSpec((1, 1, d, ff), lambda e: (layer, e, 0, 0)),
            pl.BlockSpec((1, 1, d, ff), lambda e: (layer, e, 0, 0)),
            pl.BlockSpec((1, 1, ff, d), lambda e: (layer, e, 0, 0)),
        ],
        out_specs=pl.BlockSpec((ntok, d), lambda e: (0, 0)),
        out_shape=jax.ShapeDtypeStruct((ntok, d), F32),
        scratch_shapes=[pltpu.VMEM((ntok, d), BF16), pltpu.VMEM((ntok, ntok), BF16)],
        compiler_params=_cparams(("arbitrary",)),
        name="dense_moe",
    )(h2_rows, wfull.reshape(n_exp, 1, ntok), we_gate, we_up, we_down)


def _t5_bucket(dist):
    n = np.asarray(dist, dtype=np.int64)
    exact = N_BUCKETS // 2
    log_ratio = np.log(np.maximum(n, 1) / exact) / np.log(MAX_DISTANCE / exact)
    large = np.minimum(exact + (log_ratio * (N_BUCKETS - exact)).astype(np.int64), N_BUCKETS - 1)
    return np.where(n < exact, n, large).astype(np.int32)


def _bias_table(rel_bias, h0, h1, steps, dil):
    steps = np.asarray(steps)
    buckets = np.where(steps >= 0, _t5_bucket(np.maximum(steps, 0) * dil), -1)
    onehot = buckets[..., None] == np.arange(N_BUCKETS)
    table = rel_bias[:, h0:h1].T.astype(F32).reshape((h1 - h0,) + (1,) * steps.ndim + (N_BUCKETS,))
    val = jnp.sum(jnp.where(onehot[None], table, 0.0), axis=-1)
    return jnp.where((steps >= 0)[None], val, NEG_INF)


def _band_steps():
    dist = BLOCK + np.arange(BLOCK)[:, None] - np.arange(2 * BLOCK)[None, :]
    return np.where((dist >= 0) & (dist <= BLOCK), dist, -1)


def _sample_steps(n_buf, dil, t_new):
    t = np.arange(t_new)[:, None]
    delta_c = n_buf + t - np.arange(n_buf)[None, :]
    ok_c = (delta_c % dil == 0) & (delta_c // dil <= BLOCK)
    lane = np.arange(BLOCK)[None, :]
    delta_n = t - (lane - (BLOCK - t_new))
    ok_n = (lane >= BLOCK - t_new) & (delta_n >= 0) & (delta_n % dil == 0) & (delta_n // dil <= BLOCK)
    return np.where(ok_c, delta_c // dil, -1), np.where(ok_n, delta_n // dil, -1)


def _route_plan(idx, rank, counts, bm):
    n_tok = idx.shape[1]
    n_blocks = -(-(n_tok * TOP_K + N_EXPERTS * (bm - 1)) // bm)
    counts = counts.reshape(N_EXPERTS)
    padded = (counts + bm - 1) // bm * bm
    pad_end = jnp.cumsum(padded)
    pad_start = pad_end - padded
    experts = jnp.arange(N_EXPERTS, dtype=jnp.int32)
    onehot = idx[:, :, None] == experts[None, None, :]
    dest = rank + jnp.sum(jnp.where(onehot, pad_start[None, None, :], 0), axis=-1)
    starts = jnp.arange(n_blocks, dtype=jnp.int32) * bm
    block_exp = jnp.minimum(jnp.sum((starts[:, None] >= pad_end[None, :]).astype(jnp.int32), axis=1), N_EXPERTS - 1)
    hot = block_exp[:, None] == experts[None, :]
    blk_cnt = jnp.sum(jnp.where(hot, counts[None, :], 0), axis=1)
    blk_start = jnp.sum(jnp.where(hot, pad_start[None, :], 0), axis=1)
    nvalid = jnp.clip(blk_cnt - (starts - blk_start), 0, bm).astype(jnp.int32)
    n_used = (pad_end[-1] // bm).astype(jnp.int32).reshape(1)
    return dest.astype(jnp.int32), block_exp.astype(jnp.int32), nvalid, n_used, n_blocks * bm


def _moe_routed(h2_rows, idx, rank, counts, we_gate, we_up, we_down, layer, bm):
    n_tok = idx.shape[1]
    dest, block_exp, nvalid, n_used, n_rows = _route_plan(idx, rank, counts, bm)
    x_sorted = _sc_scatter_rows(h2_rows.reshape(n_tok, ROW_SUB, ROW_LANES), dest, n_rows)
    y_sorted = _experts(x_sorted.reshape(n_rows * ROW_SUB, ROW_LANES), block_exp, nvalid, n_used,
                        we_gate, we_up, we_down, layer, bm)
    yk = _sc_gather_rows(y_sorted.reshape(n_rows, ROW_SUB, ROW_LANES), dest)
    return yk.reshape(TOP_K, n_tok * ROW_SUB, ROW_LANES)


def _to_time_minor(c):
    depth, nseq, n_buf = c.shape[:3]
    return jnp.transpose(c, (0, 1, 3, 4, 5, 2)).reshape(depth, nseq, -1, n_buf)


def _from_time_minor(ct, n_heads):
    lead = ct.shape[:-2]
    rows = ct.shape[-1]
    nl = len(lead)
    x = ct.reshape(lead + (2, n_heads, HEAD_DIM, rows))
    return jnp.transpose(x, tuple(range(nl)) + (nl + 3, nl, nl + 1, nl + 2))


def kernel(x_prompt, x_sample, cache_b1, cache_b2, cache_b3, cache_c, state_pool, c_prompt, c_sample, rel_bias, w_ada, b_ada, g_pre_mix, g_post_mix, g_pre_ffn, g_post_ffn, w_in, pool_w, pool_scale, w_b_up, w_c_up, sinks, w_out, w_router, router_bias, we_gate, we_up, we_down, ws_gate, ws_up, ws_down):
    batch, seq, d = x_prompt.shape
    dec_batch, dec_seq, _ = x_sample.shape
    depth = w_in.shape[0]
    tm_p = 256
    tm_s = 128
    n_dil = len(DIL_GROUPS)
    dils = tuple(dil for _, dil in DIL_GROUPS)

    n_seq_all = batch + dec_batch
    pad = -n_seq_all % 8
    c_all = jnp.concatenate([c_prompt, c_sample, jnp.zeros((pad, d), F32)], axis=0)
    ada = _ada_all(c_all, w_ada, b_ada)

    head0 = [g * DIL_HEADS for g in range(n_dil)] + [n_dil * DIL_HEADS]
    head1 = [(g + 1) * DIL_HEADS for g in range(n_dil)] + [n_dil * DIL_HEADS + SWA_HEADS]
    band = []
    for g, dil in enumerate(dils + (1,)):
        tab = _bias_table(rel_bias, head0[g], head1[g], _band_steps(), dil)
        n_h = head1[g] - head0[g]
        band.append(tab.reshape(n_h // 2, 2, BLOCK, 2 * BLOCK).transpose(0, 2, 1, 3).reshape(n_h // 2, BLOCK, 4 * BLOCK))
    caches = (cache_b1, cache_b2, cache_b3, cache_c)
    samp = []
    for g, dil in enumerate(dils + (1,)):
        n_buf = caches[g].shape[2]
        sc_steps, sn_steps = _sample_steps(n_buf, dil, dec_seq)
        rows = (head1[g] - head0[g]) * dec_seq
        samp.append((_bias_table(rel_bias, head0[g], head1[g], sc_steps, dil).reshape(rows, n_buf),
                     _bias_table(rel_bias, head0[g], head1[g], sn_steps, dil).reshape(rows, BLOCK)))

    splits = np.cumsum([0, POOL_WIDTH, 3 * DIL_W, 3 * DIL_W, 3 * DIL_W, SWA_QW, SWA_KW, SWA_KW, 3 * d])
    col_slices = [(int(splits[0]), int(splits[1]))]
    for g in range(n_dil):
        for part in range(3):
            c0 = int(splits[1 + part]) + g * DIL_W
            col_slices.append((c0, c0 + DIL_W))
    col_slices.append((int(splits[4]), int(splits[8])))

    vec3 = lambda a: a.reshape(depth, 1, -1)
    g1, g2, g3, g4 = vec3(g_pre_mix), vec3(g_post_mix), vec3(g_pre_ffn), vec3(g_post_ffn)
    ps3, rb3 = vec3(pool_scale), router_bias.reshape(depth, N_EXPERTS, 1)
    cache_t = [_to_time_minor(c) for c in caches]
    pool_halo = jnp.pad(state_pool, ((0, 0), (0, 0), (POOL_HALO - state_pool.shape[2], 0), (0, 0)))
    pw = pool_w.astype(BF16)

    xp, xs = x_prompt, x_sample
    cache_out = [None] * (n_dil + 1)
    st_p = [[] for _ in range(n_dil + 2)]
    pool_s = []
    keeps_p = tuple(min(win, seq) for win, _ in DIL_GROUPS) + (min(BLOCK, seq),)
    n_heads = (DIL_HEADS,) * n_dil + (SWA_HEADS,)
    n_kvs = (DIL_HEADS,) * n_dil + (SWA_KV_HEADS,)
    for l in range(depth):
        w_perm = jnp.concatenate([w_in[l][:, a:b] for a, b in col_slices], axis=1).astype(BF16)
        wb, wc, wo, wr = w_b_up[l].astype(BF16), w_c_up[l].astype(BF16), w_out[l].astype(BF16), w_router[l].T.astype(BF16)
        wsg, wsu, wsd = ws_gate[l].astype(BF16), ws_up[l].astype(BF16), ws_down[l].astype(BF16)
        post_w = (g2, g3, pw, ps3, wb, wc, wo, wr, rb3)
        ada_p = ada[l, :batch].reshape(batch, 1, -1)
        ada_s = ada[l, batch:n_seq_all].reshape(dec_batch, 1, -1)
        sink_rows = jnp.broadcast_to(jnp.repeat(sinks[l].astype(F32), dec_seq)[:, None], (SWA_HEADS * dec_seq, BLOCK))

        outs = _in_proj(xp, ada_p, g1, w_perm, l, tm_p, keeps_p, True)
        u, gl, states = outs[0], outs[3 + 2 * n_dil], outs[4 + 2 * n_dil:]
        attn = []
        for g in range(n_dil):
            o, lse = _band_attn(outs[1 + 2 * g], outs[2 + 2 * g], band[g], DIL_HEADS, DIL_HEADS)
            if dils[g] == 1:
                o, lse = o.reshape(batch * seq, DIL_W), lse.reshape(batch * seq, DIL_W)
            attn += [o, lse]
        oc = _band_attn(outs[1 + 2 * n_dil], outs[2 + 2 * n_dil], band[n_dil], SWA_HEADS, SWA_KV_HEADS, sinks=sinks[l])
        attn.append(oc.reshape(batch * seq, SWA_QW))
        xmid, h2, idx, rank, wk, counts, _ = _post_mixer(xp, ada_p, u, u, attn, gl, post_w, l, tm_p, False, dils)
        y_k = _moe_routed(h2, idx, rank, counts, we_gate, we_up, we_down, l, 512)
        xp = _final(xmid, ada_p, g4, h2, (y_k, wk), wsg, wsu, wsd, l, tm_p)
        for k in range(n_dil + 1):
            st_p[k].append(_from_time_minor(states[k], n_kvs[k]))
        st_p[n_dil + 1].append(u[:, seq - (POOL_HALO - 1):])

        outs = _in_proj(xs, ada_s, g1, w_perm, l, tm_s, (dec_seq,) * (n_dil + 1), False)
        u, gl, states = outs[0], outs[2 + n_dil], outs[3 + n_dil:]
        attn = []
        for g in range(n_dil + 1):
            res = _sample_attn(outs[1 + g], states[g], cache_t[g], cache_out[g], samp[g][0], samp[g][1], l,
                               n_heads[g], n_kvs[g], dec_seq, sink_rows=sink_rows if g == n_dil else None)
            attn += list(res[:-1])
            cache_out[g] = res[-1]
        xmid, h2, _, _, _, _, wfull = _post_mixer(xs, ada_s, u, pool_halo[l], attn, gl, post_w, l, tm_s, True,
                                                  (1,) * n_dil)
        y_r = _dense_moe(h2, wfull, we_gate, we_up, we_down, l)
        xs = _final(xmid, ada_s, g4, h2, y_r, wsg, wsu, wsd, l, tm_s)
        pool_s.append(jnp.concatenate([state_pool[l], u], axis=1)[:, -(POOL_HALO - 1):])

    b1_p, b2_p, b3_p, c_p, pool_p = [jnp.stack(s, axis=0) for s in st_p]
    outs_s = [_from_time_minor(co, n_kvs[g]) for g, co in enumerate(cache_out)]
    return (xp, xs, b1_p, b2_p, b3_p, c_p, pool_p, outs_s[0], outs_s[1], outs_s[2], outs_s[3], jnp.stack(pool_s, axis=0))
```

```python
import functools

import numpy as np
import jax
import jax.numpy as jnp
from jax import lax
from jax.experimental import pallas as pl
from jax.experimental.pallas import tpu as pltpu
from jax.experimental.pallas import tpu_sc as plsc

F32 = jnp.float32
BF16 = jnp.bfloat16

HEAD_DIM = 64
SCALE = HEAD_DIM ** -0.5
BLOCK = 128
POOL_WINDOWS = (2, 4, 8, 16)
POOL_CH = 128
POOL_WIDTH = len(POOL_WINDOWS) * POOL_CH
POOL_HALO = 16
DIL_GROUPS = ((128, 1), (512, 4), (2048, 16))
DIL_HEADS = 4
DIL_W = DIL_HEADS * HEAD_DIM
SWA_HEADS = 8
SWA_KV_HEADS = 2
SWA_QW = SWA_HEADS * HEAD_DIM
SWA_KW = SWA_KV_HEADS * HEAD_DIM
N_BUCKETS = 32
MAX_DISTANCE = 2048
N_EXPERTS = 64
TOP_K = 8
ROUTED_SCALE = 2.5
EPS = 1e-6
NEG_INF = float("-inf")

V7X_VMEM_BYTES = 64 * 1024 * 1024
VMEM_LIMIT = 56 * 1024 * 1024


def _cparams(sem):
    return pltpu.CompilerParams(dimension_semantics=sem, vmem_limit_bytes=VMEM_LIMIT)


def _rms(x, g):
    ms = jnp.mean(x * x, axis=-1, keepdims=True)
    return x * lax.rsqrt(ms + EPS) * g


def _sigmoid(x):
    return 1.0 / (1.0 + jnp.exp(-x))


def _silu(x):
    return x * _sigmoid(x)


def _dot(a, b):
    return jnp.dot(a, b, preferred_element_type=F32)


def _dot_nt(a, b):
    return lax.dot_general(a, b, (((1,), (1,)), ((), ())), preferred_element_type=F32)


def _tile_split(tseq, tm):
    if tseq >= tm:
        return 1, tm, tseq // tm
    return tm // tseq, tseq, 1


_TIE_SPEC = pl.BlockSpec(memory_space=pl.ANY)


ROW_SUB = 4
ROW_LANES = 128
ROW_DTYPE = jnp.uint32
HI_MASK = 0xFFFF0000


def _pack_chunks(x):
    half = x.shape[1] // 2
    hi = pltpu.bitcast(x[:, :half].astype(BF16).astype(F32), jnp.uint32) & jnp.uint32(HI_MASK)
    lo = pltpu.bitcast(x[:, half:].astype(BF16).astype(F32), jnp.uint32) >> 16
    w = hi | lo
    return [w[:, j * ROW_LANES:(j + 1) * ROW_LANES] for j in range(ROW_SUB)]


def _unpack_chunks(chunks):
    his = [pltpu.bitcast(w & jnp.uint32(HI_MASK), F32) for w in chunks]
    los = [pltpu.bitcast(w << 16, F32) for w in chunks]
    return jnp.concatenate(his + los, axis=1)


def _store_rows(ref, chunks, rows, first=0):
    for j, c in enumerate(chunks):
        ref[pl.ds(first * ROW_SUB + j, rows, stride=ROW_SUB), :] = c


def _load_rows(ref, rows, first=0, lead=None):
    if lead is None:
        return [ref[pl.ds(first * ROW_SUB + j, rows, stride=ROW_SUB), :] for j in range(ROW_SUB)]
    return [ref[lead, pl.ds(first * ROW_SUB + j, rows, stride=ROW_SUB), :] for j in range(ROW_SUB)]


SC_CORES = 2
SC_SUBCORES = 16
SC_WORKERS = SC_CORES * SC_SUBCORES
SC_CHUNK = 64


def _sc_scatter_rows(src, dest, n_out):
    n, sub, lanes = src.shape
    kk = dest.shape[0]
    ch = min(SC_CHUNK, n // SC_WORKERS)
    n_chunks = n // (SC_WORKERS * ch)
    idx3 = dest.reshape(kk, n // ch, ch).transpose(1, 0, 2)
    mesh = plsc.VectorSubcoreMesh(core_axis_name="c", subcore_axis_name="s")

    @functools.partial(
        pl.kernel, mesh=mesh,
        out_type=jax.ShapeDtypeStruct((n_out, sub, lanes), src.dtype),
        scratch_types=[pltpu.VMEM((kk, ch), jnp.int32), pltpu.VMEM((ch, sub, lanes), src.dtype),
                       pltpu.SemaphoreType.DMA],
    )
    def scatter_kernel(src_hbm, idx_hbm, out_hbm, idx_v, rows_v, sem):
        wid = lax.axis_index("s") * SC_CORES + lax.axis_index("c")

        @pl.loop(0, n_chunks)
        def _(c):
            j = wid * n_chunks + c
            pltpu.sync_copy(idx_hbm.at[j], idx_v)
            pltpu.sync_copy(src_hbm.at[pl.ds(j * ch, ch)], rows_v)
            copies = [pltpu.async_copy(rows_v, out_hbm.at[idx_v.at[q]], sem) for q in range(kk)]
            for cp in copies:
                cp.wait()

    return scatter_kernel(src, idx3)


def _sc_gather_rows(table, dest):
    v, sub, lanes = table.shape
    kk, n = dest.shape
    total = n * kk
    ch = min(SC_CHUNK, total // SC_WORKERS)
    n_chunks = total // (SC_WORKERS * ch)
    idx2 = dest.reshape(total // ch, ch)
    mesh = plsc.VectorSubcoreMesh(core_axis_name="c", subcore_axis_name="s")

    assert n_chunks % 2 == 0, "the two-buffer gather pipeline walks chunks in pairs"

    @functools.partial(
        pl.kernel, mesh=mesh,
        out_type=jax.ShapeDtypeStruct((total, sub, lanes), table.dtype),
        scratch_types=[pltpu.VMEM((n_chunks, ch), jnp.int32), pltpu.VMEM((2, ch, sub, lanes), table.dtype),
                       pltpu.SemaphoreType.DMA((2,)), pltpu.SemaphoreType.DMA((2,))],
    )
    def gather_kernel(tab_hbm, idx_hbm, out_hbm, idx_v, rows_v, gsem, wsem):
        wid = lax.axis_index("s") * SC_CORES + lax.axis_index("c")
        base = wid * n_chunks
        pltpu.sync_copy(idx_hbm.at[pl.ds(base, n_chunks)], idx_v)

        def gather(c, b):
            return pltpu.make_async_copy(tab_hbm.at[idx_v.at[c]], rows_v.at[b], gsem.at[b])

        def write(c, b):
            return pltpu.make_async_copy(rows_v.at[b], out_hbm.at[pl.ds((base + c) * ch, ch)], wsem.at[b])

        gather(0, 0).start()

        @pl.loop(0, n_chunks, step=2)
        def _(c0):
            for b in range(2):
                c = c0 + b
                gather(c, b).wait()

                @pl.when(c + 1 < n_chunks)
                def _():
                    @pl.when(c >= 1)
                    def _():
                        write(c - 1, 1 - b).wait()

                    gather(c + 1, 1 - b).start()

                write(c, b).start()

        write(n_chunks - 2, 0).wait()
        write(n_chunks - 1, 1).wait()

    return gather_kernel(table, idx2).reshape(kk, n, sub, lanes)


def _ada_kernel(c_ref, w_ref, b_ref, o_ref):
    c = _silu(c_ref[...]).astype(BF16)
    o_ref[0] = _dot(c, w_ref[0].astype(BF16)) + b_ref[0]


def _ada_all(c_all, w_ada, b_ada):
    depth, d, n = w_ada.shape
    rows = c_all.shape[0]
    tn = 1536
    return pl.pallas_call(
        _ada_kernel,
        grid=(depth, n // tn),
        in_specs=[
            pl.BlockSpec((rows, d), lambda l, j: (0, 0)),
            pl.BlockSpec((1, d, tn), lambda l, j: (l, 0, j)),
            pl.BlockSpec((1, 1, tn), lambda l, j: (l, 0, j)),
        ],
        out_specs=pl.BlockSpec((1, rows, tn), lambda l, j: (l, 0, j)),
        out_shape=jax.ShapeDtypeStruct((depth, rows, n), F32),
        compiler_params=_cparams(("parallel", "parallel")),
        name="ada",
    )(c_all, w_ada, b_ada.reshape(depth, 1, n))


def _in_proj_kernel(x_ref, sh_ref, sc_ref, g_ref, w_ref, tie_ref, *refs, dils, fold):
    s, t, d = x_ref.shape
    rows = s * t
    h = _rms(x_ref[...], g_ref[...]) * (1.0 + sc_ref[...]) + sh_ref[...]
    hb = h.reshape(rows, d).astype(BF16)
    n_g = len(dils)
    if fold:
        u_ref = refs[0]
        q_refs = refs[1:1 + 2 * n_g:2] + (refs[1 + 2 * n_g],)
        kv_refs = refs[2:2 + 2 * n_g:2] + (refs[2 + 2 * n_g],)
        gl_ref = refs[3 + 2 * n_g]
        st_refs = refs[4 + 2 * n_g:5 + 3 * n_g]
        zs_ref = refs[5 + 3 * n_g]
    else:
        u_ref = refs[0]
        q_refs = refs[1:2 + n_g]
        kv_refs = (None,) * (n_g + 1)
        gl_ref = refs[2 + n_g]
        st_refs = refs[3 + n_g:4 + 2 * n_g]
        zs_ref = None

    def proj(c0, c1):
        return _dot(hb, w_ref[:, c0:c1])

    u_ref[...] = proj(0, POOL_WIDTH).reshape(s, t, POOL_WIDTH)
    off = POOL_WIDTH
    for g, dil in enumerate(tuple(dils) + (1,)):
        qw = DIL_W if g < n_g else SWA_QW
        kvw = 2 * DIL_W if g < n_g else 2 * SWA_KW
        z = proj(off, off + qw + kvw)
        off += qw + kvw
        zq = z[:, :qw] * SCALE
        zkv = z[:, qw:]
        st = st_refs[g]
        st[...] = zkv[rows - st.shape[-1]:, :].T
        if not fold:
            q_refs[g][...] = zq
        elif dil == 1:
            q_refs[g][0, 0] = zq.astype(BF16)
            kv_refs[g][0, 0] = zkv.astype(BF16)
        else:
            zf = jnp.concatenate([zq, zkv], axis=1)
            n_c = zf.shape[1] // BLOCK
            for c in range(n_c):
                zs_ref[c] = zf[:, c * BLOCK:(c + 1) * BLOCK]
            n = rows // dil
            for r in range(dil):
                part = jnp.concatenate([zs_ref[c, pl.ds(r, n, stride=dil), :] for c in range(n_c)], axis=1)
                q_refs[g][0, r] = part[:, :qw].astype(BF16)
                kv_refs[g][0, r] = part[:, qw:].astype(BF16)
    gl_ref[...] = proj(off, off + 3 * d)


def _in_proj(x3, ada3, g_pre, w_perm, layer, tm, keeps, fold, tie):
    nseq, tseq, d = x3.shape
    ntok = nseq * tseq
    s_blk, t_blk, tps = _tile_split(tseq, tm)
    n_tiles = ntok // tm
    in_w = w_perm.shape[-1]
    dils = tuple(dil for _, dil in DIL_GROUPS)
    widths = [(DIL_W, 2 * DIL_W)] * len(dils) + [(SWA_QW, 2 * SWA_KW)]

    def x_map(i):
        return (i // tps, i % tps, 0)

    def tok_map(i):
        return (i, 0)

    out_shapes = [jax.ShapeDtypeStruct((nseq, tseq, POOL_WIDTH), F32)]
    out_specs = [pl.BlockSpec((s_blk, t_blk, POOL_WIDTH), x_map)]
    for (qw, kvw), dil in zip(widths, dils + (1,)):
        if fold:
            for w in (qw, kvw):
                out_shapes.append(jax.ShapeDtypeStruct((nseq, dil, tseq // dil, w), BF16))
                out_specs.append(pl.BlockSpec((1, dil, tm // dil, w), lambda i: (i // tps, 0, i % tps, 0)))
        else:
            out_shapes.append(jax.ShapeDtypeStruct((ntok, qw), F32))
            out_specs.append(pl.BlockSpec((tm, qw), tok_map))
    out_shapes.append(jax.ShapeDtypeStruct((ntok, 3 * d), F32))
    out_specs.append(pl.BlockSpec((tm, 3 * d), tok_map))
    for keep, (_, kvw) in zip(keeps, widths):
        if fold:
            sb = min(tm, keep)
            bps = keep // sb
            first = (tseq - keep) // tm

            def st_map(i, bps=bps, first=first):
                return (i // tps, 0, jnp.maximum(i % tps - first, 0) * (1 if bps > 1 else 0))

            out_shapes.append(jax.ShapeDtypeStruct((nseq, kvw, keep), F32))
            out_specs.append(pl.BlockSpec((None, kvw, sb), st_map))
        else:
            out_shapes.append(jax.ShapeDtypeStruct((kvw, ntok), F32))
            out_specs.append(pl.BlockSpec((kvw, tm), lambda i: (0, i)))
    scratch = [pltpu.VMEM((3 * DIL_W // BLOCK, tm, BLOCK), F32)] if fold else []
    return pl.pallas_call(
        functools.partial(_in_proj_kernel, dils=dils, fold=fold),
        grid=(n_tiles,),
        in_specs=[
            pl.BlockSpec((s_blk, t_blk, d), x_map),
            pl.BlockSpec((s_blk, 1, d), lambda i: (i // tps, 0, 0)),
            pl.BlockSpec((s_blk, 1, d), lambda i: (i // tps, 0, 1)),
            pl.BlockSpec((1, 1, d), lambda i: (layer, 0, 0)),
            pl.BlockSpec((d, in_w), lambda i: (0, 0)),
            _TIE_SPEC,
        ],
        out_specs=out_specs,
        out_shape=out_shapes,
        scratch_shapes=scratch,
        compiler_params=_cparams(("arbitrary",)),
        name="in_proj",
    )(x3, ada3, ada3, g_pre, w_perm, tie)


PAIR = 2 * HEAD_DIM


def _band_attn_kernel(*refs, n_heads, n_kv, nq, with_sink):
    if with_sink:
        sink_ref, q_ref, kv_ref, halo_ref, bm_ref, mask_ref, o_ref = refs
        lse_ref = None
    else:
        q_ref, kv_ref, halo_ref, bm_ref, mask_ref, o_ref, lse_ref = refs
        sink_ref = None
    kw = n_kv * HEAD_DIM
    grp = n_heads // n_kv
    chunk = pl.program_id(2)
    lo, hi = mask_ref[0], mask_ref[1]
    low_lanes = lax.broadcasted_iota(jnp.int32, (BLOCK, PAIR), 1) < HEAD_DIM
    col = lax.broadcasted_iota(jnp.int32, (BLOCK, 4 * BLOCK), 1)
    prev_penalty = jnp.where(col % (2 * BLOCK) < BLOCK, NEG_INF, 0.0)

    def swap_halves(x):
        return jnp.concatenate([x[:, HEAD_DIM:], x[:, :HEAD_DIM]], axis=1)

    def pair_sources(kv_blk, p):
        if grp == 1:
            k = kv_blk[:, p * PAIR:(p + 1) * PAIR]
            v = kv_blk[:, kw + p * PAIR:kw + (p + 1) * PAIR]
            return (k, k), (v, v)
        kh = (2 * p) // grp
        c0 = (kh // 2) * PAIR
        k = kv_blk[:, c0:c0 + PAIR]
        v = kv_blk[:, kw + c0:kw + c0 + PAIR]
        ks, vs = swap_halves(k), swap_halves(v)
        return ((k, ks), (v, vs)) if kh % 2 == 0 else ((ks, k), (vs, v))

    def one_block(r0, kv_prev, first):
        qb = q_ref[pl.ds(r0, BLOCK), :]
        kv_cur = kv_ref[pl.ds(r0, BLOCK), :]
        for p in range(n_heads // 2):
            (kpe, kpo), (vpe, vpo) = pair_sources(kv_prev, p)
            (kce, kco), (vce, vco) = pair_sources(kv_cur, p)
            k_blk = jnp.concatenate([kpe * lo, kce * lo, kpo * hi, kco * hi], axis=0)
            v_blk = jnp.concatenate([
                jnp.concatenate([vpe * lo, lo], axis=1), jnp.concatenate([vce * lo, lo], axis=1),
                jnp.concatenate([vpo * hi, hi], axis=1), jnp.concatenate([vco * hi, hi], axis=1)], axis=0)
            s = _dot_nt(qb[:, p * PAIR:(p + 1) * PAIR], k_blk) + bm_ref[p]
            if first is not None:
                s = s + jnp.where(first, prev_penalty, 0.0)
            s_e, s_o = s[:, :2 * BLOCK], s[:, 2 * BLOCK:]
            m_e = jnp.max(jnp.maximum(s_e[:, :BLOCK], s_e[:, BLOCK:]), axis=-1, keepdims=True)
            m_o = jnp.max(jnp.maximum(s_o[:, :BLOCK], s_o[:, BLOCK:]), axis=-1, keepdims=True)
            if with_sink:
                m_e = jnp.maximum(m_e, sink_ref[2 * p])
                m_o = jnp.maximum(m_o, sink_ref[2 * p + 1])
            pr = jnp.concatenate([jnp.exp(s_e - m_e), jnp.exp(s_o - m_o)], axis=1).astype(BF16)
            res = _dot(pr, v_blk)
            den = res[:, PAIR:]
            if with_sink:
                den = den + jnp.where(low_lanes, jnp.exp(sink_ref[2 * p] - m_e), jnp.exp(sink_ref[2 * p + 1] - m_o))
            o_ref[pl.ds(r0, BLOCK), p * PAIR:(p + 1) * PAIR] = res[:, :PAIR] / den
            if lse_ref is not None:
                lse_ref[pl.ds(r0, BLOCK), p * PAIR:(p + 1) * PAIR] = jnp.where(low_lanes, m_e, m_o) + jnp.log(den)

    one_block(0, halo_ref[...], chunk == 0)

    def body(j, carry):
        r0 = pl.multiple_of(j * BLOCK, BLOCK)
        one_block(r0, kv_ref[pl.ds(r0 - BLOCK, BLOCK), :], None)
        return carry

    if nq > 1:
        lax.fori_loop(1, nq, body, 0)


def _band_attn(q, kv, bm, n_heads, n_kv, sinks=None):
    batch, dil, fold, qw = q.shape
    kvw = kv.shape[-1]
    nq = min(8, fold // BLOCK)
    rows = nq * BLOCK
    n_chunks = fold // rows
    with_sink = sinks is not None
    lanes_low = np.arange(PAIR) < HEAD_DIM
    masks = jnp.asarray(np.broadcast_to(np.stack([lanes_low, ~lanes_low])[:, None, :], (2, BLOCK, PAIR)), F32).astype(BF16)
    in_specs = [
        pl.BlockSpec((None, None, rows, qw), lambda b, r, c: (b, r, c, 0)),
        pl.BlockSpec((None, None, rows, kvw), lambda b, r, c: (b, r, c, 0)),
        pl.BlockSpec((None, None, BLOCK, kvw), lambda b, r, c: (b, r, jnp.maximum(c * nq - 1, 0), 0)),
        pl.BlockSpec(bm.shape, lambda b, r, c: (0, 0, 0)),
        pl.BlockSpec(masks.shape, lambda b, r, c: (0, 0, 0)),
    ]
    args = [q, kv, kv, bm, masks]
    o_spec = pl.BlockSpec((None, None, rows, qw), lambda b, r, c: (b, r, c, 0))
    o_shape = jax.ShapeDtypeStruct((batch, dil, fold, qw), F32)
    if with_sink:
        in_specs = [pl.BlockSpec(memory_space=pltpu.SMEM)] + in_specs
        args = [sinks] + args
        out_specs, out_shape = o_spec, o_shape
    else:
        out_specs, out_shape = [o_spec, o_spec], [o_shape, o_shape]
    return pl.pallas_call(
        functools.partial(_band_attn_kernel, n_heads=n_heads, n_kv=n_kv, nq=nq, with_sink=with_sink),
        grid=(batch, dil, n_chunks),
        in_specs=in_specs,
        out_specs=out_specs,
        out_shape=out_shape,
        compiler_params=_cparams(("parallel", "parallel", "parallel")),
        name="band_attn",
    )(*args)


SAMPLE_SEQ_UNROLL = 4


def _sample_attn_kernel(*refs, n_heads, n_kv, sb, t_new, with_sink, aliased):
    refs = list(refs)
    q_ref, newt_ref, cache_ref, bmc_ref, bmn_ref = refs[:5]
    refs = refs[5:]
    sink_ref = refs.pop(0) if with_sink else None
    if aliased:
        refs.pop(0)
    o_ref = refs.pop(0)
    lse_ref = None if with_sink else refs.pop(0)
    cout_ref = refs.pop(0)
    kw = n_kv * HEAD_DIM
    grp = n_heads // n_kv
    w, n_buf = cache_ref.shape[2:]
    first_new = BLOCK - t_new
    per_blk = BLOCK // t_new
    i = pl.program_id(0)
    lane = lax.broadcasted_iota(jnp.int32, (w, BLOCK), 1)

    def one_seq(s, carry):
        r0 = s * t_new if isinstance(s, int) else pl.multiple_of(s * t_new, t_new)
        qs = q_ref[pl.ds(r0, t_new), :]
        q_rows = []
        for h in range(n_heads):
            piece = qs[:, h * HEAD_DIM:(h + 1) * HEAD_DIM]
            parts = [piece if k == h // grp else jnp.zeros_like(piece) for k in range(n_kv)]
            q_rows.append(jnp.concatenate(parts, axis=1))
        qbd = jnp.concatenate(q_rows, axis=0).astype(BF16)
        cache = cache_ref[0, s]
        off = ((i * sb + s) % per_blk) * t_new
        placed = pltpu.roll(newt_ref[...], first_new - off, axis=1)
        s_c = _dot(qbd, cache[:kw].astype(BF16)) + bmc_ref[...]
        s_n = _dot(qbd, placed[:kw].astype(BF16)) + bmn_ref[...]
        m = jnp.maximum(jnp.max(s_c, axis=-1, keepdims=True), jnp.max(s_n, axis=-1, keepdims=True))
        if with_sink:
            m = jnp.maximum(m, sink_ref[:, :1])
        p_c = jnp.exp(s_c - m)
        p_n = jnp.exp(s_n - m)
        den = jnp.sum(p_c, axis=-1, keepdims=True) + jnp.sum(p_n, axis=-1, keepdims=True)
        if with_sink:
            den = den + jnp.exp(sink_ref[:, :1] - m)
        o_all = (_dot_nt(p_c.astype(BF16), cache[kw:].astype(BF16))
                 + _dot_nt(p_n.astype(BF16), placed[kw:].astype(BF16))) / den
        outs = []
        for h in range(n_heads):
            kh = h // grp
            outs.append(o_all[h * t_new:(h + 1) * t_new, kh * HEAD_DIM:(kh + 1) * HEAD_DIM])
        o_ref[pl.ds(r0, t_new), :] = jnp.concatenate(outs, axis=1)
        if lse_ref is not None:
            lse = m + jnp.log(den)
            lse_ref[pl.ds(r0, t_new), :] = jnp.concatenate(
                [jnp.broadcast_to(lse[h * t_new:(h + 1) * t_new], (t_new, HEAD_DIM)) for h in range(n_heads)], axis=1)
        rolled = pltpu.roll(cache, n_buf - t_new, axis=1)
        if n_buf > BLOCK:
            cout_ref[0, s, :, : n_buf - BLOCK] = rolled[:, : n_buf - BLOCK]
        cout_ref[0, s, :, n_buf - BLOCK:] = jnp.where(lane >= first_new, placed, rolled[:, n_buf - BLOCK:])
        return carry

    group = min(sb, SAMPLE_SEQ_UNROLL)
    if sb == group:
        for s in range(sb):
            one_seq(s, 0)
    else:
        def body(it, carry):
            for u in range(group):
                one_seq(it * group + u, carry)
            return carry

        lax.fori_loop(0, sb // group, body, 0)


def _sample_attn(q, newt, cache_t, prev_out, bmc, bmn, layer, n_heads, n_kv, t_new, sink_rows=None):
    depth, nseq, w, n_buf = cache_t.shape
    qw = n_heads * HEAD_DIM
    sb = max(1, min(BLOCK // t_new, 2048 // n_buf))
    with_sink = sink_rows is not None
    aliased = prev_out is not None
    rows = sb * t_new
    per_blk = BLOCK // t_new
    in_specs = [
        pl.BlockSpec((rows, qw), lambda i: (i, 0)),
        pl.BlockSpec((w, BLOCK), lambda i: (0, (i * sb) // per_blk)),
        pl.BlockSpec((1, sb, w, n_buf), lambda i: (layer, i, 0, 0)),
        pl.BlockSpec(bmc.shape, lambda i: (0, 0)),
        pl.BlockSpec(bmn.shape, lambda i: (0, 0)),
    ]
    args = [q, newt, cache_t, bmc, bmn]
    if with_sink:
        in_specs.append(pl.BlockSpec(sink_rows.shape, lambda i: (0, 0)))
        args.append(sink_rows)
    aliases = {}
    if aliased:
        aliases = {len(args): 1 if with_sink else 2}
        in_specs.append(pl.BlockSpec(memory_space=pl.ANY))
        args.append(prev_out)
    o_spec = pl.BlockSpec((rows, qw), lambda i: (i, 0))
    o_shape = jax.ShapeDtypeStruct((nseq * t_new, qw), F32)
    c_spec = pl.BlockSpec((1, sb, w, n_buf), lambda i: (layer, i, 0, 0))
    c_shape = jax.ShapeDtypeStruct(cache_t.shape, F32)
    if with_sink:
        out_specs, out_shape = [o_spec, c_spec], [o_shape, c_shape]
    else:
        out_specs, out_shape = [o_spec, o_spec, c_spec], [o_shape, o_shape, c_shape]
    return pl.pallas_call(
        functools.partial(_sample_attn_kernel, n_heads=n_heads, n_kv=n_kv, sb=sb, t_new=t_new,
                          with_sink=with_sink, aliased=aliased),
        grid=(nseq // sb,),
        in_specs=in_specs,
        out_specs=out_specs,
        out_shape=out_shape,
        input_output_aliases=aliases,
        compiler_params=_cparams(("parallel",)),
        name="sample_attn",
    )(*args)


def _post_kernel(x_ref, gt_ref, shf_ref, scf_ref, g2_ref, g3_ref, u_ref, halo_ref,
                 o1_ref, l1_ref, o2_ref, l2_ref, o3_ref, l3_ref, oc_ref, gl_ref,
                 pw_ref, ps_ref, wb_ref, wc_ref, wo_ref, wrt_ref, rb_ref, tie_ref,
                 xmid_ref, h2_ref, idx_ref, rank_ref, wk_ref, counts_ref, wfull_ref, cnt_ref, *unfold_refs,
                 tps, full_windows, dils):
    s, t, d = x_ref.shape
    rows = s * t
    i = pl.program_id(0)

    u = u_ref[...]
    halo = halo_ref[...]
    if not full_windows:
        halo = jnp.where(i % tps == 0, 0.0, halo)
    ue = jnp.concatenate([halo, u], axis=1)
    if full_windows:
        row = None
    else:
        row = (i % tps) * t + lax.broadcasted_iota(jnp.int32, (1, t, 1), 1)
    parts = []
    for g, win in enumerate(POOL_WINDOWS):
        cs = slice(g * POOL_CH, (g + 1) * POOL_CH)
        acc = ue[:, :, cs]
        base = 0
        span = 1
        while span < win:
            acc = acc[:, span:, :] + acc[:, : acc.shape[1] - span, :]
            base += span
            span *= 2
        tot = acc[:, POOL_HALO - base:, :]
        if full_windows:
            mean = tot / float(win)
        else:
            cnt = jnp.minimum(row + 1, win).astype(F32)
            mean = tot / cnt
        zg = (mean - u[:, :, cs]).reshape(rows, POOL_CH).astype(BF16)
        parts.append(_dot(zg, pw_ref[0, g]))
    a = jnp.concatenate(parts, axis=-1) * ps_ref[0]

    scratch = list(unfold_refs)

    def token_order(ref, dil):
        if dil == 1:
            return ref[...]
        scr = scratch.pop(0)
        n = rows // dil
        n_c = scr.shape[0]
        for r in range(dil):
            part = ref[0, r]
            for c in range(n_c):
                scr[c, pl.ds(r, n, stride=dil), :] = part[:, c * BLOCK:(c + 1) * BLOCK]
        return jnp.concatenate([scr[c] for c in range(n_c)], axis=1)

    o1, l1 = token_order(o1_ref, dils[0]), token_order(l1_ref, dils[0])
    o2, l2 = token_order(o2_ref, dils[1]), token_order(l2_ref, dils[1])
    o3, l3 = token_order(o3_ref, dils[2]), token_order(l3_ref, dils[2])

    lm = jnp.maximum(jnp.maximum(l1, l2), l3)
    e1, e2, e3 = jnp.exp(l1 - lm), jnp.exp(l2 - lm), jnp.exp(l3 - lm)
    esum = e1 + e2 + e3
    bmix = (e1 / esum) * o1 + (e2 / esum) * o2 + (e3 / esum) * o3
    b = _dot(bmix.astype(BF16), wb_ref[...])
    c = _dot(oc_ref[...].astype(BF16), wc_ref[...])
    g_a = _sigmoid(gl_ref[:, :d])
    g_b = _sigmoid(gl_ref[:, d:2 * d])
    g_c = _sigmoid(gl_ref[:, 2 * d:])
    mix = _dot((g_a * a + g_b * b + g_c * c).astype(BF16), wo_ref[...])

    x = x_ref[...]
    xm = x + gt_ref[...] * _rms(mix, g2_ref[0]).reshape(s, t, d)
    xmid_ref[...] = xm
    h2f = (_rms(xm, g3_ref[...]) * (1.0 + scf_ref[...]) + shf_ref[...]).reshape(rows, d)
    h2 = h2f.astype(BF16)
    _store_rows(h2_ref, _pack_chunks(h2f), rows)

    scores = _sigmoid(_dot_nt(wrt_ref[...], h2))
    work = scores + rb_ref[0]
    e_iota = lax.broadcasted_iota(jnp.int32, (N_EXPERTS, rows), 0).astype(F32)
    sel = jnp.zeros((N_EXPERTS, rows), F32)
    picks = []
    for _ in range(TOP_K):
        mx = jnp.max(work, axis=0, keepdims=True)
        pick = jnp.min(jnp.where(work == mx, e_iota, float(N_EXPERTS)), axis=0, keepdims=True)
        hit = e_iota == pick
        sel = jnp.where(hit, 1.0, sel)
        work = jnp.where(hit, NEG_INF, work)
        picks.append(pick)
    top_s = scores * sel
    wmat = top_s / jnp.sum(top_s, axis=0, keepdims=True) * ROUTED_SCALE
    wfull_ref[...] = wmat

    @pl.when(i == 0)
    def _():
        cnt_ref[...] = jnp.zeros_like(cnt_ref)

    r_i = lax.broadcasted_iota(jnp.int32, (rows, rows), 0)
    c_i = lax.broadcasted_iota(jnp.int32, (rows, rows), 1)
    before = jnp.where(r_i < c_i, 1.0, 0.0).astype(BF16)
    rank_all = cnt_ref[...] + _dot(sel.astype(BF16), before)
    cnt_ref[...] = cnt_ref[...] + jnp.sum(sel, axis=1, keepdims=True)
    counts_ref[...] = cnt_ref[...].astype(jnp.int32)

    krow = lax.broadcasted_iota(jnp.int32, (TOP_K, rows), 0)
    idx = jnp.zeros((TOP_K, rows), F32)
    rank = jnp.zeros((TOP_K, rows), F32)
    wk = jnp.zeros((TOP_K, rows), F32)
    for k, pick in enumerate(picks):
        hit = e_iota == pick
        idx = jnp.where(krow == k, pick, idx)
        rank = jnp.where(krow == k, jnp.sum(jnp.where(hit, rank_all, 0.0), axis=0, keepdims=True), rank)
        wk = jnp.where(krow == k, jnp.sum(jnp.where(hit, wmat, 0.0), axis=0, keepdims=True), wk)
    idx_ref[...] = idx.astype(jnp.int32)
    rank_ref[...] = rank.astype(jnp.int32)
    eye = jnp.where(r_i == c_i, 1.0, 0.0).astype(BF16)
    hi = wk.astype(BF16)
    rest = wk - hi.astype(F32)
    mid = rest.astype(BF16)
    lo = (rest - mid.astype(F32)).astype(BF16)
    wk_ref[...] = _dot_nt(eye, hi) + _dot_nt(eye, mid) + _dot_nt(eye, lo)


def _post_mixer(x3, ada3, u3, halo3, attn, gl, weights, layer, tm, full_windows, dils, tie):
    nseq, tseq, d = x3.shape
    ntok = nseq * tseq
    s_blk, t_blk, tps = _tile_split(tseq, tm)
    n_tiles = ntok // tm
    (g2, g3, pool_w, pool_scale, w_b_up, w_c_up, w_out, w_router, router_bias) = weights

    def x_map(i):
        return (i // tps, i % tps, 0)

    def tok_map(i):
        return (i, 0)

    def ada_spec(j):
        return pl.BlockSpec((s_blk, 1, d), lambda i: (i // tps, 0, j))

    if full_windows:
        halo_spec = pl.BlockSpec((s_blk, POOL_HALO, POOL_WIDTH), lambda i: (i, 0, 0))
    else:
        hb = t_blk // POOL_HALO
        halo_spec = pl.BlockSpec((1, POOL_HALO, POOL_WIDTH),
                                 lambda i: (i // tps, jnp.maximum((i % tps) * hb - 1, 0), 0))
    vec = pl.BlockSpec((1, 1, d), lambda i: (layer, 0, 0))
    attn_specs = []
    for g, dil in enumerate(dils):
        if dil == 1:
            spec = pl.BlockSpec((tm, DIL_W), tok_map)
        else:
            spec = pl.BlockSpec((1, dil, tm // dil, DIL_W), lambda i: (i // tps, 0, i % tps, 0))
        attn_specs += [spec, spec]
    in_specs = [
        pl.BlockSpec((s_blk, t_blk, d), x_map), ada_spec(2), ada_spec(3), ada_spec(4), vec, vec,
        pl.BlockSpec((s_blk, t_blk, POOL_WIDTH), x_map), halo_spec,
    ] + attn_specs + [
        pl.BlockSpec((tm, SWA_QW), tok_map), pl.BlockSpec((tm, 3 * d), tok_map),
        pl.BlockSpec((1,) + pool_w.shape[1:], lambda i: (layer, 0, 0, 0)),
        vec,
        pl.BlockSpec(w_b_up.shape, lambda i: (0, 0)), pl.BlockSpec(w_c_up.shape, lambda i: (0, 0)),
        pl.BlockSpec(w_out.shape, lambda i: (0, 0)), pl.BlockSpec(w_router.shape, lambda i: (0, 0)),
        pl.BlockSpec((1, N_EXPERTS, 1), lambda i: (layer, 0, 0)),
        _TIE_SPEC,
    ]
    out_shape = [
        jax.ShapeDtypeStruct((nseq, tseq, d), F32),
        jax.ShapeDtypeStruct((ntok * ROW_SUB, ROW_LANES), ROW_DTYPE),
        jax.ShapeDtypeStruct((TOP_K, ntok), jnp.int32),
        jax.ShapeDtypeStruct((TOP_K, ntok), jnp.int32),
        jax.ShapeDtypeStruct((ntok, TOP_K), F32),
        jax.ShapeDtypeStruct((N_EXPERTS, 1), jnp.int32),
        jax.ShapeDtypeStruct((N_EXPERTS, ntok), F32),
    ]
    out_specs = [
        pl.BlockSpec((s_blk, t_blk, d), x_map),
        pl.BlockSpec((tm * ROW_SUB, ROW_LANES), tok_map),
        pl.BlockSpec((TOP_K, tm), lambda i: (0, i)),
        pl.BlockSpec((TOP_K, tm), lambda i: (0, i)),
        pl.BlockSpec((tm, TOP_K), tok_map),
        pl.BlockSpec((N_EXPERTS, 1), lambda i: (0, 0)),
        pl.BlockSpec((N_EXPERTS, tm), lambda i: (0, i)),
    ]
    n_unfold = 2 * sum(1 for dil in dils if dil > 1)
    scratch = [pltpu.VMEM((N_EXPERTS, 1), F32)] + [pltpu.VMEM((DIL_W // BLOCK, tm, BLOCK), F32)] * n_unfold
    return pl.pallas_call(
        functools.partial(_post_kernel, tps=tps, full_windows=full_windows, dils=tuple(dils)),
        grid=(n_tiles,),
        in_specs=in_specs,
        out_specs=out_specs,
        out_shape=out_shape,
        scratch_shapes=scratch,
        compiler_params=_cparams(("arbitrary",)),
        name="post_mixer",
    )(x3, ada3, ada3, ada3, g2, g3, u3, halo3, *attn, gl,
      pool_w, pool_scale, w_b_up, w_c_up, w_out, w_router, router_bias, tie)


EXPERT_ROWS = 256


def _expert_kernel(bexp_ref, nvalid_ref, nused_ref, x_ref, wg_ref, wu_ref, wd_ref, tie_ref, y_ref, wg_s, wu_s, wd_s, h_s):
    i = pl.program_id(0)
    bm = x_ref.shape[0] // ROW_SUB
    sub = min(bm, EXPERT_ROWS)
    last = bexp_ref.shape[0] - 1
    up_blk = jnp.minimum(i, last)
    dn_blk = jnp.maximum(i - 1, 0)

    @pl.when((i == 0) | (bexp_ref[up_blk] != bexp_ref[jnp.maximum(up_blk - 1, 0)]))
    def _():
        wg_s[...] = wg_ref[0, 0].astype(BF16)
        wu_s[...] = wu_ref[0, 0].astype(BF16)

    @pl.when((i <= 1) | (bexp_ref[dn_blk] != bexp_ref[jnp.maximum(dn_blk - 1, 0)]))
    def _():
        wd_s[...] = wd_ref[0, 0].astype(BF16)

    @pl.when(i == 0)
    def _():
        h_s[...] = jnp.zeros_like(h_s)

    @pl.when(i <= nused_ref[0])
    def _():
        for c in range(bm // sub):
            y = _dot(h_s[c * sub:(c + 1) * sub, :], wd_s[...])
            _store_rows(y_ref, _pack_chunks(y), sub, first=c * sub)
        for c in range(bm // sub):
            x = _unpack_chunks(_load_rows(x_ref, sub, first=c * sub))
            row = c * sub + lax.broadcasted_iota(jnp.int32, (sub, 1), 0)
            x = jnp.where(row < nvalid_ref[up_blk], x, 0.0).astype(BF16)
            gate = _dot(x, wg_s[...])
            up = _dot(x, wu_s[...])
            h_s[c * sub:(c + 1) * sub, :] = (_silu(gate) * up).astype(BF16)

    @pl.when(i > nused_ref[0])
    def _():
        y_ref[...] = jnp.zeros_like(y_ref)


def _experts(x_rows, block_exp, nvalid, n_used, we_gate, we_up, we_down, layer, bm, tie):
    n_rows = x_rows.shape[0] // ROW_SUB
    d, ff = we_gate.shape[-2:]
    n_blocks = n_rows // bm
    last = n_blocks - 1

    def up_map(i, be, nv, nu):
        return (layer, be[jnp.minimum(i, last)], 0, 0)

    def down_map(i, be, nv, nu):
        return (layer, be[jnp.maximum(i - 1, 0)], 0, 0)

    grid_spec = pltpu.PrefetchScalarGridSpec(
        num_scalar_prefetch=3,
        grid=(n_blocks + 1,),
        in_specs=[
            pl.BlockSpec((bm * ROW_SUB, ROW_LANES), lambda i, be, nv, nu: (jnp.minimum(i, last), 0)),
            pl.BlockSpec((1, 1, d, ff), up_map),
            pl.BlockSpec((1, 1, d, ff), up_map),
            pl.BlockSpec((1, 1, ff, d), down_map),
            _TIE_SPEC,
        ],
        out_specs=pl.BlockSpec((bm * ROW_SUB, ROW_LANES), lambda i, be, nv, nu: (jnp.maximum(i - 1, 0), 0)),
        scratch_shapes=[pltpu.VMEM((d, ff), BF16), pltpu.VMEM((d, ff), BF16), pltpu.VMEM((ff, d), BF16),
                        pltpu.VMEM((bm, ff), BF16)],
    )
    return pl.pallas_call(
        _expert_kernel,
        grid_spec=grid_spec,
        out_shape=jax.ShapeDtypeStruct(x_rows.shape, x_rows.dtype),
        compiler_params=_cparams(("arbitrary",)),
        name="experts",
    )(block_exp, nvalid, n_used, x_rows, we_gate, we_up, we_down, tie)


def _final_kernel(x_ref, gt_ref, g4_ref, h2_ref, tie_ref, *refs, dense):
    s, t, d = x_ref.shape
    rows = s * t
    if dense:
        yr_ref, wg_ref, wu_ref, wd_ref, o_ref = refs
    else:
        yk_ref, wk_ref, wg_ref, wu_ref, wd_ref, o_ref = refs
    h2 = _unpack_chunks(_load_rows(h2_ref, rows)).astype(BF16)
    f = _dot((_silu(_dot(h2, wg_ref[...])) * _dot(h2, wu_ref[...])).astype(BF16), wd_ref[...])
    if dense:
        f = f + yr_ref[...]
    else:
        wk = wk_ref[...]
        for k in range(TOP_K):
            f = f + wk[:, k:k + 1] * _unpack_chunks(_load_rows(yk_ref, rows, lead=k))
    o_ref[...] = x_ref[...] + gt_ref[...] * _rms(f, g4_ref[0]).reshape(s, t, d)


def _final(xmid3, ada3, g4, h2_rows, routed, ws_gate, ws_up, ws_down, layer, tm, tie):
    nseq, tseq, d = xmid3.shape
    ntok = nseq * tseq
    s_blk, t_blk, tps = _tile_split(tseq, tm)
    dense = not isinstance(routed, tuple)

    def x_map(i):
        return (i // tps, i % tps, 0)

    if dense:
        routed_specs = [pl.BlockSpec((tm, d), lambda i: (i, 0))]
        routed = (routed,)
    else:
        routed_specs = [pl.BlockSpec((TOP_K, tm * ROW_SUB, ROW_LANES), lambda i: (0, i, 0)),
                        pl.BlockSpec((tm, TOP_K), lambda i: (i, 0))]
    return pl.pallas_call(
        functools.partial(_final_kernel, dense=dense),
        grid=(ntok // tm,),
        in_specs=[
            pl.BlockSpec((s_blk, t_blk, d), x_map),
            pl.BlockSpec((s_blk, 1, d), lambda i: (i // tps, 0, 5)),
            pl.BlockSpec((1, 1, d), lambda i: (layer, 0, 0)),
            pl.BlockSpec((tm * ROW_SUB, ROW_LANES), lambda i: (i, 0)),
            _TIE_SPEC,
        ] + routed_specs + [
            pl.BlockSpec(ws_gate.shape, lambda i: (0, 0)),
            pl.BlockSpec(ws_up.shape, lambda i: (0, 0)),
            pl.BlockSpec(ws_down.shape, lambda i: (0, 0)),
        ],
        out_specs=pl.BlockSpec((s_blk, t_blk, d), x_map),
        out_shape=jax.ShapeDtypeStruct((nseq, tseq, d), F32),
        compiler_params=_cparams(("parallel",)),
        name="final",
    )(xmid3, ada3, g4, h2_rows, tie, *routed, ws_gate, ws_up, ws_down)


def _dense_moe_kernel(h2_ref, wm_ref, wg_ref, wu_ref, wd_ref, y_ref, x_s, eye_s):
    e = pl.program_id(0)
    rows = y_ref.shape[0]

    @pl.when(e == 0)
    def _():
        x_s[...] = _unpack_chunks(_load_rows(h2_ref, rows)).astype(BF16)
        r_i = lax.broadcasted_iota(jnp.int32, (rows, rows), 0)
        c_i = lax.broadcasted_iota(jnp.int32, (rows, rows), 1)
        eye_s[...] = jnp.where(r_i == c_i, 1.0, 0.0).astype(BF16)
        y_ref[...] = jnp.zeros_like(y_ref)

    x = x_s[...]
    gate = _dot(x, wg_ref[0, 0].astype(BF16))
    up = _dot(x, wu_ref[0, 0].astype(BF16))
    y = _dot((_silu(gate) * up).astype(BF16), wd_ref[0, 0].astype(BF16))
    w_row = wm_ref[0]
    hi = w_row.astype(BF16).astype(F32)
    mid = (w_row - hi).astype(BF16).astype(F32)
    lo = w_row - hi - mid
    piece = lax.broadcasted_iota(jnp.int32, (8, rows), 0)
    parts = jnp.where(piece == 0, hi, jnp.where(piece == 1, mid, jnp.where(piece == 2, lo, 0.0))).astype(BF16)
    w_col = jnp.sum(_dot_nt(eye_s[...], parts), axis=1, keepdims=True)
    y_ref[...] += w_col * y


def _dense_moe(h2_rows, wfull, we_gate, we_up, we_down, layer):
    n_exp, ntok = wfull.shape
    d, ff = we_gate.shape[-2:]
    return pl.pallas_call(
        _dense_moe_kernel,
        grid=(n_exp,),
        in_specs=[
            pl.BlockSpec(h2_rows.shape, lambda e: (0, 0)),
            pl.BlockSpec((1, 1, ntok), lambda e: (e, 0, 0)),
            pl.BlockSpec((1, 1, d, ff), lambda e: (layer, e, 0, 0)),
            pl.BlockSpec((1, 1, d, ff), lambda e: (layer, e, 0, 0)),
            pl.BlockSpec((1, 1, ff, d), lambda e: (layer, e, 0, 0)),
        ],
        out_specs=pl.BlockSpec((ntok, d), lambda e: (0, 0)),
        out_shape=jax.ShapeDtypeStruct((ntok, d), F32),
        scratch_shapes=[pltpu.VMEM((ntok, d), BF16), pltpu.VMEM((ntok, ntok), BF16)],
        compiler_params=_cparams(("arbitrary",)),
        name="dense_moe",
    )(h2_rows, wfull.reshape(n_exp, 1, ntok), we_gate, we_up, we_down)


def _t5_bucket(dist):
    n = np.asarray(dist, dtype=np.int64)
    exact = N_BUCKETS // 2
    log_ratio = np.log(np.maximum(n, 1) / exact) / np.log(MAX_DISTANCE / exact)
    large = np.minimum(exact + (log_ratio * (N_BUCKETS - exact)).astype(np.int64), N_BUCKETS - 1)
    return np.where(n < exact, n, large).astype(np.int32)


def _bias_table(rel_bias, h0, h1, steps, dil):
    steps = np.asarray(steps)
    buckets = np.where(steps >= 0, _t5_bucket(np.maximum(steps, 0) * dil), -1)
    onehot = buckets[..., None] == np.arange(N_BUCKETS)
    table = rel_bias[:, h0:h1].T.astype(F32).reshape((h1 - h0,) + (1,) * steps.ndim + (N_BUCKETS,))
    val = jnp.sum(jnp.where(onehot[None], table, 0.0), axis=-1)
    return jnp.where((steps >= 0)[None], val, NEG_INF)


def _band_steps():
    dist = BLOCK + np.arange(BLOCK)[:, None] - np.arange(2 * BLOCK)[None, :]
    return np.where((dist >= 0) & (dist <= BLOCK), dist, -1)


def _sample_steps(n_buf, dil, t_new):
    t = np.arange(t_new)[:, None]
    delta_c = n_buf + t - np.arange(n_buf)[None, :]
    ok_c = (delta_c % dil == 0) & (delta_c // dil <= BLOCK)
    lane = np.arange(BLOCK)[None, :]
    delta_n = t - (lane - (BLOCK - t_new))
    ok_n = (lane >= BLOCK - t_new) & (delta_n >= 0) & (delta_n % dil == 0) & (delta_n // dil <= BLOCK)
    return np.where(ok_c, delta_c // dil, -1), np.where(ok_n, delta_n // dil, -1)


def _route_plan(idx, rank, counts, bm):
    n_tok = idx.shape[1]
    n_blocks = -(-(n_tok * TOP_K + N_EXPERTS * (bm - 1)) // bm)
    counts = counts.reshape(N_EXPERTS)
    padded = (counts + bm - 1) // bm * bm
    pad_end = jnp.cumsum(padded)
    pad_start = pad_end - padded
    experts = jnp.arange(N_EXPERTS, dtype=jnp.int32)
    onehot = idx[:, :, None] == experts[None, None, :]
    dest = rank + jnp.sum(jnp.where(onehot, pad_start[None, None, :], 0), axis=-1)
    starts = jnp.arange(n_blocks, dtype=jnp.int32) * bm
    block_exp = jnp.minimum(jnp.sum((starts[:, None] >= pad_end[None, :]).astype(jnp.int32), axis=1), N_EXPERTS - 1)
    hot = block_exp[:, None] == experts[None, :]
    blk_cnt = jnp.sum(jnp.where(hot, counts[None, :], 0), axis=1)
    blk_start = jnp.sum(jnp.where(hot, pad_start[None, :], 0), axis=1)
    nvalid = jnp.clip(blk_cnt - (starts - blk_start), 0, bm).astype(jnp.int32)
    n_used = (pad_end[-1] // bm).astype(jnp.int32).reshape(1)
    return dest.astype(jnp.int32), block_exp.astype(jnp.int32), nvalid, n_used, n_blocks * bm


def _moe_routed(h2_rows, idx, rank, counts, we_gate, we_up, we_down, layer, bm, tie):
    n_tok = idx.shape[1]
    dest, block_exp, nvalid, n_used, n_rows = _route_plan(idx, rank, counts, bm)
    x_sorted = _sc_scatter_rows(h2_rows.reshape(n_tok, ROW_SUB, ROW_LANES), dest, n_rows)
    y_sorted = _experts(x_sorted.reshape(n_rows * ROW_SUB, ROW_LANES), block_exp, nvalid, n_used,
                        we_gate, we_up, we_down, layer, bm, tie)
    yk = _sc_gather_rows(y_sorted.reshape(n_rows, ROW_SUB, ROW_LANES), dest)
    return yk.reshape(TOP_K, n_tok * ROW_SUB, ROW_LANES), y_sorted


def _to_time_minor(c):
    depth, nseq, n_buf = c.shape[:3]
    return jnp.transpose(c, (0, 1, 3, 4, 5, 2)).reshape(depth, nseq, -1, n_buf)


def _from_time_minor(ct, n_heads):
    lead = ct.shape[:-2]
    rows = ct.shape[-1]
    nl = len(lead)
    x = ct.reshape(lead + (2, n_heads, HEAD_DIM, rows))
    return jnp.transpose(x, tuple(range(nl)) + (nl + 3, nl, nl + 1, nl + 2))


def kernel(x_prompt, x_sample, cache_b1, cache_b2, cache_b3, cache_c, state_pool, c_prompt, c_sample, rel_bias, w_ada, b_ada, g_pre_mix, g_post_mix, g_pre_ffn, g_post_ffn, w_in, pool_w, pool_scale, w_b_up, w_c_up, sinks, w_out, w_router, router_bias, we_gate, we_up, we_down, ws_gate, ws_up, ws_down):
    batch, seq, d = x_prompt.shape
    dec_batch, dec_seq, _ = x_sample.shape
    depth = w_in.shape[0]
    tm_p = 256
    tm_s = 128
    n_dil = len(DIL_GROUPS)
    dils = tuple(dil for _, dil in DIL_GROUPS)

    n_seq_all = batch + dec_batch
    pad = -n_seq_all % 8
    c_all = jnp.concatenate([c_prompt, c_sample, jnp.zeros((pad, d), F32)], axis=0)
    ada = _ada_all(c_all, w_ada, b_ada)

    head0 = [g * DIL_HEADS for g in range(n_dil)] + [n_dil * DIL_HEADS]
    head1 = [(g + 1) * DIL_HEADS for g in range(n_dil)] + [n_dil * DIL_HEADS + SWA_HEADS]
    band = []
    for g, dil in enumerate(dils + (1,)):
        tab = _bias_table(rel_bias, head0[g], head1[g], _band_steps(), dil)
        n_h = head1[g] - head0[g]
        band.append(tab.reshape(n_h // 2, 2, BLOCK, 2 * BLOCK).transpose(0, 2, 1, 3).reshape(n_h // 2, BLOCK, 4 * BLOCK))
    caches = (cache_b1, cache_b2, cache_b3, cache_c)
    samp = []
    for g, dil in enumerate(dils + (1,)):
        n_buf = caches[g].shape[2]
        sc_steps, sn_steps = _sample_steps(n_buf, dil, dec_seq)
        rows = (head1[g] - head0[g]) * dec_seq
        samp.append((_bias_table(rel_bias, head0[g], head1[g], sc_steps, dil).reshape(rows, n_buf),
                     _bias_table(rel_bias, head0[g], head1[g], sn_steps, dil).reshape(rows, BLOCK)))

    splits = np.cumsum([0, POOL_WIDTH, 3 * DIL_W, 3 * DIL_W, 3 * DIL_W, SWA_QW, SWA_KW, SWA_KW, 3 * d])
    col_slices = [(int(splits[0]), int(splits[1]))]
    for g in range(n_dil):
        for part in range(3):
            c0 = int(splits[1 + part]) + g * DIL_W
            col_slices.append((c0, c0 + DIL_W))
    col_slices.append((int(splits[4]), int(splits[8])))

    vec3 = lambda a: a.reshape(depth, 1, -1)
    g1, g2, g3, g4 = vec3(g_pre_mix), vec3(g_post_mix), vec3(g_pre_ffn), vec3(g_post_ffn)
    ps3, rb3 = vec3(pool_scale), router_bias.reshape(depth, N_EXPERTS, 1)
    cache_t = [_to_time_minor(c) for c in caches]
    pool_halo = jnp.pad(state_pool, ((0, 0), (0, 0), (POOL_HALO - state_pool.shape[2], 0), (0, 0)))
    pw = pool_w.astype(BF16)

    xp, xs = x_prompt, x_sample
    no_tie = jnp.zeros((8, BLOCK), F32)
    cache_out = [None] * (n_dil + 1)
    st_p = [[] for _ in range(n_dil + 2)]
    pool_s = []
    keeps_p = tuple(min(win, seq) for win, _ in DIL_GROUPS) + (min(BLOCK, seq),)
    n_heads = (DIL_HEADS,) * n_dil + (SWA_HEADS,)
    n_kvs = (DIL_HEADS,) * n_dil + (SWA_KV_HEADS,)
    for l in range(depth):
        w_perm = jnp.concatenate([w_in[l][:, a:b] for a, b in col_slices], axis=1).astype(BF16)
        wb, wc, wo, wr = w_b_up[l].astype(BF16), w_c_up[l].astype(BF16), w_out[l].astype(BF16), w_router[l].T.astype(BF16)
        wsg, wsu, wsd = ws_gate[l].astype(BF16), ws_up[l].astype(BF16), ws_down[l].astype(BF16)
        post_w = (g2, g3, pw, ps3, wb, wc, wo, wr, rb3)
        ada_p = ada[l, :batch].reshape(batch, 1, -1)
        ada_s = ada[l, batch:n_seq_all].reshape(dec_batch, 1, -1)
        sink_rows = jnp.broadcast_to(jnp.repeat(sinks[l].astype(F32), dec_seq)[:, None], (SWA_HEADS * dec_seq, BLOCK))

        outs = _in_proj(xp, ada_p, g1, w_perm, l, tm_p, keeps_p, True, no_tie)
        u, gl, states = outs[0], outs[3 + 2 * n_dil], outs[4 + 2 * n_dil:]
        attn = []
        for g in range(n_dil):
            o, lse = _band_attn(outs[1 + 2 * g], outs[2 + 2 * g], band[g], DIL_HEADS, DIL_HEADS)
            if dils[g] == 1:
                o, lse = o.reshape(batch * seq, DIL_W), lse.reshape(batch * seq, DIL_W)
            attn += [o, lse]
        oc = _band_attn(outs[1 + 2 * n_dil], outs[2 + 2 * n_dil], band[n_dil], SWA_HEADS, SWA_KV_HEADS, sinks=sinks[l])
        attn.append(oc.reshape(batch * seq, SWA_QW))
        xmid_p, h2_p, idx, rank, wk, counts, _ = _post_mixer(xp, ada_p, u, u, attn, gl, post_w, l, tm_p, False, dils,
                                                             no_tie)
        for k in range(n_dil + 1):
            st_p[k].append(_from_time_minor(states[k], n_kvs[k]))
        st_p[n_dil + 1].append(u[:, seq - (POOL_HALO - 1):])

        outs = _in_proj(xs, ada_s, g1, w_perm, l, tm_s, (dec_seq,) * (n_dil + 1), False, h2_p)
        u, gl, states = outs[0], outs[2 + n_dil], outs[3 + n_dil:]
        attn = []
        for g in range(n_dil + 1):
            res = _sample_attn(outs[1 + g], states[g], cache_t[g], cache_out[g], samp[g][0], samp[g][1], l,
                               n_heads[g], n_kvs[g], dec_seq, sink_rows=sink_rows if g == n_dil else None)
            attn += list(res[:-1])
            cache_out[g] = res[-1]
        attn_done = functools.reduce(jnp.add, [a[:8, :BLOCK] for a in attn[0::2]])
        y_k, y_sorted = _moe_routed(h2_p, idx, rank, counts, we_gate, we_up, we_down, l, 512, attn_done)
        xmid, h2, _, _, _, _, wfull = _post_mixer(xs, ada_s, u, pool_halo[l], attn, gl, post_w, l, tm_s, True,
                                                  (1,) * n_dil, y_sorted)
        y_r = _dense_moe(h2, wfull, we_gate, we_up, we_down, l)
        xs = _final(xmid, ada_s, g4, h2, y_r, wsg, wsu, wsd, l, tm_s, no_tie)
        xp = _final(xmid_p, ada_p, g4, h2_p, (y_k, wk), wsg, wsu, wsd, l, tm_p, xs)
        pool_s.append(jnp.concatenate([state_pool[l], u], axis=1)[:, -(POOL_HALO - 1):])

    b1_p, b2_p, b3_p, c_p, pool_p = [jnp.stack(s, axis=0) for s in st_p]
    outs_s = [_from_time_minor(co, n_kvs[g]) for g, co in enumerate(cache_out)]
    return (xp, xs, b1_p, b2_p, b3_p, c_p, pool_p, outs_s[0], outs_s[1], outs_s[2], outs_s[3], jnp.stack(pool_s, axis=0))
```

```python
import functools

import numpy as np
import jax
import jax.numpy as jnp
from jax import lax
from jax.experimental import pallas as pl
from jax.experimental.pallas import tpu as pltpu
from jax.experimental.pallas import tpu_sc as plsc

F32 = jnp.float32
BF16 = jnp.bfloat16

HEAD_DIM = 64
SCALE = HEAD_DIM ** -0.5
BLOCK = 128
POOL_WINDOWS = (2, 4, 8, 16)
POOL_CH = 128
POOL_WIDTH = len(POOL_WINDOWS) * POOL_CH
POOL_HALO = 16
DIL_GROUPS = ((128, 1), (512, 4), (2048, 16))
DIL_HEADS = 4
DIL_W = DIL_HEADS * HEAD_DIM
SWA_HEADS = 8
SWA_KV_HEADS = 2
SWA_QW = SWA_HEADS * HEAD_DIM
SWA_KW = SWA_KV_HEADS * HEAD_DIM
N_BUCKETS = 32
MAX_DISTANCE = 2048
N_EXPERTS = 64
TOP_K = 8
ROUTED_SCALE = 2.5
EPS = 1e-6
NEG_INF = float("-inf")

V7X_VMEM_BYTES = 64 * 1024 * 1024
VMEM_LIMIT = 56 * 1024 * 1024


def _cparams(sem):
    return pltpu.CompilerParams(dimension_semantics=sem, vmem_limit_bytes=VMEM_LIMIT)


def _rms(x, g):
    ms = jnp.mean(x * x, axis=-1, keepdims=True)
    return x * lax.rsqrt(ms + EPS) * g


def _sigmoid(x):
    return 1.0 / (1.0 + jnp.exp(-x))


def _silu(x):
    return x * _sigmoid(x)


def _dot(a, b):
    return jnp.dot(a, b, preferred_element_type=F32)


def _dot_nt(a, b):
    return lax.dot_general(a, b, (((1,), (1,)), ((), ())), preferred_element_type=F32)


def _tile_split(tseq, tm):
    if tseq >= tm:
        return 1, tm, tseq // tm
    return tm // tseq, tseq, 1


_TIE_SPEC = pl.BlockSpec(memory_space=pl.ANY)


ROW_SUB = 4
ROW_LANES = 128
ROW_DTYPE = jnp.uint32
HI_MASK = 0xFFFF0000


def _pack_chunks(x):
    half = x.shape[1] // 2
    hi = pltpu.bitcast(x[:, :half].astype(BF16).astype(F32), jnp.uint32) & jnp.uint32(HI_MASK)
    lo = pltpu.bitcast(x[:, half:].astype(BF16).astype(F32), jnp.uint32) >> 16
    w = hi | lo
    return [w[:, j * ROW_LANES:(j + 1) * ROW_LANES] for j in range(ROW_SUB)]


def _unpack_chunks(chunks):
    his = [pltpu.bitcast(w & jnp.uint32(HI_MASK), F32) for w in chunks]
    los = [pltpu.bitcast(w << 16, F32) for w in chunks]
    return jnp.concatenate(his + los, axis=1)


def _store_rows(ref, chunks, rows, first=0):
    for j, c in enumerate(chunks):
        ref[pl.ds(first * ROW_SUB + j, rows, stride=ROW_SUB), :] = c


def _load_rows(ref, rows, first=0, lead=None):
    if lead is None:
        return [ref[pl.ds(first * ROW_SUB + j, rows, stride=ROW_SUB), :] for j in range(ROW_SUB)]
    return [ref[lead, pl.ds(first * ROW_SUB + j, rows, stride=ROW_SUB), :] for j in range(ROW_SUB)]


SC_CORES = 2
SC_SUBCORES = 16
SC_WORKERS = SC_CORES * SC_SUBCORES
SC_CHUNK = 64


def _sc_scatter_rows(src, dest, n_out):
    n, sub, lanes = src.shape
    kk = dest.shape[0]
    ch = min(SC_CHUNK, n // SC_WORKERS)
    n_chunks = n // (SC_WORKERS * ch)
    idx3 = dest.reshape(kk, n // ch, ch).transpose(1, 0, 2)
    mesh = plsc.VectorSubcoreMesh(core_axis_name="c", subcore_axis_name="s")

    @functools.partial(
        pl.kernel, mesh=mesh,
        out_type=jax.ShapeDtypeStruct((n_out, sub, lanes), src.dtype),
        scratch_types=[pltpu.VMEM((kk, ch), jnp.int32), pltpu.VMEM((ch, sub, lanes), src.dtype),
                       pltpu.SemaphoreType.DMA],
    )
    def scatter_kernel(src_hbm, idx_hbm, out_hbm, idx_v, rows_v, sem):
        wid = lax.axis_index("s") * SC_CORES + lax.axis_index("c")

        @pl.loop(0, n_chunks)
        def _(c):
            j = wid * n_chunks + c
            pltpu.sync_copy(idx_hbm.at[j], idx_v)
            pltpu.sync_copy(src_hbm.at[pl.ds(j * ch, ch)], rows_v)
            copies = [pltpu.async_copy(rows_v, out_hbm.at[idx_v.at[q]], sem) for q in range(kk)]
            for cp in copies:
                cp.wait()

    return scatter_kernel(src, idx3)


def _sc_gather_rows(table, dest):
    v, sub, lanes = table.shape
    kk, n = dest.shape
    total = n * kk
    ch = min(SC_CHUNK, total // SC_WORKERS)
    n_chunks = total // (SC_WORKERS * ch)
    idx2 = dest.reshape(total // ch, ch)
    mesh = plsc.VectorSubcoreMesh(core_axis_name="c", subcore_axis_name="s")

    assert n_chunks % 2 == 0, "the two-buffer gather pipeline walks chunks in pairs"

    @functools.partial(
        pl.kernel, mesh=mesh,
        out_type=jax.ShapeDtypeStruct((total, sub, lanes), table.dtype),
        scratch_types=[pltpu.VMEM((n_chunks, ch), jnp.int32), pltpu.VMEM((2, ch, sub, lanes), table.dtype),
                       pltpu.SemaphoreType.DMA((2,)), pltpu.SemaphoreType.DMA((2,))],
    )
    def gather_kernel(tab_hbm, idx_hbm, out_hbm, idx_v, rows_v, gsem, wsem):
        wid = lax.axis_index("s") * SC_CORES + lax.axis_index("c")
        base = wid * n_chunks
        pltpu.sync_copy(idx_hbm.at[pl.ds(base, n_chunks)], idx_v)

        def gather(c, b):
            return pltpu.make_async_copy(tab_hbm.at[idx_v.at[c]], rows_v.at[b], gsem.at[b])

        def write(c, b):
            return pltpu.make_async_copy(rows_v.at[b], out_hbm.at[pl.ds((base + c) * ch, ch)], wsem.at[b])

        gather(0, 0).start()

        @pl.loop(0, n_chunks, step=2)
        def _(c0):
            for b in range(2):
                c = c0 + b
                gather(c, b).wait()

                @pl.when(c + 1 < n_chunks)
                def _():
                    @pl.when(c >= 1)
                    def _():
                        write(c - 1, 1 - b).wait()

                    gather(c + 1, 1 - b).start()

                write(c, b).start()

        write(n_chunks - 2, 0).wait()
        write(n_chunks - 1, 1).wait()

    return gather_kernel(table, idx2).reshape(kk, n, sub, lanes)


def _ada_kernel(c_ref, w_ref, b_ref, o_ref):
    c = _silu(c_ref[...]).astype(BF16)
    o_ref[0] = _dot(c, w_ref[0].astype(BF16)) + b_ref[0]


def _ada_all(c_all, w_ada, b_ada):
    depth, d, n = w_ada.shape
    rows = c_all.shape[0]
    tn = 1536
    return pl.pallas_call(
        _ada_kernel,
        grid=(depth, n // tn),
        in_specs=[
            pl.BlockSpec((rows, d), lambda l, j: (0, 0)),
            pl.BlockSpec((1, d, tn), lambda l, j: (l, 0, j)),
            pl.BlockSpec((1, 1, tn), lambda l, j: (l, 0, j)),
        ],
        out_specs=pl.BlockSpec((1, rows, tn), lambda l, j: (l, 0, j)),
        out_shape=jax.ShapeDtypeStruct((depth, rows, n), F32),
        compiler_params=_cparams(("parallel", "parallel")),
        name="ada",
    )(c_all, w_ada, b_ada.reshape(depth, 1, n))


def _in_proj_kernel(x_ref, sh_ref, sc_ref, g_ref, w_ref, tie_ref, *refs, dils, fold):
    s, t, d = x_ref.shape
    rows = s * t
    h = _rms(x_ref[...], g_ref[...]) * (1.0 + sc_ref[...]) + sh_ref[...]
    hb = h.reshape(rows, d).astype(BF16)
    n_g = len(dils)
    if fold:
        u_ref = refs[0]
        q_refs = refs[1:1 + 2 * n_g:2] + (refs[1 + 2 * n_g],)
        kv_refs = refs[2:2 + 2 * n_g:2] + (refs[2 + 2 * n_g],)
        gl_ref = refs[3 + 2 * n_g]
        st_refs = refs[4 + 2 * n_g:5 + 3 * n_g]
        zs_ref = refs[5 + 3 * n_g]
    else:
        u_ref = refs[0]
        q_refs = refs[1:2 + n_g]
        kv_refs = (None,) * (n_g + 1)
        gl_ref = refs[2 + n_g]
        st_refs = refs[3 + n_g:4 + 2 * n_g]
        zs_ref = None

    def proj(c0, c1):
        return _dot(hb, w_ref[:, c0:c1])

    u_ref[...] = proj(0, POOL_WIDTH).reshape(s, t, POOL_WIDTH)
    off = POOL_WIDTH
    for g, dil in enumerate(tuple(dils) + (1,)):
        qw = DIL_W if g < n_g else SWA_QW
        kvw = 2 * DIL_W if g < n_g else 2 * SWA_KW
        z = proj(off, off + qw + kvw)
        off += qw + kvw
        zq = z[:, :qw] * SCALE
        zkv = z[:, qw:]
        st = st_refs[g]
        st[...] = zkv[rows - st.shape[-1]:, :].T
        if not fold:
            q_refs[g][...] = zq
        elif dil == 1:
            q_refs[g][0, 0] = zq.astype(BF16)
            kv_refs[g][0, 0] = zkv.astype(BF16)
        else:
            zf = jnp.concatenate([zq, zkv], axis=1)
            n_c = zf.shape[1] // BLOCK
            for c in range(n_c):
                zs_ref[c] = zf[:, c * BLOCK:(c + 1) * BLOCK]
            n = rows // dil
            for r in range(dil):
                part = jnp.concatenate([zs_ref[c, pl.ds(r, n, stride=dil), :] for c in range(n_c)], axis=1)
                q_refs[g][0, r] = part[:, :qw].astype(BF16)
                kv_refs[g][0, r] = part[:, qw:].astype(BF16)
    gl_ref[...] = proj(off, off + 3 * d)


def _in_proj(x3, ada3, g_pre, w_perm, layer, tm, keeps, fold, tie):
    nseq, tseq, d = x3.shape
    ntok = nseq * tseq
    s_blk, t_blk, tps = _tile_split(tseq, tm)
    n_tiles = ntok // tm
    in_w = w_perm.shape[-1]
    dils = tuple(dil for _, dil in DIL_GROUPS)
    widths = [(DIL_W, 2 * DIL_W)] * len(dils) + [(SWA_QW, 2 * SWA_KW)]

    def x_map(i):
        return (i // tps, i % tps, 0)

    def tok_map(i):
        return (i, 0)

    out_shapes = [jax.ShapeDtypeStruct((nseq, tseq, POOL_WIDTH), F32)]
    out_specs = [pl.BlockSpec((s_blk, t_blk, POOL_WIDTH), x_map)]
    for (qw, kvw), dil in zip(widths, dils + (1,)):
        if fold:
            for w in (qw, kvw):
                out_shapes.append(jax.ShapeDtypeStruct((nseq, dil, tseq // dil, w), BF16))
                out_specs.append(pl.BlockSpec((1, dil, tm // dil, w), lambda i: (i // tps, 0, i % tps, 0)))
        else:
            out_shapes.append(jax.ShapeDtypeStruct((ntok, qw), F32))
            out_specs.append(pl.BlockSpec((tm, qw), tok_map))
    out_shapes.append(jax.ShapeDtypeStruct((ntok, 3 * d), F32))
    out_specs.append(pl.BlockSpec((tm, 3 * d), tok_map))
    for keep, (_, kvw) in zip(keeps, widths):
        if fold:
            sb = min(tm, keep)
            bps = keep // sb
            first = (tseq - keep) // tm

            def st_map(i, bps=bps, first=first):
                return (i // tps, 0, jnp.maximum(i % tps - first, 0) * (1 if bps > 1 else 0))

            out_shapes.append(jax.ShapeDtypeStruct((nseq, kvw, keep), F32))
            out_specs.append(pl.BlockSpec((None, kvw, sb), st_map))
        else:
            out_shapes.append(jax.ShapeDtypeStruct((kvw, ntok), F32))
            out_specs.append(pl.BlockSpec((kvw, tm), lambda i: (0, i)))
    scratch = [pltpu.VMEM((3 * DIL_W // BLOCK, tm, BLOCK), F32)] if fold else []
    return pl.pallas_call(
        functools.partial(_in_proj_kernel, dils=dils, fold=fold),
        grid=(n_tiles,),
        in_specs=[
            pl.BlockSpec((s_blk, t_blk, d), x_map),
            pl.BlockSpec((s_blk, 1, d), lambda i: (i // tps, 0, 0)),
            pl.BlockSpec((s_blk, 1, d), lambda i: (i // tps, 0, 1)),
            pl.BlockSpec((1, 1, d), lambda i: (layer, 0, 0)),
            pl.BlockSpec((d, in_w), lambda i: (0, 0)),
            _TIE_SPEC,
        ],
        out_specs=out_specs,
        out_shape=out_shapes,
        scratch_shapes=scratch,
        compiler_params=_cparams(("arbitrary",)),
        name="in_proj",
    )(x3, ada3, ada3, g_pre, w_perm, tie)


PAIR = 2 * HEAD_DIM


def _band_attn_kernel(*refs, n_heads, n_kv, nq, with_sink):
    if with_sink:
        sink_ref, q_ref, kv_ref, halo_ref, bm_ref, mask_ref, o_ref = refs
        lse_ref = None
    else:
        q_ref, kv_ref, halo_ref, bm_ref, mask_ref, o_ref, lse_ref = refs
        sink_ref = None
    kw = n_kv * HEAD_DIM
    grp = n_heads // n_kv
    chunk = pl.program_id(2)
    lo, hi = mask_ref[0], mask_ref[1]
    low_lanes = lax.broadcasted_iota(jnp.int32, (BLOCK, PAIR), 1) < HEAD_DIM
    col = lax.broadcasted_iota(jnp.int32, (BLOCK, 4 * BLOCK), 1)
    prev_penalty = jnp.where(col % (2 * BLOCK) < BLOCK, NEG_INF, 0.0)

    def swap_halves(x):
        return jnp.concatenate([x[:, HEAD_DIM:], x[:, :HEAD_DIM]], axis=1)

    def pair_sources(kv_blk, p):
        if grp == 1:
            k = kv_blk[:, p * PAIR:(p + 1) * PAIR]
            v = kv_blk[:, kw + p * PAIR:kw + (p + 1) * PAIR]
            return (k, k), (v, v)
        kh = (2 * p) // grp
        c0 = (kh // 2) * PAIR
        k = kv_blk[:, c0:c0 + PAIR]
        v = kv_blk[:, kw + c0:kw + c0 + PAIR]
        ks, vs = swap_halves(k), swap_halves(v)
        return ((k, ks), (v, vs)) if kh % 2 == 0 else ((ks, k), (vs, v))

    def one_block(r0, kv_prev, first):
        qb = q_ref[pl.ds(r0, BLOCK), :]
        kv_cur = kv_ref[pl.ds(r0, BLOCK), :]
        for p in range(n_heads // 2):
            (kpe, kpo), (vpe, vpo) = pair_sources(kv_prev, p)
            (kce, kco), (vce, vco) = pair_sources(kv_cur, p)
            k_blk = jnp.concatenate([kpe * lo, kce * lo, kpo * hi, kco * hi], axis=0)
            v_blk = jnp.concatenate([
                jnp.concatenate([vpe * lo, lo], axis=1), jnp.concatenate([vce * lo, lo], axis=1),
                jnp.concatenate([vpo * hi, hi], axis=1), jnp.concatenate([vco * hi, hi], axis=1)], axis=0)
            s = _dot_nt(qb[:, p * PAIR:(p + 1) * PAIR], k_blk) + bm_ref[p]
            if first is not None:
                s = s + jnp.where(first, prev_penalty, 0.0)
            s_e, s_o = s[:, :2 * BLOCK], s[:, 2 * BLOCK:]
            m_e = jnp.max(jnp.maximum(s_e[:, :BLOCK], s_e[:, BLOCK:]), axis=-1, keepdims=True)
            m_o = jnp.max(jnp.maximum(s_o[:, :BLOCK], s_o[:, BLOCK:]), axis=-1, keepdims=True)
            if with_sink:
                m_e = jnp.maximum(m_e, sink_ref[2 * p])
                m_o = jnp.maximum(m_o, sink_ref[2 * p + 1])
            pr = jnp.concatenate([jnp.exp(s_e - m_e), jnp.exp(s_o - m_o)], axis=1).astype(BF16)
            res = _dot(pr, v_blk)
            den = res[:, PAIR:]
            if with_sink:
                den = den + jnp.where(low_lanes, jnp.exp(sink_ref[2 * p] - m_e), jnp.exp(sink_ref[2 * p + 1] - m_o))
            o_ref[pl.ds(r0, BLOCK), p * PAIR:(p + 1) * PAIR] = res[:, :PAIR] / den
            if lse_ref is not None:
                lse_ref[pl.ds(r0, BLOCK), p * PAIR:(p + 1) * PAIR] = jnp.where(low_lanes, m_e, m_o) + jnp.log(den)

    one_block(0, halo_ref[...], chunk == 0)

    def body(j, carry):
        r0 = pl.multiple_of(j * BLOCK, BLOCK)
        one_block(r0, kv_ref[pl.ds(r0 - BLOCK, BLOCK), :], None)
        return carry

    if nq > 1:
        lax.fori_loop(1, nq, body, 0)


def _band_attn(q, kv, bm, n_heads, n_kv, sinks=None):
    batch, dil, fold, qw = q.shape
    kvw = kv.shape[-1]
    nq = min(8, fold // BLOCK)
    rows = nq * BLOCK
    n_chunks = fold // rows
    with_sink = sinks is not None
    lanes_low = np.arange(PAIR) < HEAD_DIM
    masks = jnp.asarray(np.broadcast_to(np.stack([lanes_low, ~lanes_low])[:, None, :], (2, BLOCK, PAIR)), F32).astype(BF16)
    in_specs = [
        pl.BlockSpec((None, None, rows, qw), lambda b, r, c: (b, r, c, 0)),
        pl.BlockSpec((None, None, rows, kvw), lambda b, r, c: (b, r, c, 0)),
        pl.BlockSpec((None, None, BLOCK, kvw), lambda b, r, c: (b, r, jnp.maximum(c * nq - 1, 0), 0)),
        pl.BlockSpec(bm.shape, lambda b, r, c: (0, 0, 0)),
        pl.BlockSpec(masks.shape, lambda b, r, c: (0, 0, 0)),
    ]
    args = [q, kv, kv, bm, masks]
    o_spec = pl.BlockSpec((None, None, rows, qw), lambda b, r, c: (b, r, c, 0))
    o_shape = jax.ShapeDtypeStruct((batch, dil, fold, qw), F32)
    if with_sink:
        in_specs = [pl.BlockSpec(memory_space=pltpu.SMEM)] + in_specs
        args = [sinks] + args
        out_specs, out_shape = o_spec, o_shape
    else:
        out_specs, out_shape = [o_spec, o_spec], [o_shape, o_shape]
    return pl.pallas_call(
        functools.partial(_band_attn_kernel, n_heads=n_heads, n_kv=n_kv, nq=nq, with_sink=with_sink),
        grid=(batch, dil, n_chunks),
        in_specs=in_specs,
        out_specs=out_specs,
        out_shape=out_shape,
        compiler_params=_cparams(("parallel", "parallel", "parallel")),
        name="band_attn",
    )(*args)


SAMPLE_SEQ_UNROLL = 4


def _sample_attn_kernel(*refs, n_heads, n_kv, sb, t_new, with_sink, aliased):
    refs = list(refs)
    q_ref, newt_ref, cache_ref, bmc_ref, bmn_ref, tie_ref = refs[:6]
    refs = refs[6:]
    sink_ref = refs.pop(0) if with_sink else None
    if aliased:
        refs.pop(0)
    o_ref = refs.pop(0)
    lse_ref = None if with_sink else refs.pop(0)
    cout_ref = refs.pop(0)
    kw = n_kv * HEAD_DIM
    grp = n_heads // n_kv
    w, n_buf = cache_ref.shape[2:]
    first_new = BLOCK - t_new
    per_blk = BLOCK // t_new
    i = pl.program_id(0)
    lane = lax.broadcasted_iota(jnp.int32, (w, BLOCK), 1)

    def one_seq(s, carry):
        r0 = s * t_new if isinstance(s, int) else pl.multiple_of(s * t_new, t_new)
        qs = q_ref[pl.ds(r0, t_new), :]
        q_rows = []
        for h in range(n_heads):
            piece = qs[:, h * HEAD_DIM:(h + 1) * HEAD_DIM]
            parts = [piece if k == h // grp else jnp.zeros_like(piece) for k in range(n_kv)]
            q_rows.append(jnp.concatenate(parts, axis=1))
        qbd = jnp.concatenate(q_rows, axis=0).astype(BF16)
        cache = cache_ref[0, s]
        off = ((i * sb + s) % per_blk) * t_new
        placed = pltpu.roll(newt_ref[...], first_new - off, axis=1)
        s_c = _dot(qbd, cache[:kw].astype(BF16)) + bmc_ref[...]
        s_n = _dot(qbd, placed[:kw].astype(BF16)) + bmn_ref[...]
        m = jnp.maximum(jnp.max(s_c, axis=-1, keepdims=True), jnp.max(s_n, axis=-1, keepdims=True))
        if with_sink:
            m = jnp.maximum(m, sink_ref[:, :1])
        p_c = jnp.exp(s_c - m)
        p_n = jnp.exp(s_n - m)
        den = jnp.sum(p_c, axis=-1, keepdims=True) + jnp.sum(p_n, axis=-1, keepdims=True)
        if with_sink:
            den = den + jnp.exp(sink_ref[:, :1] - m)
        o_all = (_dot_nt(p_c.astype(BF16), cache[kw:].astype(BF16))
                 + _dot_nt(p_n.astype(BF16), placed[kw:].astype(BF16))) / den
        outs = []
        for h in range(n_heads):
            kh = h // grp
            outs.append(o_all[h * t_new:(h + 1) * t_new, kh * HEAD_DIM:(kh + 1) * HEAD_DIM])
        o_ref[pl.ds(r0, t_new), :] = jnp.concatenate(outs, axis=1)
        if lse_ref is not None:
            lse = m + jnp.log(den)
            lse_ref[pl.ds(r0, t_new), :] = jnp.concatenate(
                [jnp.broadcast_to(lse[h * t_new:(h + 1) * t_new], (t_new, HEAD_DIM)) for h in range(n_heads)], axis=1)
        rolled = pltpu.roll(cache, n_buf - t_new, axis=1)
        if n_buf > BLOCK:
            cout_ref[0, s, :, : n_buf - BLOCK] = rolled[:, : n_buf - BLOCK]
        cout_ref[0, s, :, n_buf - BLOCK:] = jnp.where(lane >= first_new, placed, rolled[:, n_buf - BLOCK:])
        return carry

    group = min(sb, SAMPLE_SEQ_UNROLL)
    if sb == group:
        for s in range(sb):
            one_seq(s, 0)
    else:
        def body(it, carry):
            for u in range(group):
                one_seq(it * group + u, carry)
            return carry

        lax.fori_loop(0, sb // group, body, 0)


def _sample_attn(q, newt, cache_t, prev_out, bmc, bmn, layer, n_heads, n_kv, t_new, tie, sink_rows=None):
    depth, nseq, w, n_buf = cache_t.shape
    qw = n_heads * HEAD_DIM
    sb = max(1, min(BLOCK // t_new, 2048 // n_buf))
    with_sink = sink_rows is not None
    aliased = prev_out is not None
    rows = sb * t_new
    per_blk = BLOCK // t_new
    in_specs = [
        pl.BlockSpec((rows, qw), lambda i: (i, 0)),
        pl.BlockSpec((w, BLOCK), lambda i: (0, (i * sb) // per_blk)),
        pl.BlockSpec((1, sb, w, n_buf), lambda i: (layer, i, 0, 0)),
        pl.BlockSpec(bmc.shape, lambda i: (0, 0)),
        pl.BlockSpec(bmn.shape, lambda i: (0, 0)),
        _TIE_SPEC,
    ]
    args = [q, newt, cache_t, bmc, bmn, tie]
    if with_sink:
        in_specs.append(pl.BlockSpec(sink_rows.shape, lambda i: (0, 0)))
        args.append(sink_rows)
    aliases = {}
    if aliased:
        aliases = {len(args): 1 if with_sink else 2}
        in_specs.append(pl.BlockSpec(memory_space=pl.ANY))
        args.append(prev_out)
    o_spec = pl.BlockSpec((rows, qw), lambda i: (i, 0))
    o_shape = jax.ShapeDtypeStruct((nseq * t_new, qw), F32)
    c_spec = pl.BlockSpec((1, sb, w, n_buf), lambda i: (layer, i, 0, 0))
    c_shape = jax.ShapeDtypeStruct(cache_t.shape, F32)
    if with_sink:
        out_specs, out_shape = [o_spec, c_spec], [o_shape, c_shape]
    else:
        out_specs, out_shape = [o_spec, o_spec, c_spec], [o_shape, o_shape, c_shape]
    return pl.pallas_call(
        functools.partial(_sample_attn_kernel, n_heads=n_heads, n_kv=n_kv, sb=sb, t_new=t_new,
                          with_sink=with_sink, aliased=aliased),
        grid=(nseq // sb,),
        in_specs=in_specs,
        out_specs=out_specs,
        out_shape=out_shape,
        input_output_aliases=aliases,
        compiler_params=_cparams(("parallel",)),
        name="sample_attn",
    )(*args)


def _post_kernel(x_ref, gt_ref, shf_ref, scf_ref, g2_ref, g3_ref, u_ref, halo_ref,
                 o1_ref, l1_ref, o2_ref, l2_ref, o3_ref, l3_ref, oc_ref, gl_ref,
                 pw_ref, ps_ref, wb_ref, wc_ref, wo_ref, wrt_ref, rb_ref, tie_ref,
                 xmid_ref, h2_ref, idx_ref, rank_ref, wk_ref, counts_ref, wfull_ref, cnt_ref, *unfold_refs,
                 tps, full_windows, dils):
    s, t, d = x_ref.shape
    rows = s * t
    i = pl.program_id(0)

    u = u_ref[...]
    halo = halo_ref[...]
    if not full_windows:
        halo = jnp.where(i % tps == 0, 0.0, halo)
    ue = jnp.concatenate([halo, u], axis=1)
    if full_windows:
        row = None
    else:
        row = (i % tps) * t + lax.broadcasted_iota(jnp.int32, (1, t, 1), 1)
    parts = []
    for g, win in enumerate(POOL_WINDOWS):
        cs = slice(g * POOL_CH, (g + 1) * POOL_CH)
        acc = ue[:, :, cs]
        base = 0
        span = 1
        while span < win:
            acc = acc[:, span:, :] + acc[:, : acc.shape[1] - span, :]
            base += span
            span *= 2
        tot = acc[:, POOL_HALO - base:, :]
        if full_windows:
            mean = tot / float(win)
        else:
            cnt = jnp.minimum(row + 1, win).astype(F32)
            mean = tot / cnt
        zg = (mean - u[:, :, cs]).reshape(rows, POOL_CH).astype(BF16)
        parts.append(_dot(zg, pw_ref[0, g]))
    a = jnp.concatenate(parts, axis=-1) * ps_ref[0]

    scratch = list(unfold_refs)

    def token_order(ref, dil):
        if dil == 1:
            return ref[...]
        scr = scratch.pop(0)
        n = rows // dil
        n_c = scr.shape[0]
        for r in range(dil):
            part = ref[0, r]
            for c in range(n_c):
                scr[c, pl.ds(r, n, stride=dil), :] = part[:, c * BLOCK:(c + 1) * BLOCK]
        return jnp.concatenate([scr[c] for c in range(n_c)], axis=1)

    o1, l1 = token_order(o1_ref, dils[0]), token_order(l1_ref, dils[0])
    o2, l2 = token_order(o2_ref, dils[1]), token_order(l2_ref, dils[1])
    o3, l3 = token_order(o3_ref, dils[2]), token_order(l3_ref, dils[2])

    lm = jnp.maximum(jnp.maximum(l1, l2), l3)
    e1, e2, e3 = jnp.exp(l1 - lm), jnp.exp(l2 - lm), jnp.exp(l3 - lm)
    esum = e1 + e2 + e3
    bmix = (e1 / esum) * o1 + (e2 / esum) * o2 + (e3 / esum) * o3
    b = _dot(bmix.astype(BF16), wb_ref[...])
    c = _dot(oc_ref[...].astype(BF16), wc_ref[...])
    g_a = _sigmoid(gl_ref[:, :d])
    g_b = _sigmoid(gl_ref[:, d:2 * d])
    g_c = _sigmoid(gl_ref[:, 2 * d:])
    mix = _dot((g_a * a + g_b * b + g_c * c).astype(BF16), wo_ref[...])

    x = x_ref[...]
    xm = x + gt_ref[...] * _rms(mix, g2_ref[0]).reshape(s, t, d)
    xmid_ref[...] = xm
    h2f = (_rms(xm, g3_ref[...]) * (1.0 + scf_ref[...]) + shf_ref[...]).reshape(rows, d)
    h2 = h2f.astype(BF16)
    _store_rows(h2_ref, _pack_chunks(h2f), rows)

    scores = _sigmoid(_dot_nt(wrt_ref[...], h2))
    work = scores + rb_ref[0]
    e_iota = lax.broadcasted_iota(jnp.int32, (N_EXPERTS, rows), 0).astype(F32)
    sel = jnp.zeros((N_EXPERTS, rows), F32)
    picks = []
    for _ in range(TOP_K):
        mx = jnp.max(work, axis=0, keepdims=True)
        pick = jnp.min(jnp.where(work == mx, e_iota, float(N_EXPERTS)), axis=0, keepdims=True)
        hit = e_iota == pick
        sel = jnp.where(hit, 1.0, sel)
        work = jnp.where(hit, NEG_INF, work)
        picks.append(pick)
    top_s = scores * sel
    wmat = top_s / jnp.sum(top_s, axis=0, keepdims=True) * ROUTED_SCALE

    @pl.when(i == 0)
    def _():
        cnt_ref[...] = jnp.zeros_like(cnt_ref)

    r_i = lax.broadcasted_iota(jnp.int32, (rows, rows), 0)
    c_i = lax.broadcasted_iota(jnp.int32, (rows, rows), 1)
    before = jnp.where(r_i < c_i, 1.0, 0.0).astype(BF16)
    rank_all = cnt_ref[...] + _dot(sel.astype(BF16), before)
    cnt_ref[...] = cnt_ref[...] + jnp.sum(sel, axis=1, keepdims=True)
    counts_ref[...] = cnt_ref[...].astype(jnp.int32)

    krow = lax.broadcasted_iota(jnp.int32, (TOP_K, rows), 0)
    idx = jnp.zeros((TOP_K, rows), F32)
    rank = jnp.zeros((TOP_K, rows), F32)
    wk = jnp.zeros((TOP_K, rows), F32)
    for k, pick in enumerate(picks):
        hit = e_iota == pick
        idx = jnp.where(krow == k, pick, idx)
        rank = jnp.where(krow == k, jnp.sum(jnp.where(hit, rank_all, 0.0), axis=0, keepdims=True), rank)
        wk = jnp.where(krow == k, jnp.sum(jnp.where(hit, wmat, 0.0), axis=0, keepdims=True), wk)
    idx_ref[...] = idx.astype(jnp.int32)
    rank_ref[...] = rank.astype(jnp.int32)
    eye = jnp.where(r_i == c_i, 1.0, 0.0).astype(BF16)

    def token_major(w):
        hi = w.astype(BF16)
        rest = w - hi.astype(F32)
        mid = rest.astype(BF16)
        lo = (rest - mid.astype(F32)).astype(BF16)
        return _dot_nt(eye, hi) + _dot_nt(eye, mid) + _dot_nt(eye, lo)

    wk_ref[...] = token_major(wk)
    wfull_ref[...] = token_major(wmat)


def _post_mixer(x3, ada3, u3, halo3, attn, gl, weights, layer, tm, full_windows, dils, tie):
    nseq, tseq, d = x3.shape
    ntok = nseq * tseq
    s_blk, t_blk, tps = _tile_split(tseq, tm)
    n_tiles = ntok // tm
    (g2, g3, pool_w, pool_scale, w_b_up, w_c_up, w_out, w_router, router_bias) = weights

    def x_map(i):
        return (i // tps, i % tps, 0)

    def tok_map(i):
        return (i, 0)

    def ada_spec(j):
        return pl.BlockSpec((s_blk, 1, d), lambda i: (i // tps, 0, j))

    if full_windows:
        halo_spec = pl.BlockSpec((s_blk, POOL_HALO, POOL_WIDTH), lambda i: (i, 0, 0))
    else:
        hb = t_blk // POOL_HALO
        halo_spec = pl.BlockSpec((1, POOL_HALO, POOL_WIDTH),
                                 lambda i: (i // tps, jnp.maximum((i % tps) * hb - 1, 0), 0))
    vec = pl.BlockSpec((1, 1, d), lambda i: (layer, 0, 0))
    attn_specs = []
    for g, dil in enumerate(dils):
        if dil == 1:
            spec = pl.BlockSpec((tm, DIL_W), tok_map)
        else:
            spec = pl.BlockSpec((1, dil, tm // dil, DIL_W), lambda i: (i // tps, 0, i % tps, 0))
        attn_specs += [spec, spec]
    in_specs = [
        pl.BlockSpec((s_blk, t_blk, d), x_map), ada_spec(2), ada_spec(3), ada_spec(4), vec, vec,
        pl.BlockSpec((s_blk, t_blk, POOL_WIDTH), x_map), halo_spec,
    ] + attn_specs + [
        pl.BlockSpec((tm, SWA_QW), tok_map), pl.BlockSpec((tm, 3 * d), tok_map),
        pl.BlockSpec((1,) + pool_w.shape[1:], lambda i: (layer, 0, 0, 0)),
        vec,
        pl.BlockSpec(w_b_up.shape, lambda i: (0, 0)), pl.BlockSpec(w_c_up.shape, lambda i: (0, 0)),
        pl.BlockSpec(w_out.shape, lambda i: (0, 0)), pl.BlockSpec(w_router.shape, lambda i: (0, 0)),
        pl.BlockSpec((1, N_EXPERTS, 1), lambda i: (layer, 0, 0)),
        _TIE_SPEC,
    ]
    out_shape = [
        jax.ShapeDtypeStruct((nseq, tseq, d), F32),
        jax.ShapeDtypeStruct((ntok * ROW_SUB, ROW_LANES), ROW_DTYPE),
        jax.ShapeDtypeStruct((TOP_K, ntok), jnp.int32),
        jax.ShapeDtypeStruct((TOP_K, ntok), jnp.int32),
        jax.ShapeDtypeStruct((ntok, TOP_K), F32),
        jax.ShapeDtypeStruct((N_EXPERTS, 1), jnp.int32),
        jax.ShapeDtypeStruct((ntok, N_EXPERTS), F32),
    ]
    out_specs = [
        pl.BlockSpec((s_blk, t_blk, d), x_map),
        pl.BlockSpec((tm * ROW_SUB, ROW_LANES), tok_map),
        pl.BlockSpec((TOP_K, tm), lambda i: (0, i)),
        pl.BlockSpec((TOP_K, tm), lambda i: (0, i)),
        pl.BlockSpec((tm, TOP_K), tok_map),
        pl.BlockSpec((N_EXPERTS, 1), lambda i: (0, 0)),
        pl.BlockSpec((tm, N_EXPERTS), tok_map),
    ]
    n_unfold = 2 * sum(1 for dil in dils if dil > 1)
    scratch = [pltpu.VMEM((N_EXPERTS, 1), F32)] + [pltpu.VMEM((DIL_W // BLOCK, tm, BLOCK), F32)] * n_unfold
    return pl.pallas_call(
        functools.partial(_post_kernel, tps=tps, full_windows=full_windows, dils=tuple(dils)),
        grid=(n_tiles,),
        in_specs=in_specs,
        out_specs=out_specs,
        out_shape=out_shape,
        scratch_shapes=scratch,
        compiler_params=_cparams(("arbitrary",)),
        name="post_mixer",
    )(x3, ada3, ada3, ada3, g2, g3, u3, halo3, *attn, gl,
      pool_w, pool_scale, w_b_up, w_c_up, w_out, w_router, router_bias, tie)


EXPERT_ROWS = 256


def _expert_kernel(bexp_ref, nvalid_ref, nused_ref, x_ref, wg_ref, wu_ref, wd_ref, tie_ref, y_ref, wg_s, wu_s, wd_s, h_s):
    i = pl.program_id(0)
    bm = x_ref.shape[0] // ROW_SUB
    sub = min(bm, EXPERT_ROWS)
    last = bexp_ref.shape[0] - 1
    up_blk = jnp.minimum(i, last)
    dn_blk = jnp.maximum(i - 1, 0)

    @pl.when((i == 0) | (bexp_ref[up_blk] != bexp_ref[jnp.maximum(up_blk - 1, 0)]))
    def _():
        wg_s[...] = wg_ref[0, 0].astype(BF16)
        wu_s[...] = wu_ref[0, 0].astype(BF16)

    @pl.when((i <= 1) | (bexp_ref[dn_blk] != bexp_ref[jnp.maximum(dn_blk - 1, 0)]))
    def _():
        wd_s[...] = wd_ref[0, 0].astype(BF16)

    @pl.when(i == 0)
    def _():
        h_s[...] = jnp.zeros_like(h_s)

    @pl.when(i <= nused_ref[0])
    def _():
        for c in range(bm // sub):
            y = _dot(h_s[c * sub:(c + 1) * sub, :], wd_s[...])
            _store_rows(y_ref, _pack_chunks(y), sub, first=c * sub)
        for c in range(bm // sub):
            x = _unpack_chunks(_load_rows(x_ref, sub, first=c * sub))
            row = c * sub + lax.broadcasted_iota(jnp.int32, (sub, 1), 0)
            x = jnp.where(row < nvalid_ref[up_blk], x, 0.0).astype(BF16)
            gate = _dot(x, wg_s[...])
            up = _dot(x, wu_s[...])
            h_s[c * sub:(c + 1) * sub, :] = (_silu(gate) * up).astype(BF16)

    @pl.when(i > nused_ref[0])
    def _():
        y_ref[...] = jnp.zeros_like(y_ref)


def _experts(x_rows, block_exp, nvalid, n_used, we_gate, we_up, we_down, layer, bm, tie):
    n_rows = x_rows.shape[0] // ROW_SUB
    d, ff = we_gate.shape[-2:]
    n_blocks = n_rows // bm
    last = n_blocks - 1

    def up_map(i, be, nv, nu):
        return (layer, be[jnp.minimum(i, last)], 0, 0)

    def down_map(i, be, nv, nu):
        return (layer, be[jnp.maximum(i - 1, 0)], 0, 0)

    grid_spec = pltpu.PrefetchScalarGridSpec(
        num_scalar_prefetch=3,
        grid=(n_blocks + 1,),
        in_specs=[
            pl.BlockSpec((bm * ROW_SUB, ROW_LANES), lambda i, be, nv, nu: (jnp.minimum(i, last), 0)),
            pl.BlockSpec((1, 1, d, ff), up_map),
            pl.BlockSpec((1, 1, d, ff), up_map),
            pl.BlockSpec((1, 1, ff, d), down_map),
            _TIE_SPEC,
        ],
        out_specs=pl.BlockSpec((bm * ROW_SUB, ROW_LANES), lambda i, be, nv, nu: (jnp.maximum(i - 1, 0), 0)),
        scratch_shapes=[pltpu.VMEM((d, ff), BF16), pltpu.VMEM((d, ff), BF16), pltpu.VMEM((ff, d), BF16),
                        pltpu.VMEM((bm, ff), BF16)],
    )
    return pl.pallas_call(
        _expert_kernel,
        grid_spec=grid_spec,
        out_shape=jax.ShapeDtypeStruct(x_rows.shape, x_rows.dtype),
        compiler_params=_cparams(("arbitrary",)),
        name="experts",
    )(block_exp, nvalid, n_used, x_rows, we_gate, we_up, we_down, tie)


def _final_kernel(x_ref, gt_ref, g4_ref, h2_ref, tie_ref, *refs, dense):
    s, t, d = x_ref.shape
    rows = s * t
    if dense:
        yr_ref, wg_ref, wu_ref, wd_ref, o_ref = refs
    else:
        yk_ref, wk_ref, wg_ref, wu_ref, wd_ref, o_ref = refs
    h2 = _unpack_chunks(_load_rows(h2_ref, rows)).astype(BF16)
    f = _dot((_silu(_dot(h2, wg_ref[...])) * _dot(h2, wu_ref[...])).astype(BF16), wd_ref[...])
    if dense:
        f = f + yr_ref[...]
    else:
        wk = wk_ref[...]
        for k in range(TOP_K):
            f = f + wk[:, k:k + 1] * _unpack_chunks(_load_rows(yk_ref, rows, lead=k))
    o_ref[...] = x_ref[...] + gt_ref[...] * _rms(f, g4_ref[0]).reshape(s, t, d)


def _final(xmid3, ada3, g4, h2_rows, routed, ws_gate, ws_up, ws_down, layer, tm, tie):
    nseq, tseq, d = xmid3.shape
    ntok = nseq * tseq
    s_blk, t_blk, tps = _tile_split(tseq, tm)
    dense = not isinstance(routed, tuple)

    def x_map(i):
        return (i // tps, i % tps, 0)

    if dense:
        routed_specs = [pl.BlockSpec((tm, d), lambda i: (i, 0))]
        routed = (routed,)
    else:
        routed_specs = [pl.BlockSpec((TOP_K, tm * ROW_SUB, ROW_LANES), lambda i: (0, i, 0)),
                        pl.BlockSpec((tm, TOP_K), lambda i: (i, 0))]
    return pl.pallas_call(
        functools.partial(_final_kernel, dense=dense),
        grid=(ntok // tm,),
        in_specs=[
            pl.BlockSpec((s_blk, t_blk, d), x_map),
            pl.BlockSpec((s_blk, 1, d), lambda i: (i // tps, 0, 5)),
            pl.BlockSpec((1, 1, d), lambda i: (layer, 0, 0)),
            pl.BlockSpec((tm * ROW_SUB, ROW_LANES), lambda i: (i, 0)),
            _TIE_SPEC,
        ] + routed_specs + [
            pl.BlockSpec(ws_gate.shape, lambda i: (0, 0)),
            pl.BlockSpec(ws_up.shape, lambda i: (0, 0)),
            pl.BlockSpec(ws_down.shape, lambda i: (0, 0)),
        ],
        out_specs=pl.BlockSpec((s_blk, t_blk, d), x_map),
        out_shape=jax.ShapeDtypeStruct((nseq, tseq, d), F32),
        compiler_params=_cparams(("parallel",)),
        name="final",
    )(xmid3, ada3, g4, h2_rows, tie, *routed, ws_gate, ws_up, ws_down)


DENSE_EXPERTS_PER_STEP = 2


def _dense_moe_kernel(h2_ref, wt_ref, wg_ref, wu_ref, wd_ref, y_ref, x_s):
    step = pl.program_id(0)
    rows = y_ref.shape[0]
    n_e = wg_ref.shape[1]

    @pl.when(step == 0)
    def _():
        x_s[...] = _unpack_chunks(_load_rows(h2_ref, rows)).astype(BF16)
        y_ref[...] = jnp.zeros_like(y_ref)

    x = x_s[...]
    wt = wt_ref[...]
    lane = lax.broadcasted_iota(jnp.int32, wt.shape, 1)
    acts, downs = [], []
    for j in range(n_e):
        w_col = jnp.sum(jnp.where(lane == step * n_e + j, wt, 0.0), axis=1, keepdims=True)
        gate = _dot(x, wg_ref[0, j].astype(BF16))
        up = _dot(x, wu_ref[0, j].astype(BF16))
        acts.append((w_col * (_silu(gate) * up)).astype(BF16))
        downs.append(wd_ref[0, j].astype(BF16))
    y_ref[...] += _dot(jnp.concatenate(acts, axis=1), jnp.concatenate(downs, axis=0))


def _dense_moe(h2_rows, wfull_t, we_gate, we_up, we_down, layer):
    ntok, n_exp = wfull_t.shape
    d, ff = we_gate.shape[-2:]
    n_e = DENSE_EXPERTS_PER_STEP
    return pl.pallas_call(
        _dense_moe_kernel,
        grid=(n_exp // n_e,),
        in_specs=[
            pl.BlockSpec(h2_rows.shape, lambda e: (0, 0)),
            pl.BlockSpec((ntok, n_exp), lambda e: (0, 0)),
            pl.BlockSpec((1, n_e, d, ff), lambda e: (layer, e, 0, 0)),
            pl.BlockSpec((1, n_e, d, ff), lambda e: (layer, e, 0, 0)),
            pl.BlockSpec((1, n_e, ff, d), lambda e: (layer, e, 0, 0)),
        ],
        out_specs=pl.BlockSpec((ntok, d), lambda e: (0, 0)),
        out_shape=jax.ShapeDtypeStruct((ntok, d), F32),
        scratch_shapes=[pltpu.VMEM((ntok, d), BF16)],
        compiler_params=_cparams(("arbitrary",)),
        name="dense_moe",
    )(h2_rows, wfull_t, we_gate, we_up, we_down)


def _t5_bucket(dist):
    n = np.asarray(dist, dtype=np.int64)
    exact = N_BUCKETS // 2
    log_ratio = np.log(np.maximum(n, 1) / exact) / np.log(MAX_DISTANCE / exact)
    large = np.minimum(exact + (log_ratio * (N_BUCKETS - exact)).astype(np.int64), N_BUCKETS - 1)
    return np.where(n < exact, n, large).astype(np.int32)


def _bias_table(rel_bias, h0, h1, steps, dil):
    steps = np.asarray(steps)
    buckets = np.where(steps >= 0, _t5_bucket(np.maximum(steps, 0) * dil), -1)
    onehot = buckets[..., None] == np.arange(N_BUCKETS)
    table = rel_bias[:, h0:h1].T.astype(F32).reshape((h1 - h0,) + (1,) * steps.ndim + (N_BUCKETS,))
    val = jnp.sum(jnp.where(onehot[None], table, 0.0), axis=-1)
    return jnp.where((steps >= 0)[None], val, NEG_INF)


def _band_steps():
    dist = BLOCK + np.arange(BLOCK)[:, None] - np.arange(2 * BLOCK)[None, :]
    return np.where((dist >= 0) & (dist <= BLOCK), dist, -1)


def _sample_steps(n_buf, dil, t_new):
    t = np.arange(t_new)[:, None]
    delta_c = n_buf + t - np.arange(n_buf)[None, :]
    ok_c = (delta_c % dil == 0) & (delta_c // dil <= BLOCK)
    lane = np.arange(BLOCK)[None, :]
    delta_n = t - (lane - (BLOCK - t_new))
    ok_n = (lane >= BLOCK - t_new) & (delta_n >= 0) & (delta_n % dil == 0) & (delta_n // dil <= BLOCK)
    return np.where(ok_c, delta_c // dil, -1), np.where(ok_n, delta_n // dil, -1)


def _route_plan(idx, rank, counts, bm):
    n_tok = idx.shape[1]
    n_blocks = -(-(n_tok * TOP_K + N_EXPERTS * (bm - 1)) // bm)
    counts = counts.reshape(N_EXPERTS)
    padded = (counts + bm - 1) // bm * bm
    pad_end = jnp.cumsum(padded)
    pad_start = pad_end - padded
    experts = jnp.arange(N_EXPERTS, dtype=jnp.int32)
    onehot = idx[:, :, None] == experts[None, None, :]
    dest = rank + jnp.sum(jnp.where(onehot, pad_start[None, None, :], 0), axis=-1)
    starts = jnp.arange(n_blocks, dtype=jnp.int32) * bm
    block_exp = jnp.minimum(jnp.sum((starts[:, None] >= pad_end[None, :]).astype(jnp.int32), axis=1), N_EXPERTS - 1)
    hot = block_exp[:, None] == experts[None, :]
    blk_cnt = jnp.sum(jnp.where(hot, counts[None, :], 0), axis=1)
    blk_start = jnp.sum(jnp.where(hot, pad_start[None, :], 0), axis=1)
    nvalid = jnp.clip(blk_cnt - (starts - blk_start), 0, bm).astype(jnp.int32)
    n_used = (pad_end[-1] // bm).astype(jnp.int32).reshape(1)
    return dest.astype(jnp.int32), block_exp.astype(jnp.int32), nvalid, n_used, n_blocks * bm


def _moe_routed(h2_rows, idx, rank, counts, we_gate, we_up, we_down, layer, bm, tie):
    n_tok = idx.shape[1]
    dest, block_exp, nvalid, n_used, n_rows = _route_plan(idx, rank, counts, bm)
    x_sorted = _sc_scatter_rows(h2_rows.reshape(n_tok, ROW_SUB, ROW_LANES), dest, n_rows)
    y_sorted = _experts(x_sorted.reshape(n_rows * ROW_SUB, ROW_LANES), block_exp, nvalid, n_used,
                        we_gate, we_up, we_down, layer, bm, tie)
    yk = _sc_gather_rows(y_sorted.reshape(n_rows, ROW_SUB, ROW_LANES), dest)
    return yk.reshape(TOP_K, n_tok * ROW_SUB, ROW_LANES), y_sorted


def _to_time_minor(c):
    depth, nseq, n_buf = c.shape[:3]
    return jnp.transpose(c, (0, 1, 3, 4, 5, 2)).reshape(depth, nseq, -1, n_buf)


def _from_time_minor(ct, n_heads):
    lead = ct.shape[:-2]
    rows = ct.shape[-1]
    nl = len(lead)
    x = ct.reshape(lead + (2, n_heads, HEAD_DIM, rows))
    return jnp.transpose(x, tuple(range(nl)) + (nl + 3, nl, nl + 1, nl + 2))


def kernel(x_prompt, x_sample, cache_b1, cache_b2, cache_b3, cache_c, state_pool, c_prompt, c_sample, rel_bias, w_ada, b_ada, g_pre_mix, g_post_mix, g_pre_ffn, g_post_ffn, w_in, pool_w, pool_scale, w_b_up, w_c_up, sinks, w_out, w_router, router_bias, we_gate, we_up, we_down, ws_gate, ws_up, ws_down):
    batch, seq, d = x_prompt.shape
    dec_batch, dec_seq, _ = x_sample.shape
    depth = w_in.shape[0]
    tm_p = 256
    tm_s = 128
    n_dil = len(DIL_GROUPS)
    dils = tuple(dil for _, dil in DIL_GROUPS)

    n_seq_all = batch + dec_batch
    pad = -n_seq_all % 8
    c_all = jnp.concatenate([c_prompt, c_sample, jnp.zeros((pad, d), F32)], axis=0)
    ada = _ada_all(c_all, w_ada, b_ada)

    head0 = [g * DIL_HEADS for g in range(n_dil)] + [n_dil * DIL_HEADS]
    head1 = [(g + 1) * DIL_HEADS for g in range(n_dil)] + [n_dil * DIL_HEADS + SWA_HEADS]
    band = []
    for g, dil in enumerate(dils + (1,)):
        tab = _bias_table(rel_bias, head0[g], head1[g], _band_steps(), dil)
        n_h = head1[g] - head0[g]
        band.append(tab.reshape(n_h // 2, 2, BLOCK, 2 * BLOCK).transpose(0, 2, 1, 3).reshape(n_h // 2, BLOCK, 4 * BLOCK))
    caches = (cache_b1, cache_b2, cache_b3, cache_c)
    samp = []
    for g, dil in enumerate(dils + (1,)):
        n_buf = caches[g].shape[2]
        sc_steps, sn_steps = _sample_steps(n_buf, dil, dec_seq)
        rows = (head1[g] - head0[g]) * dec_seq
        samp.append((_bias_table(rel_bias, head0[g], head1[g], sc_steps, dil).reshape(rows, n_buf),
                     _bias_table(rel_bias, head0[g], head1[g], sn_steps, dil).reshape(rows, BLOCK)))

    splits = np.cumsum([0, POOL_WIDTH, 3 * DIL_W, 3 * DIL_W, 3 * DIL_W, SWA_QW, SWA_KW, SWA_KW, 3 * d])
    col_slices = [(int(splits[0]), int(splits[1]))]
    for g in range(n_dil):
        for part in range(3):
            c0 = int(splits[1 + part]) + g * DIL_W
            col_slices.append((c0, c0 + DIL_W))
    col_slices.append((int(splits[4]), int(splits[8])))

    vec3 = lambda a: a.reshape(depth, 1, -1)
    g1, g2, g3, g4 = vec3(g_pre_mix), vec3(g_post_mix), vec3(g_pre_ffn), vec3(g_post_ffn)
    ps3, rb3 = vec3(pool_scale), router_bias.reshape(depth, N_EXPERTS, 1)
    cache_t = [_to_time_minor(c) for c in caches]
    pool_halo = jnp.pad(state_pool, ((0, 0), (0, 0), (POOL_HALO - state_pool.shape[2], 0), (0, 0)))
    pw = pool_w.astype(BF16)

    xp, xs = x_prompt, x_sample
    no_tie = jnp.zeros((8, BLOCK), F32)
    cache_out = [None] * (n_dil + 1)
    st_p = [[] for _ in range(n_dil + 2)]
    pool_s = []
    keeps_p = tuple(min(win, seq) for win, _ in DIL_GROUPS) + (min(BLOCK, seq),)
    n_heads = (DIL_HEADS,) * n_dil + (SWA_HEADS,)
    n_kvs = (DIL_HEADS,) * n_dil + (SWA_KV_HEADS,)
    for l in range(depth):
        w_perm = jnp.concatenate([w_in[l][:, a:b] for a, b in col_slices], axis=1).astype(BF16)
        wb, wc, wo, wr = w_b_up[l].astype(BF16), w_c_up[l].astype(BF16), w_out[l].astype(BF16), w_router[l].T.astype(BF16)
        wsg, wsu, wsd = ws_gate[l].astype(BF16), ws_up[l].astype(BF16), ws_down[l].astype(BF16)
        post_w = (g2, g3, pw, ps3, wb, wc, wo, wr, rb3)
        ada_p = ada[l, :batch].reshape(batch, 1, -1)
        ada_s = ada[l, batch:n_seq_all].reshape(dec_batch, 1, -1)
        sink_rows = jnp.broadcast_to(jnp.repeat(sinks[l].astype(F32), dec_seq)[:, None], (SWA_HEADS * dec_seq, BLOCK))

        outs = _in_proj(xp, ada_p, g1, w_perm, l, tm_p, keeps_p, True, no_tie)
        u, gl, states = outs[0], outs[3 + 2 * n_dil], outs[4 + 2 * n_dil:]
        attn = []
        for g in range(n_dil):
            o, lse = _band_attn(outs[1 + 2 * g], outs[2 + 2 * g], band[g], DIL_HEADS, DIL_HEADS)
            if dils[g] == 1:
                o, lse = o.reshape(batch * seq, DIL_W), lse.reshape(batch * seq, DIL_W)
            attn += [o, lse]
        oc = _band_attn(outs[1 + 2 * n_dil], outs[2 + 2 * n_dil], band[n_dil], SWA_HEADS, SWA_KV_HEADS, sinks=sinks[l])
        attn.append(oc.reshape(batch * seq, SWA_QW))
        xmid_p, h2_p, idx, rank, wk, counts, _ = _post_mixer(xp, ada_p, u, u, attn, gl, post_w, l, tm_p, False, dils,
                                                             no_tie)
        for k in range(n_dil + 1):
            st_p[k].append(_from_time_minor(states[k], n_kvs[k]))
        st_p[n_dil + 1].append(u[:, seq - (POOL_HALO - 1):])

        outs = _in_proj(xs, ada_s, g1, w_perm, l, tm_s, (dec_seq,) * (n_dil + 1), False, h2_p)
        u, gl, states = outs[0], outs[2 + n_dil], outs[3 + n_dil:]
        late = max(range(n_dil + 1), key=lambda g: cache_t[g].shape[-1])
        attn_s = [None] * (n_dil + 1)

        def sample_attn(g, tie):
            res = _sample_attn(outs[1 + g], states[g], cache_t[g], cache_out[g], samp[g][0], samp[g][1], l,
                               n_heads[g], n_kvs[g], dec_seq, tie, sink_rows=sink_rows if g == n_dil else None)
            attn_s[g] = list(res[:-1])
            cache_out[g] = res[-1]

        for g in range(n_dil + 1):
            if g != late:
                sample_attn(g, no_tie)
        attn_done = functools.reduce(jnp.add, [attn_s[g][0][:8, :BLOCK] for g in range(n_dil + 1) if g != late])
        y_k, y_sorted = _moe_routed(h2_p, idx, rank, counts, we_gate, we_up, we_down, l, 512, attn_done)
        sample_attn(late, y_sorted)
        attn = [a for pair in attn_s for a in pair]
        xmid, h2, _, _, _, _, wfull = _post_mixer(xs, ada_s, u, pool_halo[l], attn, gl, post_w, l, tm_s, True,
                                                  (1,) * n_dil, y_sorted)
        y_r = _dense_moe(h2, wfull, we_gate, we_up, we_down, l)
        xs = _final(xmid, ada_s, g4, h2, y_r, wsg, wsu, wsd, l, tm_s, no_tie)
        xp = _final(xmid_p, ada_p, g4, h2_p, (y_k, wk), wsg, wsu, wsd, l, tm_p, xs)
        pool_s.append(jnp.concatenate([state_pool[l], u], axis=1)[:, -(POOL_HALO - 1):])

    b1_p, b2_p, b3_p, c_p, pool_p = [jnp.stack(s, axis=0) for s in st_p]
    outs_s = [_from_time_minor(co, n_kvs[g]) for g, co in enumerate(cache_out)]
    return (xp, xs, b1_p, b2_p, b3_p, c_p, pool_p, outs_s[0], outs_s[1], outs_s[2], outs_s[3], jnp.stack(pool_s, axis=0))
```

```python
import functools

import numpy as np
import jax
import jax.numpy as jnp
from jax import lax
from jax.experimental import pallas as pl
from jax.experimental.pallas import tpu as pltpu
from jax.experimental.pallas import tpu_sc as plsc

F32 = jnp.float32
BF16 = jnp.bfloat16

HEAD_DIM = 64
SCALE = HEAD_DIM ** -0.5
BLOCK = 128
POOL_WINDOWS = (2, 4, 8, 16)
POOL_CH = 128
POOL_WIDTH = len(POOL_WINDOWS) * POOL_CH
POOL_HALO = 16
DIL_GROUPS = ((128, 1), (512, 4), (2048, 16))
DIL_HEADS = 4
DIL_W = DIL_HEADS * HEAD_DIM
SWA_HEADS = 8
SWA_KV_HEADS = 2
SWA_QW = SWA_HEADS * HEAD_DIM
SWA_KW = SWA_KV_HEADS * HEAD_DIM
N_BUCKETS = 32
MAX_DISTANCE = 2048
N_EXPERTS = 64
TOP_K = 8
ROUTED_SCALE = 2.5
EPS = 1e-6
NEG_INF = float("-inf")

V7X_VMEM_BYTES = 64 * 1024 * 1024
VMEM_LIMIT = 56 * 1024 * 1024


def _cparams(sem):
    return pltpu.CompilerParams(dimension_semantics=sem, vmem_limit_bytes=VMEM_LIMIT)


def _rms(x, g):
    ms = jnp.mean(x * x, axis=-1, keepdims=True)
    return x * lax.rsqrt(ms + EPS) * g


def _sigmoid(x):
    return 1.0 / (1.0 + jnp.exp(-x))


def _silu(x):
    return x * _sigmoid(x)


def _dot(a, b):
    return jnp.dot(a, b, preferred_element_type=F32)


def _dot_nt(a, b):
    return lax.dot_general(a, b, (((1,), (1,)), ((), ())), preferred_element_type=F32)


def _tile_split(tseq, tm):
    if tseq >= tm:
        return 1, tm, tseq // tm
    return tm // tseq, tseq, 1


_TIE_SPEC = pl.BlockSpec(memory_space=pl.ANY)


ROW_SUB = 4
ROW_LANES = 128
ROW_DTYPE = jnp.uint32
HI_MASK = 0xFFFF0000


def _pack_chunks(x):
    half = x.shape[1] // 2
    hi = pltpu.bitcast(x[:, :half].astype(BF16).astype(F32), jnp.uint32) & jnp.uint32(HI_MASK)
    lo = pltpu.bitcast(x[:, half:].astype(BF16).astype(F32), jnp.uint32) >> 16
    w = hi | lo
    return [w[:, j * ROW_LANES:(j + 1) * ROW_LANES] for j in range(ROW_SUB)]


def _unpack_chunks(chunks):
    his = [pltpu.bitcast(w & jnp.uint32(HI_MASK), F32) for w in chunks]
    los = [pltpu.bitcast(w << 16, F32) for w in chunks]
    return jnp.concatenate(his + los, axis=1)


def _store_rows(ref, chunks, rows, first=0):
    for j, c in enumerate(chunks):
        ref[pl.ds(first * ROW_SUB + j, rows, stride=ROW_SUB), :] = c


def _load_rows(ref, rows, first=0, lead=None):
    if lead is None:
        return [ref[pl.ds(first * ROW_SUB + j, rows, stride=ROW_SUB), :] for j in range(ROW_SUB)]
    return [ref[lead, pl.ds(first * ROW_SUB + j, rows, stride=ROW_SUB), :] for j in range(ROW_SUB)]


SC_CORES = 2
SC_SUBCORES = 16
SC_WORKERS = SC_CORES * SC_SUBCORES
SC_CHUNK = 64


def _sc_scatter_rows(src, dest, n_out):
    n, sub, lanes = src.shape
    kk = dest.shape[0]
    ch = min(SC_CHUNK, n // SC_WORKERS)
    n_chunks = n // (SC_WORKERS * ch)
    idx3 = dest.reshape(kk, n // ch, ch).transpose(1, 0, 2)
    mesh = plsc.VectorSubcoreMesh(core_axis_name="c", subcore_axis_name="s")

    @functools.partial(
        pl.kernel, mesh=mesh,
        out_type=jax.ShapeDtypeStruct((n_out, sub, lanes), src.dtype),
        scratch_types=[pltpu.VMEM((kk, ch), jnp.int32), pltpu.VMEM((ch, sub, lanes), src.dtype),
                       pltpu.SemaphoreType.DMA],
    )
    def scatter_kernel(src_hbm, idx_hbm, out_hbm, idx_v, rows_v, sem):
        wid = lax.axis_index("s") * SC_CORES + lax.axis_index("c")

        @pl.loop(0, n_chunks)
        def _(c):
            j = wid * n_chunks + c
            pltpu.sync_copy(idx_hbm.at[j], idx_v)
            pltpu.sync_copy(src_hbm.at[pl.ds(j * ch, ch)], rows_v)
            copies = [pltpu.async_copy(rows_v, out_hbm.at[idx_v.at[q]], sem) for q in range(kk)]
            for cp in copies:
                cp.wait()

    return scatter_kernel(src, idx3)


def _sc_gather_rows(table, dest):
    v, sub, lanes = table.shape
    kk, n = dest.shape
    total = n * kk
    ch = min(SC_CHUNK, total // SC_WORKERS)
    n_chunks = total // (SC_WORKERS * ch)
    idx2 = dest.reshape(total // ch, ch)
    mesh = plsc.VectorSubcoreMesh(core_axis_name="c", subcore_axis_name="s")

    assert n_chunks % 2 == 0, "the two-buffer gather pipeline walks chunks in pairs"

    @functools.partial(
        pl.kernel, mesh=mesh,
        out_type=jax.ShapeDtypeStruct((total, sub, lanes), table.dtype),
        scratch_types=[pltpu.VMEM((n_chunks, ch), jnp.int32), pltpu.VMEM((2, ch, sub, lanes), table.dtype),
                       pltpu.SemaphoreType.DMA((2,)), pltpu.SemaphoreType.DMA((2,))],
    )
    def gather_kernel(tab_hbm, idx_hbm, out_hbm, idx_v, rows_v, gsem, wsem):
        wid = lax.axis_index("s") * SC_CORES + lax.axis_index("c")
        base = wid * n_chunks
        pltpu.sync_copy(idx_hbm.at[pl.ds(base, n_chunks)], idx_v)

        def gather(c, b):
            return pltpu.make_async_copy(tab_hbm.at[idx_v.at[c]], rows_v.at[b], gsem.at[b])

        def write(c, b):
            return pltpu.make_async_copy(rows_v.at[b], out_hbm.at[pl.ds((base + c) * ch, ch)], wsem.at[b])

        gather(0, 0).start()

        @pl.loop(0, n_chunks, step=2)
        def _(c0):
            for b in range(2):
                c = c0 + b
                gather(c, b).wait()

                @pl.when(c + 1 < n_chunks)
                def _():
                    @pl.when(c >= 1)
                    def _():
                        write(c - 1, 1 - b).wait()

                    gather(c + 1, 1 - b).start()

                write(c, b).start()

        write(n_chunks - 2, 0).wait()
        write(n_chunks - 1, 1).wait()

    return gather_kernel(table, idx2).reshape(kk, n, sub, lanes)


def _ada_kernel(c_ref, w_ref, b_ref, o_ref):
    c = _silu(c_ref[...]).astype(BF16)
    o_ref[0] = _dot(c, w_ref[0].astype(BF16)) + b_ref[0]


def _ada_all(c_all, w_ada, b_ada):
    depth, d, n = w_ada.shape
    rows = c_all.shape[0]
    tn = 1536
    return pl.pallas_call(
        _ada_kernel,
        grid=(depth, n // tn),
        in_specs=[
            pl.BlockSpec((rows, d), lambda l, j: (0, 0)),
            pl.BlockSpec((1, d, tn), lambda l, j: (l, 0, j)),
            pl.BlockSpec((1, 1, tn), lambda l, j: (l, 0, j)),
        ],
        out_specs=pl.BlockSpec((1, rows, tn), lambda l, j: (l, 0, j)),
        out_shape=jax.ShapeDtypeStruct((depth, rows, n), F32),
        compiler_params=_cparams(("parallel", "parallel")),
        name="ada",
    )(c_all, w_ada, b_ada.reshape(depth, 1, n))


def _in_proj_kernel(x_ref, sh_ref, sc_ref, g_ref, w_ref, tie_ref, *refs, dils, fold, n_alias):
    refs = refs[n_alias:]
    s, t, d = x_ref.shape
    rows = s * t
    h = _rms(x_ref[...], g_ref[...]) * (1.0 + sc_ref[...]) + sh_ref[...]
    hb = h.reshape(rows, d).astype(BF16)
    n_g = len(dils)
    if fold:
        u_ref = refs[0]
        q_refs = refs[1:1 + 2 * n_g:2] + (refs[1 + 2 * n_g],)
        kv_refs = refs[2:2 + 2 * n_g:2] + (refs[2 + 2 * n_g],)
        gl_ref = refs[3 + 2 * n_g]
        st_refs = refs[4 + 2 * n_g:5 + 3 * n_g]
        zs_ref = refs[5 + 3 * n_g]
    else:
        u_ref = refs[0]
        q_refs = refs[1:2 + n_g]
        kv_refs = (None,) * (n_g + 1)
        gl_ref = refs[2 + n_g]
        st_refs = refs[3 + n_g:4 + 2 * n_g]
        zs_ref = None

    def proj(c0, c1):
        return _dot(hb, w_ref[:, c0:c1])

    u_ref[...] = proj(0, POOL_WIDTH).reshape(s, t, POOL_WIDTH)
    q0 = POOL_WIDTH
    k0 = q0 + n_g * DIL_W
    v0 = k0 + n_g * DIL_W
    c0 = v0 + n_g * DIL_W
    for g, dil in enumerate(tuple(dils) + (1,)):
        if g < n_g:
            qw = DIL_W
            zq = proj(q0 + g * DIL_W, q0 + (g + 1) * DIL_W) * SCALE
            zkv = jnp.concatenate([proj(k0 + g * DIL_W, k0 + (g + 1) * DIL_W),
                                   proj(v0 + g * DIL_W, v0 + (g + 1) * DIL_W)], axis=1)
        else:
            qw = SWA_QW
            z = proj(c0, c0 + SWA_QW + 2 * SWA_KW)
            zq = z[:, :qw] * SCALE
            zkv = z[:, qw:]
        st = st_refs[g]
        st[...] = zkv[rows - st.shape[-1]:, :].T
        if not fold:
            q_refs[g][...] = zq
        elif dil == 1:
            q_refs[g][0, 0] = zq.astype(BF16)
            kv_refs[g][0, 0] = zkv.astype(BF16)
        else:
            zf = jnp.concatenate([zq, zkv], axis=1)
            n_c = zf.shape[1] // BLOCK
            for c in range(n_c):
                zs_ref[c] = zf[:, c * BLOCK:(c + 1) * BLOCK]
            n = rows // dil
            for r in range(dil):
                part = jnp.concatenate([zs_ref[c, pl.ds(r, n, stride=dil), :] for c in range(n_c)], axis=1)
                q_refs[g][0, r] = part[:, :qw].astype(BF16)
                kv_refs[g][0, r] = part[:, qw:].astype(BF16)
    gl_ref[...] = proj(c0 + SWA_QW + 2 * SWA_KW, c0 + SWA_QW + 2 * SWA_KW + 3 * d)


def _in_proj(x3, ada3, g_pre, w_perm, layer, tm, keeps, fold, tie, prev_states=None):
    nseq, tseq, d = x3.shape
    ntok = nseq * tseq
    s_blk, t_blk, tps = _tile_split(tseq, tm)
    n_tiles = ntok // tm
    in_w = w_perm.shape[-1]
    depth = g_pre.shape[0]
    dils = tuple(dil for _, dil in DIL_GROUPS)
    widths = [(DIL_W, 2 * DIL_W)] * len(dils) + [(SWA_QW, 2 * SWA_KW)]

    def x_map(i):
        return (i // tps, i % tps, 0)

    def tok_map(i):
        return (i, 0)

    out_shapes = [jax.ShapeDtypeStruct((nseq, tseq, POOL_WIDTH), F32)]
    out_specs = [pl.BlockSpec((s_blk, t_blk, POOL_WIDTH), x_map)]
    for (qw, kvw), dil in zip(widths, dils + (1,)):
        if fold:
            for w in (qw, kvw):
                out_shapes.append(jax.ShapeDtypeStruct((nseq, dil, tseq // dil, w), BF16))
                out_specs.append(pl.BlockSpec((1, dil, tm // dil, w), lambda i: (i // tps, 0, i % tps, 0)))
        else:
            out_shapes.append(jax.ShapeDtypeStruct((ntok, qw), F32))
            out_specs.append(pl.BlockSpec((tm, qw), tok_map))
    out_shapes.append(jax.ShapeDtypeStruct((ntok, 3 * d), F32))
    out_specs.append(pl.BlockSpec((tm, 3 * d), tok_map))
    for keep, (_, kvw) in zip(keeps, widths):
        if fold:
            sb = min(tm, keep)
            bps = keep // sb
            first = (tseq - keep) // tm

            def st_map(i, bps=bps, first=first):
                return (layer, i // tps, 0, jnp.maximum(i % tps - first, 0) * (1 if bps > 1 else 0))

            out_shapes.append(jax.ShapeDtypeStruct((depth, nseq, kvw, keep), F32))
            out_specs.append(pl.BlockSpec((None, None, kvw, sb), st_map))
        else:
            out_shapes.append(jax.ShapeDtypeStruct((kvw, ntok), F32))
            out_specs.append(pl.BlockSpec((kvw, tm), lambda i: (0, i)))
    scratch = [pltpu.VMEM((3 * DIL_W // BLOCK, tm, BLOCK), F32)] if fold else []
    aliased = list(prev_states) if prev_states is not None else []
    n_fixed = 6
    first_state = len(out_shapes) - len(keeps)
    aliases = {n_fixed + k: first_state + k for k in range(len(aliased))}
    return pl.pallas_call(
        functools.partial(_in_proj_kernel, dils=dils, fold=fold, n_alias=len(aliased)),
        grid=(n_tiles,),
        in_specs=[
            pl.BlockSpec((s_blk, t_blk, d), x_map),
            pl.BlockSpec((s_blk, 1, d), lambda i: (i // tps, 0, 0)),
            pl.BlockSpec((s_blk, 1, d), lambda i: (i // tps, 0, 1)),
            pl.BlockSpec((1, 1, d), lambda i: (layer, 0, 0)),
            pl.BlockSpec((d, in_w), lambda i: (0, 0)),
            _TIE_SPEC,
        ] + [pl.BlockSpec(memory_space=pl.ANY)] * len(aliased),
        out_specs=out_specs,
        out_shape=out_shapes,
        input_output_aliases=aliases,
        scratch_shapes=scratch,
        compiler_params=_cparams(("arbitrary",)),
        name="in_proj",
    )(x3, ada3, ada3, g_pre, w_perm, tie, *aliased)


PAIR = 2 * HEAD_DIM


def _band_attn_kernel(*refs, n_heads, n_kv, nq, with_sink):
    if with_sink:
        sink_ref, q_ref, kv_ref, halo_ref, bm_ref, mask_ref, o_ref = refs
        lse_ref = None
    else:
        q_ref, kv_ref, halo_ref, bm_ref, mask_ref, o_ref, lse_ref = refs
        sink_ref = None
    kw = n_kv * HEAD_DIM
    grp = n_heads // n_kv
    chunk = pl.program_id(2)
    lo, hi = mask_ref[0], mask_ref[1]
    low_lanes = lax.broadcasted_iota(jnp.int32, (BLOCK, PAIR), 1) < HEAD_DIM
    col = lax.broadcasted_iota(jnp.int32, (BLOCK, 4 * BLOCK), 1)
    prev_penalty = jnp.where(col % (2 * BLOCK) < BLOCK, NEG_INF, 0.0)

    def swap_halves(x):
        return jnp.concatenate([x[:, HEAD_DIM:], x[:, :HEAD_DIM]], axis=1)

    def pair_sources(kv_blk, p):
        if grp == 1:
            k = kv_blk[:, p * PAIR:(p + 1) * PAIR]
            v = kv_blk[:, kw + p * PAIR:kw + (p + 1) * PAIR]
            return (k, k), (v, v)
        kh = (2 * p) // grp
        c0 = (kh // 2) * PAIR
        k = kv_blk[:, c0:c0 + PAIR]
        v = kv_blk[:, kw + c0:kw + c0 + PAIR]
        ks, vs = swap_halves(k), swap_halves(v)
        return ((k, ks), (v, vs)) if kh % 2 == 0 else ((ks, k), (vs, v))

    def one_block(r0, kv_prev, first):
        qb = q_ref[pl.ds(r0, BLOCK), :]
        kv_cur = kv_ref[pl.ds(r0, BLOCK), :]
        for p in range(n_heads // 2):
            (kpe, kpo), (vpe, vpo) = pair_sources(kv_prev, p)
            (kce, kco), (vce, vco) = pair_sources(kv_cur, p)
            k_blk = jnp.concatenate([kpe * lo, kce * lo, kpo * hi, kco * hi], axis=0)
            v_blk = jnp.concatenate([
                jnp.concatenate([vpe * lo, lo], axis=1), jnp.concatenate([vce * lo, lo], axis=1),
                jnp.concatenate([vpo * hi, hi], axis=1), jnp.concatenate([vco * hi, hi], axis=1)], axis=0)
            s = _dot_nt(qb[:, p * PAIR:(p + 1) * PAIR], k_blk) + bm_ref[p]
            if first is not None:
                s = s + jnp.where(first, prev_penalty, 0.0)
            s_e, s_o = s[:, :2 * BLOCK], s[:, 2 * BLOCK:]
            m_e = jnp.max(jnp.maximum(s_e[:, :BLOCK], s_e[:, BLOCK:]), axis=-1, keepdims=True)
            m_o = jnp.max(jnp.maximum(s_o[:, :BLOCK], s_o[:, BLOCK:]), axis=-1, keepdims=True)
            if with_sink:
                m_e = jnp.maximum(m_e, sink_ref[2 * p])
                m_o = jnp.maximum(m_o, sink_ref[2 * p + 1])
            pr = jnp.concatenate([jnp.exp(s_e - m_e), jnp.exp(s_o - m_o)], axis=1).astype(BF16)
            res = _dot(pr, v_blk)
            den = res[:, PAIR:]
            if with_sink:
                den = den + jnp.where(low_lanes, jnp.exp(sink_ref[2 * p] - m_e), jnp.exp(sink_ref[2 * p + 1] - m_o))
            o_ref[pl.ds(r0, BLOCK), p * PAIR:(p + 1) * PAIR] = res[:, :PAIR] / den
            if lse_ref is not None:
                lse_ref[pl.ds(r0, BLOCK), p * PAIR:(p + 1) * PAIR] = jnp.where(low_lanes, m_e, m_o) + jnp.log(den)

    one_block(0, halo_ref[...], chunk == 0)

    def body(j, carry):
        r0 = pl.multiple_of(j * BLOCK, BLOCK)
        one_block(r0, kv_ref[pl.ds(r0 - BLOCK, BLOCK), :], None)
        return carry

    if nq > 1:
        lax.fori_loop(1, nq, body, 0)


def _band_attn(q, kv, bm, n_heads, n_kv, sinks=None):
    batch, dil, fold, qw = q.shape
    kvw = kv.shape[-1]
    nq = min(8, fold // BLOCK)
    rows = nq * BLOCK
    n_chunks = fold // rows
    with_sink = sinks is not None
    lanes_low = np.arange(PAIR) < HEAD_DIM
    masks = jnp.asarray(np.broadcast_to(np.stack([lanes_low, ~lanes_low])[:, None, :], (2, BLOCK, PAIR)), F32).astype(BF16)
    in_specs = [
        pl.BlockSpec((None, None, rows, qw), lambda b, r, c: (b, r, c, 0)),
        pl.BlockSpec((None, None, rows, kvw), lambda b, r, c: (b, r, c, 0)),
        pl.BlockSpec((None, None, BLOCK, kvw), lambda b, r, c: (b, r, jnp.maximum(c * nq - 1, 0), 0)),
        pl.BlockSpec(bm.shape, lambda b, r, c: (0, 0, 0)),
        pl.BlockSpec(masks.shape, lambda b, r, c: (0, 0, 0)),
    ]
    args = [q, kv, kv, bm, masks]
    o_spec = pl.BlockSpec((None, None, rows, qw), lambda b, r, c: (b, r, c, 0))
    o_shape = jax.ShapeDtypeStruct((batch, dil, fold, qw), F32)
    if with_sink:
        in_specs = [pl.BlockSpec(memory_space=pltpu.SMEM)] + in_specs
        args = [sinks] + args
        out_specs, out_shape = o_spec, o_shape
    else:
        out_specs, out_shape = [o_spec, o_spec], [o_shape, o_shape]
    return pl.pallas_call(
        functools.partial(_band_attn_kernel, n_heads=n_heads, n_kv=n_kv, nq=nq, with_sink=with_sink),
        grid=(batch, dil, n_chunks),
        in_specs=in_specs,
        out_specs=out_specs,
        out_shape=out_shape,
        compiler_params=_cparams(("parallel", "parallel", "parallel")),
        name="band_attn",
    )(*args)


SAMPLE_SEQ_UNROLL = 4


def _sample_attn_kernel(*refs, n_heads, n_kv, sb, t_new, with_sink, aliased):
    refs = list(refs)
    q_ref, newt_ref, cache_ref, bmc_ref, bmn_ref, tie_ref = refs[:6]
    refs = refs[6:]
    sink_ref = refs.pop(0) if with_sink else None
    if aliased:
        refs.pop(0)
    o_ref = refs.pop(0)
    lse_ref = None if with_sink else refs.pop(0)
    cout_ref = refs.pop(0)
    kw = n_kv * HEAD_DIM
    grp = n_heads // n_kv
    w, n_buf = cache_ref.shape[2:]
    first_new = BLOCK - t_new
    per_blk = BLOCK // t_new
    i = pl.program_id(0)
    lane = lax.broadcasted_iota(jnp.int32, (w, BLOCK), 1)

    def one_seq(s, carry):
        r0 = s * t_new if isinstance(s, int) else pl.multiple_of(s * t_new, t_new)
        qs = q_ref[pl.ds(r0, t_new), :]
        q_rows = []
        for h in range(n_heads):
            piece = qs[:, h * HEAD_DIM:(h + 1) * HEAD_DIM]
            parts = [piece if k == h // grp else jnp.zeros_like(piece) for k in range(n_kv)]
            q_rows.append(jnp.concatenate(parts, axis=1))
        qbd = jnp.concatenate(q_rows, axis=0).astype(BF16)
        cache = cache_ref[0, s]
        off = ((i * sb + s) % per_blk) * t_new
        placed = pltpu.roll(newt_ref[...], first_new - off, axis=1)
        s_c = _dot(qbd, cache[:kw].astype(BF16)) + bmc_ref[...]
        s_n = _dot(qbd, placed[:kw].astype(BF16)) + bmn_ref[...]
        m = jnp.maximum(jnp.max(s_c, axis=-1, keepdims=True), jnp.max(s_n, axis=-1, keepdims=True))
        if with_sink:
            m = jnp.maximum(m, sink_ref[:, :1])
        p_c = jnp.exp(s_c - m)
        p_n = jnp.exp(s_n - m)
        den = jnp.sum(p_c, axis=-1, keepdims=True) + jnp.sum(p_n, axis=-1, keepdims=True)
        if with_sink:
            den = den + jnp.exp(sink_ref[:, :1] - m)
        o_all = (_dot_nt(p_c.astype(BF16), cache[kw:].astype(BF16))
                 + _dot_nt(p_n.astype(BF16), placed[kw:].astype(BF16))) / den
        outs = []
        for h in range(n_heads):
            kh = h // grp
            outs.append(o_all[h * t_new:(h + 1) * t_new, kh * HEAD_DIM:(kh + 1) * HEAD_DIM])
        o_ref[pl.ds(r0, t_new), :] = jnp.concatenate(outs, axis=1)
        if lse_ref is not None:
            lse = m + jnp.log(den)
            lse_ref[pl.ds(r0, t_new), :] = jnp.concatenate(
                [jnp.broadcast_to(lse[h * t_new:(h + 1) * t_new], (t_new, HEAD_DIM)) for h in range(n_heads)], axis=1)
        rolled = pltpu.roll(cache, n_buf - t_new, axis=1)
        if n_buf > BLOCK:
            cout_ref[0, s, :, : n_buf - BLOCK] = rolled[:, : n_buf - BLOCK]
        cout_ref[0, s, :, n_buf - BLOCK:] = jnp.where(lane >= first_new, placed, rolled[:, n_buf - BLOCK:])
        return carry

    group = min(sb, SAMPLE_SEQ_UNROLL)
    if sb == group:
        for s in range(sb):
            one_seq(s, 0)
    else:
        def body(it, carry):
            for u in range(group):
                one_seq(it * group + u, carry)
            return carry

        lax.fori_loop(0, sb // group, body, 0)


def _sample_attn(q, newt, cache_t, prev_out, bmc, bmn, layer, n_heads, n_kv, t_new, tie, sink_rows=None):
    depth, nseq, w, n_buf = cache_t.shape
    qw = n_heads * HEAD_DIM
    sb = max(1, min(BLOCK // t_new, 2048 // n_buf))
    with_sink = sink_rows is not None
    aliased = prev_out is not None
    rows = sb * t_new
    per_blk = BLOCK // t_new
    in_specs = [
        pl.BlockSpec((rows, qw), lambda i: (i, 0)),
        pl.BlockSpec((w, BLOCK), lambda i: (0, (i * sb) // per_blk)),
        pl.BlockSpec((1, sb, w, n_buf), lambda i: (layer, i, 0, 0)),
        pl.BlockSpec(bmc.shape, lambda i: (0, 0)),
        pl.BlockSpec(bmn.shape, lambda i: (0, 0)),
        _TIE_SPEC,
    ]
    args = [q, newt, cache_t, bmc, bmn, tie]
    if with_sink:
        in_specs.append(pl.BlockSpec(sink_rows.shape, lambda i: (0, 0)))
        args.append(sink_rows)
    aliases = {}
    if aliased:
        aliases = {len(args): 1 if with_sink else 2}
        in_specs.append(pl.BlockSpec(memory_space=pl.ANY))
        args.append(prev_out)
    o_spec = pl.BlockSpec((rows, qw), lambda i: (i, 0))
    o_shape = jax.ShapeDtypeStruct((nseq * t_new, qw), F32)
    c_spec = pl.BlockSpec((1, sb, w, n_buf), lambda i: (layer, i, 0, 0))
    c_shape = jax.ShapeDtypeStruct(cache_t.shape, F32)
    if with_sink:
        out_specs, out_shape = [o_spec, c_spec], [o_shape, c_shape]
    else:
        out_specs, out_shape = [o_spec, o_spec, c_spec], [o_shape, o_shape, c_shape]
    return pl.pallas_call(
        functools.partial(_sample_attn_kernel, n_heads=n_heads, n_kv=n_kv, sb=sb, t_new=t_new,
                          with_sink=with_sink, aliased=aliased),
        grid=(nseq // sb,),
        in_specs=in_specs,
        out_specs=out_specs,
        out_shape=out_shape,
        input_output_aliases=aliases,
        compiler_params=_cparams(("parallel",)),
        name="sample_attn",
    )(*args)


def _post_kernel(x_ref, gt_ref, shf_ref, scf_ref, g2_ref, g3_ref, u_ref, halo_ref,
                 o1_ref, l1_ref, o2_ref, l2_ref, o3_ref, l3_ref, oc_ref, gl_ref,
                 pw_ref, ps_ref, wb_ref, wc_ref, wo_ref, wrt_ref, rb_ref, tie_ref,
                 xmid_ref, h2_ref, idx_ref, rank_ref, wk_ref, counts_ref, wfull_ref, cnt_ref, *unfold_refs,
                 tps, full_windows, dils):
    s, t, d = x_ref.shape
    rows = s * t
    i = pl.program_id(0)

    u = u_ref[...]
    halo = halo_ref[...]
    if not full_windows:
        halo = jnp.where(i % tps == 0, 0.0, halo)
    ue = jnp.concatenate([halo, u], axis=1)
    if full_windows:
        row = None
    else:
        row = (i % tps) * t + lax.broadcasted_iota(jnp.int32, (1, t, 1), 1)
    parts = []
    for g, win in enumerate(POOL_WINDOWS):
        cs = slice(g * POOL_CH, (g + 1) * POOL_CH)
        acc = ue[:, :, cs]
        base = 0
        span = 1
        while span < win:
            acc = acc[:, span:, :] + acc[:, : acc.shape[1] - span, :]
            base += span
            span *= 2
        tot = acc[:, POOL_HALO - base:, :]
        if full_windows:
            mean = tot / float(win)
        else:
            cnt = jnp.minimum(row + 1, win).astype(F32)
            mean = tot / cnt
        zg = (mean - u[:, :, cs]).reshape(rows, POOL_CH).astype(BF16)
        parts.append(_dot(zg, pw_ref[0, g]))
    a = jnp.concatenate(parts, axis=-1) * ps_ref[0]

    scratch = list(unfold_refs)

    def token_order(ref, dil):
        if dil == 1:
            return ref[...]
        scr = scratch.pop(0)
        n = rows // dil
        n_c = scr.shape[0]
        for r in range(dil):
            part = ref[0, r]
            for c in range(n_c):
                scr[c, pl.ds(r, n, stride=dil), :] = part[:, c * BLOCK:(c + 1) * BLOCK]
        return jnp.concatenate([scr[c] for c in range(n_c)], axis=1)

    o1, l1 = token_order(o1_ref, dils[0]), token_order(l1_ref, dils[0])
    o2, l2 = token_order(o2_ref, dils[1]), token_order(l2_ref, dils[1])
    o3, l3 = token_order(o3_ref, dils[2]), token_order(l3_ref, dils[2])

    lm = jnp.maximum(jnp.maximum(l1, l2), l3)
    e1, e2, e3 = jnp.exp(l1 - lm), jnp.exp(l2 - lm), jnp.exp(l3 - lm)
    esum = e1 + e2 + e3
    bmix = (e1 / esum) * o1 + (e2 / esum) * o2 + (e3 / esum) * o3
    b = _dot(bmix.astype(BF16), wb_ref[...])
    c = _dot(oc_ref[...].astype(BF16), wc_ref[...])
    g_a = _sigmoid(gl_ref[:, :d])
    g_b = _sigmoid(gl_ref[:, d:2 * d])
    g_c = _sigmoid(gl_ref[:, 2 * d:])
    mix = _dot((g_a * a + g_b * b + g_c * c).astype(BF16), wo_ref[...])

    x = x_ref[...]
    xm = x + gt_ref[...] * _rms(mix, g2_ref[0]).reshape(s, t, d)
    xmid_ref[...] = xm
    h2f = (_rms(xm, g3_ref[...]) * (1.0 + scf_ref[...]) + shf_ref[...]).reshape(rows, d)
    h2 = h2f.astype(BF16)
    _store_rows(h2_ref, _pack_chunks(h2f), rows)

    scores = _sigmoid(_dot_nt(wrt_ref[...], h2))
    work = scores + rb_ref[0]
    e_iota = lax.broadcasted_iota(jnp.int32, (N_EXPERTS, rows), 0).astype(F32)
    sel = jnp.zeros((N_EXPERTS, rows), F32)
    picks = []
    for _ in range(TOP_K):
        mx = jnp.max(work, axis=0, keepdims=True)
        pick = jnp.min(jnp.where(work == mx, e_iota, float(N_EXPERTS)), axis=0, keepdims=True)
        hit = e_iota == pick
        sel = jnp.where(hit, 1.0, sel)
        work = jnp.where(hit, NEG_INF, work)
        picks.append(pick)
    top_s = scores * sel
    wmat = top_s / jnp.sum(top_s, axis=0, keepdims=True) * ROUTED_SCALE

    @pl.when(i == 0)
    def _():
        cnt_ref[...] = jnp.zeros_like(cnt_ref)

    r_i = lax.broadcasted_iota(jnp.int32, (rows, rows), 0)
    c_i = lax.broadcasted_iota(jnp.int32, (rows, rows), 1)
    before = jnp.where(r_i < c_i, 1.0, 0.0).astype(BF16)
    rank_all = cnt_ref[...] + _dot(sel.astype(BF16), before)
    cnt_ref[...] = cnt_ref[...] + jnp.sum(sel, axis=1, keepdims=True)
    counts_ref[...] = cnt_ref[...].astype(jnp.int32)

    krow = lax.broadcasted_iota(jnp.int32, (TOP_K, rows), 0)
    idx = jnp.zeros((TOP_K, rows), F32)
    rank = jnp.zeros((TOP_K, rows), F32)
    wk = jnp.zeros((TOP_K, rows), F32)
    for k, pick in enumerate(picks):
        hit = e_iota == pick
        idx = jnp.where(krow == k, pick, idx)
        rank = jnp.where(krow == k, jnp.sum(jnp.where(hit, rank_all, 0.0), axis=0, keepdims=True), rank)
        wk = jnp.where(krow == k, jnp.sum(jnp.where(hit, wmat, 0.0), axis=0, keepdims=True), wk)
    idx_ref[...] = idx.astype(jnp.int32)
    rank_ref[...] = rank.astype(jnp.int32)
    eye = jnp.where(r_i == c_i, 1.0, 0.0).astype(BF16)

    def token_major(w):
        hi = w.astype(BF16)
        rest = w - hi.astype(F32)
        mid = rest.astype(BF16)
        lo = (rest - mid.astype(F32)).astype(BF16)
        return _dot_nt(eye, hi) + _dot_nt(eye, mid) + _dot_nt(eye, lo)

    wk_ref[...] = token_major(wk)
    wfull_ref[...] = token_major(wmat)


def _post_mixer(x3, ada3, u3, halo3, attn, gl, weights, layer, tm, full_windows, dils, tie):
    nseq, tseq, d = x3.shape
    ntok = nseq * tseq
    s_blk, t_blk, tps = _tile_split(tseq, tm)
    n_tiles = ntok // tm
    (g2, g3, pool_w, pool_scale, w_b_up, w_c_up, w_out, w_router, router_bias) = weights

    def x_map(i):
        return (i // tps, i % tps, 0)

    def tok_map(i):
        return (i, 0)

    def ada_spec(j):
        return pl.BlockSpec((s_blk, 1, d), lambda i: (i // tps, 0, j))

    if full_windows:
        halo_spec = pl.BlockSpec((s_blk, POOL_HALO, POOL_WIDTH), lambda i: (i, 0, 0))
    else:
        hb = t_blk // POOL_HALO
        halo_spec = pl.BlockSpec((1, POOL_HALO, POOL_WIDTH),
                                 lambda i: (i // tps, jnp.maximum((i % tps) * hb - 1, 0), 0))
    vec = pl.BlockSpec((1, 1, d), lambda i: (layer, 0, 0))
    attn_specs = []
    for g, dil in enumerate(dils):
        if dil == 1:
            spec = pl.BlockSpec((tm, DIL_W), tok_map)
        else:
            spec = pl.BlockSpec((1, dil, tm // dil, DIL_W), lambda i: (i // tps, 0, i % tps, 0))
        attn_specs += [spec, spec]
    in_specs = [
        pl.BlockSpec((s_blk, t_blk, d), x_map), ada_spec(2), ada_spec(3), ada_spec(4), vec, vec,
        pl.BlockSpec((s_blk, t_blk, POOL_WIDTH), x_map), halo_spec,
    ] + attn_specs + [
        pl.BlockSpec((tm, SWA_QW), tok_map), pl.BlockSpec((tm, 3 * d), tok_map),
        pl.BlockSpec((1,) + pool_w.shape[1:], lambda i: (layer, 0, 0, 0)),
        vec,
        pl.BlockSpec(w_b_up.shape, lambda i: (0, 0)), pl.BlockSpec(w_c_up.shape, lambda i: (0, 0)),
        pl.BlockSpec(w_out.shape, lambda i: (0, 0)), pl.BlockSpec(w_router.shape, lambda i: (0, 0)),
        pl.BlockSpec((1, N_EXPERTS, 1), lambda i: (layer, 0, 0)),
        _TIE_SPEC,
    ]
    out_shape = [
        jax.ShapeDtypeStruct((nseq, tseq, d), F32),
        jax.ShapeDtypeStruct((ntok * ROW_SUB, ROW_LANES), ROW_DTYPE),
        jax.ShapeDtypeStruct((TOP_K, ntok), jnp.int32),
        jax.ShapeDtypeStruct((TOP_K, ntok), jnp.int32),
        jax.ShapeDtypeStruct((ntok, TOP_K), F32),
        jax.ShapeDtypeStruct((N_EXPERTS, 1), jnp.int32),
        jax.ShapeDtypeStruct((ntok, N_EXPERTS), F32),
    ]
    out_specs = [
        pl.BlockSpec((s_blk, t_blk, d), x_map),
        pl.BlockSpec((tm * ROW_SUB, ROW_LANES), tok_map),
        pl.BlockSpec((TOP_K, tm), lambda i: (0, i)),
        pl.BlockSpec((TOP_K, tm), lambda i: (0, i)),
        pl.BlockSpec((tm, TOP_K), tok_map),
        pl.BlockSpec((N_EXPERTS, 1), lambda i: (0, 0)),
        pl.BlockSpec((tm, N_EXPERTS), tok_map),
    ]
    n_unfold = 2 * sum(1 for dil in dils if dil > 1)
    scratch = [pltpu.VMEM((N_EXPERTS, 1), F32)] + [pltpu.VMEM((DIL_W // BLOCK, tm, BLOCK), F32)] * n_unfold
    return pl.pallas_call(
        functools.partial(_post_kernel, tps=tps, full_windows=full_windows, dils=tuple(dils)),
        grid=(n_tiles,),
        in_specs=in_specs,
        out_specs=out_specs,
        out_shape=out_shape,
        scratch_shapes=scratch,
        compiler_params=_cparams(("arbitrary",)),
        name="post_mixer",
    )(x3, ada3, ada3, ada3, g2, g3, u3, halo3, *attn, gl,
      pool_w, pool_scale, w_b_up, w_c_up, w_out, w_router, router_bias, tie)


EXPERT_ROWS = 256


def _expert_kernel(bexp_ref, nvalid_ref, nused_ref, x_ref, wg_ref, wu_ref, wd_ref, tie_ref, y_ref, wg_s, wu_s, wd_s, h_s):
    i = pl.program_id(0)
    bm = x_ref.shape[0] // ROW_SUB
    sub = min(bm, EXPERT_ROWS)
    last = bexp_ref.shape[0] - 1
    up_blk = jnp.minimum(i, last)
    dn_blk = jnp.maximum(i - 1, 0)

    @pl.when((i == 0) | (bexp_ref[up_blk] != bexp_ref[jnp.maximum(up_blk - 1, 0)]))
    def _():
        wg_s[...] = wg_ref[0, 0].astype(BF16)
        wu_s[...] = wu_ref[0, 0].astype(BF16)

    @pl.when((i <= 1) | (bexp_ref[dn_blk] != bexp_ref[jnp.maximum(dn_blk - 1, 0)]))
    def _():
        wd_s[...] = wd_ref[0, 0].astype(BF16)

    @pl.when(i == 0)
    def _():
        h_s[...] = jnp.zeros_like(h_s)

    @pl.when(i <= nused_ref[0])
    def _():
        for c in range(bm // sub):
            y = _dot(h_s[c * sub:(c + 1) * sub, :], wd_s[...])
            _store_rows(y_ref, _pack_chunks(y), sub, first=c * sub)
        for c in range(bm // sub):
            x = _unpack_chunks(_load_rows(x_ref, sub, first=c * sub))
            row = c * sub + lax.broadcasted_iota(jnp.int32, (sub, 1), 0)
            x = jnp.where(row < nvalid_ref[up_blk], x, 0.0).astype(BF16)
            gate = _dot(x, wg_s[...])
            up = _dot(x, wu_s[...])
            h_s[c * sub:(c + 1) * sub, :] = (_silu(gate) * up).astype(BF16)

    @pl.when(i > nused_ref[0])
    def _():
        y_ref[...] = jnp.zeros_like(y_ref)


def _experts(x_rows, block_exp, nvalid, n_used, we_gate, we_up, we_down, layer, bm, tie):
    n_rows = x_rows.shape[0] // ROW_SUB
    d, ff = we_gate.shape[-2:]
    n_blocks = n_rows // bm
    last = n_blocks - 1

    def up_map(i, be, nv, nu):
        return (layer, be[jnp.minimum(i, last)], 0, 0)

    def down_map(i, be, nv, nu):
        return (layer, be[jnp.maximum(i - 1, 0)], 0, 0)

    grid_spec = pltpu.PrefetchScalarGridSpec(
        num_scalar_prefetch=3,
        grid=(n_blocks + 1,),
        in_specs=[
            pl.BlockSpec((bm * ROW_SUB, ROW_LANES), lambda i, be, nv, nu: (jnp.minimum(i, last), 0)),
            pl.BlockSpec((1, 1, d, ff), up_map),
            pl.BlockSpec((1, 1, d, ff), up_map),
            pl.BlockSpec((1, 1, ff, d), down_map),
            _TIE_SPEC,
        ],
        out_specs=pl.BlockSpec((bm * ROW_SUB, ROW_LANES), lambda i, be, nv, nu: (jnp.maximum(i - 1, 0), 0)),
        scratch_shapes=[pltpu.VMEM((d, ff), BF16), pltpu.VMEM((d, ff), BF16), pltpu.VMEM((ff, d), BF16),
                        pltpu.VMEM((bm, ff), BF16)],
    )
    return pl.pallas_call(
        _expert_kernel,
        grid_spec=grid_spec,
        out_shape=jax.ShapeDtypeStruct(x_rows.shape, x_rows.dtype),
        compiler_params=_cparams(("arbitrary",)),
        name="experts",
    )(block_exp, nvalid, n_used, x_rows, we_gate, we_up, we_down, tie)


def _final_kernel(x_ref, gt_ref, g4_ref, h2_ref, tie_ref, *refs, dense):
    s, t, d = x_ref.shape
    rows = s * t
    if dense:
        yr_ref, wg_ref, wu_ref, wd_ref, o_ref = refs
    else:
        yk_ref, wk_ref, wg_ref, wu_ref, wd_ref, o_ref = refs
    h2 = _unpack_chunks(_load_rows(h2_ref, rows)).astype(BF16)
    f = _dot((_silu(_dot(h2, wg_ref[...])) * _dot(h2, wu_ref[...])).astype(BF16), wd_ref[...])
    if dense:
        f = f + yr_ref[...]
    else:
        wk = wk_ref[...]
        for k in range(TOP_K):
            f = f + wk[:, k:k + 1] * _unpack_chunks(_load_rows(yk_ref, rows, lead=k))
    o_ref[...] = x_ref[...] + gt_ref[...] * _rms(f, g4_ref[0]).reshape(s, t, d)


def _final(xmid3, ada3, g4, h2_rows, routed, ws_gate, ws_up, ws_down, layer, tm, tie):
    nseq, tseq, d = xmid3.shape
    ntok = nseq * tseq
    s_blk, t_blk, tps = _tile_split(tseq, tm)
    dense = not isinstance(routed, tuple)

    def x_map(i):
        return (i // tps, i % tps, 0)

    if dense:
        routed_specs = [pl.BlockSpec((tm, d), lambda i: (i, 0))]
        routed = (routed,)
    else:
        routed_specs = [pl.BlockSpec((TOP_K, tm * ROW_SUB, ROW_LANES), lambda i: (0, i, 0)),
                        pl.BlockSpec((tm, TOP_K), lambda i: (i, 0))]
    return pl.pallas_call(
        functools.partial(_final_kernel, dense=dense),
        grid=(ntok // tm,),
        in_specs=[
            pl.BlockSpec((s_blk, t_blk, d), x_map),
            pl.BlockSpec((s_blk, 1, d), lambda i: (i // tps, 0, 5)),
            pl.BlockSpec((1, 1, d), lambda i: (layer, 0, 0)),
            pl.BlockSpec((tm * ROW_SUB, ROW_LANES), lambda i: (i, 0)),
            _TIE_SPEC,
        ] + routed_specs + [
            pl.BlockSpec(ws_gate.shape, lambda i: (0, 0)),
            pl.BlockSpec(ws_up.shape, lambda i: (0, 0)),
            pl.BlockSpec(ws_down.shape, lambda i: (0, 0)),
        ],
        out_specs=pl.BlockSpec((s_blk, t_blk, d), x_map),
        out_shape=jax.ShapeDtypeStruct((nseq, tseq, d), F32),
        compiler_params=_cparams(("parallel",)),
        name="final",
    )(xmid3, ada3, g4, h2_rows, tie, *routed, ws_gate, ws_up, ws_down)


DENSE_EXPERTS_PER_STEP = 2


def _dense_moe_kernel(h2_ref, wt_ref, wg_ref, wu_ref, wd_ref, y_ref, x_s):
    step = pl.program_id(0)
    rows = y_ref.shape[0]
    n_e = wg_ref.shape[1]

    @pl.when(step == 0)
    def _():
        x_s[...] = _unpack_chunks(_load_rows(h2_ref, rows)).astype(BF16)
        y_ref[...] = jnp.zeros_like(y_ref)

    x = x_s[...]
    wt = wt_ref[...]
    lane = lax.broadcasted_iota(jnp.int32, wt.shape, 1)
    acts, downs = [], []
    for j in range(n_e):
        w_col = jnp.sum(jnp.where(lane == step * n_e + j, wt, 0.0), axis=1, keepdims=True)
        gate = _dot(x, wg_ref[0, j].astype(BF16))
        up = _dot(x, wu_ref[0, j].astype(BF16))
        acts.append((w_col * (_silu(gate) * up)).astype(BF16))
        downs.append(wd_ref[0, j].astype(BF16))
    y_ref[...] += _dot(jnp.concatenate(acts, axis=1), jnp.concatenate(downs, axis=0))


def _dense_moe(h2_rows, wfull_t, we_gate, we_up, we_down, layer):
    ntok, n_exp = wfull_t.shape
    d, ff = we_gate.shape[-2:]
    n_e = DENSE_EXPERTS_PER_STEP
    return pl.pallas_call(
        _dense_moe_kernel,
        grid=(n_exp // n_e,),
        in_specs=[
            pl.BlockSpec(h2_rows.shape, lambda e: (0, 0)),
            pl.BlockSpec((ntok, n_exp), lambda e: (0, 0)),
            pl.BlockSpec((1, n_e, d, ff), lambda e: (layer, e, 0, 0)),
            pl.BlockSpec((1, n_e, d, ff), lambda e: (layer, e, 0, 0)),
            pl.BlockSpec((1, n_e, ff, d), lambda e: (layer, e, 0, 0)),
        ],
        out_specs=pl.BlockSpec((ntok, d), lambda e: (0, 0)),
        out_shape=jax.ShapeDtypeStruct((ntok, d), F32),
        scratch_shapes=[pltpu.VMEM((ntok, d), BF16)],
        compiler_params=_cparams(("arbitrary",)),
        name="dense_moe",
    )(h2_rows, wfull_t, we_gate, we_up, we_down)


def _t5_bucket(dist):
    n = np.asarray(dist, dtype=np.int64)
    exact = N_BUCKETS // 2
    log_ratio = np.log(np.maximum(n, 1) / exact) / np.log(MAX_DISTANCE / exact)
    large = np.minimum(exact + (log_ratio * (N_BUCKETS - exact)).astype(np.int64), N_BUCKETS - 1)
    return np.where(n < exact, n, large).astype(np.int32)


def _bias_table(rel_bias, h0, h1, steps, dil):
    steps = np.asarray(steps)
    buckets = np.where(steps >= 0, _t5_bucket(np.maximum(steps, 0) * dil), -1)
    onehot = buckets[..., None] == np.arange(N_BUCKETS)
    table = rel_bias[:, h0:h1].T.astype(F32).reshape((h1 - h0,) + (1,) * steps.ndim + (N_BUCKETS,))
    val = jnp.sum(jnp.where(onehot[None], table, 0.0), axis=-1)
    return jnp.where((steps >= 0)[None], val, NEG_INF)


def _band_steps():
    dist = BLOCK + np.arange(BLOCK)[:, None] - np.arange(2 * BLOCK)[None, :]
    return np.where((dist >= 0) & (dist <= BLOCK), dist, -1)


def _sample_steps(n_buf, dil, t_new):
    t = np.arange(t_new)[:, None]
    delta_c = n_buf + t - np.arange(n_buf)[None, :]
    ok_c = (delta_c % dil == 0) & (delta_c // dil <= BLOCK)
    lane = np.arange(BLOCK)[None, :]
    delta_n = t - (lane - (BLOCK - t_new))
    ok_n = (lane >= BLOCK - t_new) & (delta_n >= 0) & (delta_n % dil == 0) & (delta_n // dil <= BLOCK)
    return np.where(ok_c, delta_c // dil, -1), np.where(ok_n, delta_n // dil, -1)


def _route_plan(idx, rank, counts, bm):
    n_tok = idx.shape[1]
    n_blocks = -(-(n_tok * TOP_K + N_EXPERTS * (bm - 1)) // bm)
    counts = counts.reshape(N_EXPERTS)
    padded = (counts + bm - 1) // bm * bm
    pad_end = jnp.cumsum(padded)
    pad_start = pad_end - padded
    experts = jnp.arange(N_EXPERTS, dtype=jnp.int32)
    onehot = idx[:, :, None] == experts[None, None, :]
    dest = rank + jnp.sum(jnp.where(onehot, pad_start[None, None, :], 0), axis=-1)
    starts = jnp.arange(n_blocks, dtype=jnp.int32) * bm
    block_exp = jnp.minimum(jnp.sum((starts[:, None] >= pad_end[None, :]).astype(jnp.int32), axis=1), N_EXPERTS - 1)
    hot = block_exp[:, None] == experts[None, :]
    blk_cnt = jnp.sum(jnp.where(hot, counts[None, :], 0), axis=1)
    blk_start = jnp.sum(jnp.where(hot, pad_start[None, :], 0), axis=1)
    nvalid = jnp.clip(blk_cnt - (starts - blk_start), 0, bm).astype(jnp.int32)
    n_used = (pad_end[-1] // bm).astype(jnp.int32).reshape(1)
    return dest.astype(jnp.int32), block_exp.astype(jnp.int32), nvalid, n_used, n_blocks * bm


def _moe_routed(h2_rows, idx, rank, counts, we_gate, we_up, we_down, layer, bm, tie):
    n_tok = idx.shape[1]
    dest, block_exp, nvalid, n_used, n_rows = _route_plan(idx, rank, counts, bm)
    x_sorted = _sc_scatter_rows(h2_rows.reshape(n_tok, ROW_SUB, ROW_LANES), dest, n_rows)
    y_sorted = _experts(x_sorted.reshape(n_rows * ROW_SUB, ROW_LANES), block_exp, nvalid, n_used,
                        we_gate, we_up, we_down, layer, bm, tie)
    yk = _sc_gather_rows(y_sorted.reshape(n_rows, ROW_SUB, ROW_LANES), dest)
    return yk.reshape(TOP_K, n_tok * ROW_SUB, ROW_LANES), y_sorted


def _to_time_minor(c):
    depth, nseq, n_buf = c.shape[:3]
    return jnp.transpose(c, (0, 1, 3, 4, 5, 2)).reshape(depth, nseq, -1, n_buf)


def _from_time_minor(ct, n_heads):
    lead = ct.shape[:-2]
    rows = ct.shape[-1]
    nl = len(lead)
    x = ct.reshape(lead + (2, n_heads, HEAD_DIM, rows))
    return jnp.transpose(x, tuple(range(nl)) + (nl + 3, nl, nl + 1, nl + 2))


def kernel(x_prompt, x_sample, cache_b1, cache_b2, cache_b3, cache_c, state_pool, c_prompt, c_sample, rel_bias, w_ada, b_ada, g_pre_mix, g_post_mix, g_pre_ffn, g_post_ffn, w_in, pool_w, pool_scale, w_b_up, w_c_up, sinks, w_out, w_router, router_bias, we_gate, we_up, we_down, ws_gate, ws_up, ws_down):
    batch, seq, d = x_prompt.shape
    dec_batch, dec_seq, _ = x_sample.shape
    depth = w_in.shape[0]
    tm_p = 256
    tm_s = 128
    n_dil = len(DIL_GROUPS)
    dils = tuple(dil for _, dil in DIL_GROUPS)

    n_seq_all = batch + dec_batch
    pad = -n_seq_all % 8
    c_all = jnp.concatenate([c_prompt, c_sample, jnp.zeros((pad, d), F32)], axis=0)
    ada = _ada_all(c_all, w_ada, b_ada)

    head0 = [g * DIL_HEADS for g in range(n_dil)] + [n_dil * DIL_HEADS]
    head1 = [(g + 1) * DIL_HEADS for g in range(n_dil)] + [n_dil * DIL_HEADS + SWA_HEADS]
    band = []
    for g, dil in enumerate(dils + (1,)):
        tab = _bias_table(rel_bias, head0[g], head1[g], _band_steps(), dil)
        n_h = head1[g] - head0[g]
        band.append(tab.reshape(n_h // 2, 2, BLOCK, 2 * BLOCK).transpose(0, 2, 1, 3).reshape(n_h // 2, BLOCK, 4 * BLOCK))
    caches = (cache_b1, cache_b2, cache_b3, cache_c)
    samp = []
    for g, dil in enumerate(dils + (1,)):
        n_buf = caches[g].shape[2]
        sc_steps, sn_steps = _sample_steps(n_buf, dil, dec_seq)
        rows = (head1[g] - head0[g]) * dec_seq
        samp.append((_bias_table(rel_bias, head0[g], head1[g], sc_steps, dil).reshape(rows, n_buf),
                     _bias_table(rel_bias, head0[g], head1[g], sn_steps, dil).reshape(rows, BLOCK)))

    vec3 = lambda a: a.reshape(depth, 1, -1)
    g1, g2, g3, g4 = vec3(g_pre_mix), vec3(g_post_mix), vec3(g_pre_ffn), vec3(g_post_ffn)
    ps3, rb3 = vec3(pool_scale), router_bias.reshape(depth, N_EXPERTS, 1)
    cache_t = [_to_time_minor(c) for c in caches]
    pool_halo = jnp.pad(state_pool, ((0, 0), (0, 0), (POOL_HALO - state_pool.shape[2], 0), (0, 0)))
    pw = pool_w.astype(BF16)

    xp, xs = x_prompt, x_sample
    no_tie = jnp.zeros((8, BLOCK), F32)
    cache_out = [None] * (n_dil + 1)
    states_p = None
    pool_p = []
    pool_s = []
    keeps_p = tuple(min(win, seq) for win, _ in DIL_GROUPS) + (min(BLOCK, seq),)
    n_heads = (DIL_HEADS,) * n_dil + (SWA_HEADS,)
    n_kvs = (DIL_HEADS,) * n_dil + (SWA_KV_HEADS,)
    for l in range(depth):
        w_perm = w_in[l].astype(BF16)
        wb, wc, wo, wr = w_b_up[l].astype(BF16), w_c_up[l].astype(BF16), w_out[l].astype(BF16), w_router[l].T.astype(BF16)
        wsg, wsu, wsd = ws_gate[l].astype(BF16), ws_up[l].astype(BF16), ws_down[l].astype(BF16)
        post_w = (g2, g3, pw, ps3, wb, wc, wo, wr, rb3)
        ada_p = ada[l, :batch].reshape(batch, 1, -1)
        ada_s = ada[l, batch:n_seq_all].reshape(dec_batch, 1, -1)
        sink_rows = jnp.broadcast_to(jnp.repeat(sinks[l].astype(F32), dec_seq)[:, None], (SWA_HEADS * dec_seq, BLOCK))

        outs = _in_proj(xp, ada_p, g1, w_perm, l, tm_p, keeps_p, True, no_tie, states_p)
        u, gl, states_p = outs[0], outs[3 + 2 * n_dil], outs[4 + 2 * n_dil:]
        attn = []
        for g in range(n_dil):
            o, lse = _band_attn(outs[1 + 2 * g], outs[2 + 2 * g], band[g], DIL_HEADS, DIL_HEADS)
            if dils[g] == 1:
                o, lse = o.reshape(batch * seq, DIL_W), lse.reshape(batch * seq, DIL_W)
            attn += [o, lse]
        oc = _band_attn(outs[1 + 2 * n_dil], outs[2 + 2 * n_dil], band[n_dil], SWA_HEADS, SWA_KV_HEADS, sinks=sinks[l])
        attn.append(oc.reshape(batch * seq, SWA_QW))
        xmid_p, h2_p, idx, rank, wk, counts, _ = _post_mixer(xp, ada_p, u, u, attn, gl, post_w, l, tm_p, False, dils,
                                                             no_tie)
        pool_p.append(u[:, seq - (POOL_HALO - 1):])

        outs = _in_proj(xs, ada_s, g1, w_perm, l, tm_s, (dec_seq,) * (n_dil + 1), False, h2_p)
        u, gl, states = outs[0], outs[2 + n_dil], outs[3 + n_dil:]
        early = max(range(n_dil + 1), key=lambda g: cache_t[g].shape[-1])
        attn_s = [None] * (n_dil + 1)

        def sample_attn(g, tie):
            res = _sample_attn(outs[1 + g], states[g], cache_t[g], cache_out[g], samp[g][0], samp[g][1], l,
                               n_heads[g], n_kvs[g], dec_seq, tie, sink_rows=sink_rows if g == n_dil else None)
            attn_s[g] = list(res[:-1])
            cache_out[g] = res[-1]

        sample_attn(early, no_tie)
        y_k, y_sorted = _moe_routed(h2_p, idx, rank, counts, we_gate, we_up, we_down, l, 512, attn_s[early][0])
        for g in range(n_dil + 1):
            if g != early:
                sample_attn(g, y_sorted)
        attn = [a for pair in attn_s for a in pair]
        xmid, h2, _, _, _, _, wfull = _post_mixer(xs, ada_s, u, pool_halo[l], attn, gl, post_w, l, tm_s, True,
                                                  (1,) * n_dil, y_sorted)
        y_r = _dense_moe(h2, wfull, we_gate, we_up, we_down, l)
        xs = _final(xmid, ada_s, g4, h2, y_r, wsg, wsu, wsd, l, tm_s, no_tie)
        xp = _final(xmid_p, ada_p, g4, h2_p, (y_k, wk), wsg, wsu, wsd, l, tm_p, xs)
        pool_s.append(jnp.concatenate([state_pool[l], u], axis=1)[:, -(POOL_HALO - 1):])

    b1_p, b2_p, b3_p, c_p = [_from_time_minor(st, n_kvs[k]) for k, st in enumerate(states_p)]
    pool_p = jnp.stack(pool_p, axis=0)
    outs_s = [_from_time_minor(co, n_kvs[g]) for g, co in enumerate(cache_out)]
    return (xp, xs, b1_p, b2_p, b3_p, c_p, pool_p, outs_s[0], outs_s[1], outs_s[2], outs_s[3], jnp.stack(pool_s, axis=0))
```

```python
import functools

import numpy as np
import jax
import jax.numpy as jnp
from jax import lax
from jax.experimental import pallas as pl
from jax.experimental.pallas import tpu as pltpu
from jax.experimental.pallas import tpu_sc as plsc

F32 = jnp.float32
BF16 = jnp.bfloat16

HEAD_DIM = 64
SCALE = HEAD_DIM ** -0.5
BLOCK = 128
POOL_WINDOWS = (2, 4, 8, 16)
POOL_CH = 128
POOL_WIDTH = len(POOL_WINDOWS) * POOL_CH
POOL_HALO = 16
DIL_GROUPS = ((128, 1), (512, 4), (2048, 16))
DIL_HEADS = 4
DIL_W = DIL_HEADS * HEAD_DIM
SWA_HEADS = 8
SWA_KV_HEADS = 2
SWA_QW = SWA_HEADS * HEAD_DIM
SWA_KW = SWA_KV_HEADS * HEAD_DIM
N_BUCKETS = 32
MAX_DISTANCE = 2048
N_EXPERTS = 64
TOP_K = 8
ROUTED_SCALE = 2.5
EPS = 1e-6
NEG_INF = float("-inf")

V7X_VMEM_BYTES = 64 * 1024 * 1024
VMEM_LIMIT = 56 * 1024 * 1024


def _cparams(sem):
    return pltpu.CompilerParams(dimension_semantics=sem, vmem_limit_bytes=VMEM_LIMIT)


def _rms(x, g):
    ms = jnp.mean(x * x, axis=-1, keepdims=True)
    return x * lax.rsqrt(ms + EPS) * g


def _sigmoid(x):
    return 1.0 / (1.0 + jnp.exp(-x))


def _silu(x):
    return x * _sigmoid(x)


def _dot(a, b):
    return jnp.dot(a, b, preferred_element_type=F32)


def _dot_nt(a, b):
    return lax.dot_general(a, b, (((1,), (1,)), ((), ())), preferred_element_type=F32)


def _tile_split(tseq, tm):
    if tseq >= tm:
        return 1, tm, tseq // tm
    return tm // tseq, tseq, 1


_TIE_SPEC = pl.BlockSpec(memory_space=pl.ANY)


ROW_SUB = 4
ROW_LANES = 128
ROW_DTYPE = jnp.uint32
HI_MASK = 0xFFFF0000


def _pack_chunks(x):
    half = x.shape[1] // 2
    hi = pltpu.bitcast(x[:, :half].astype(BF16).astype(F32), jnp.uint32) & jnp.uint32(HI_MASK)
    lo = pltpu.bitcast(x[:, half:].astype(BF16).astype(F32), jnp.uint32) >> 16
    w = hi | lo
    return [w[:, j * ROW_LANES:(j + 1) * ROW_LANES] for j in range(ROW_SUB)]


def _unpack_chunks(chunks):
    his = [pltpu.bitcast(w & jnp.uint32(HI_MASK), F32) for w in chunks]
    los = [pltpu.bitcast(w << 16, F32) for w in chunks]
    return jnp.concatenate(his + los, axis=1)


def _store_rows(ref, chunks, rows, first=0):
    for j, c in enumerate(chunks):
        ref[pl.ds(first * ROW_SUB + j, rows, stride=ROW_SUB), :] = c


def _load_rows(ref, rows, first=0, lead=None):
    if lead is None:
        return [ref[pl.ds(first * ROW_SUB + j, rows, stride=ROW_SUB), :] for j in range(ROW_SUB)]
    return [ref[lead, pl.ds(first * ROW_SUB + j, rows, stride=ROW_SUB), :] for j in range(ROW_SUB)]


SC_CORES = 2
SC_SUBCORES = 16
SC_WORKERS = SC_CORES * SC_SUBCORES
SC_CHUNK = 64


def _sc_scatter_rows(src, dest, n_out):
    n, sub, lanes = src.shape
    kk = dest.shape[0]
    ch = min(SC_CHUNK, n // SC_WORKERS)
    n_chunks = n // (SC_WORKERS * ch)
    idx3 = dest.reshape(kk, n // ch, ch).transpose(1, 0, 2)
    mesh = plsc.VectorSubcoreMesh(core_axis_name="c", subcore_axis_name="s")

    @functools.partial(
        pl.kernel, mesh=mesh,
        out_type=jax.ShapeDtypeStruct((n_out, sub, lanes), src.dtype),
        scratch_types=[pltpu.VMEM((kk, ch), jnp.int32), pltpu.VMEM((ch, sub, lanes), src.dtype),
                       pltpu.SemaphoreType.DMA],
    )
    def scatter_kernel(src_hbm, idx_hbm, out_hbm, idx_v, rows_v, sem):
        wid = lax.axis_index("s") * SC_CORES + lax.axis_index("c")

        @pl.loop(0, n_chunks)
        def _(c):
            j = wid * n_chunks + c
            pltpu.sync_copy(idx_hbm.at[j], idx_v)
            pltpu.sync_copy(src_hbm.at[pl.ds(j * ch, ch)], rows_v)
            copies = [pltpu.async_copy(rows_v, out_hbm.at[idx_v.at[q]], sem) for q in range(kk)]
            for cp in copies:
                cp.wait()

    return scatter_kernel(src, idx3)


def _sc_gather_rows(table, dest):
    v, sub, lanes = table.shape
    kk, n = dest.shape
    total = n * kk
    ch = min(SC_CHUNK, total // SC_WORKERS)
    n_chunks = total // (SC_WORKERS * ch)
    idx2 = dest.reshape(total // ch, ch)
    mesh = plsc.VectorSubcoreMesh(core_axis_name="c", subcore_axis_name="s")

    assert n_chunks % 2 == 0, "the two-buffer gather pipeline walks chunks in pairs"

    @functools.partial(
        pl.kernel, mesh=mesh,
        out_type=jax.ShapeDtypeStruct((total, sub, lanes), table.dtype),
        scratch_types=[pltpu.VMEM((n_chunks, ch), jnp.int32), pltpu.VMEM((2, ch, sub, lanes), table.dtype),
                       pltpu.SemaphoreType.DMA((2,)), pltpu.SemaphoreType.DMA((2,))],
    )
    def gather_kernel(tab_hbm, idx_hbm, out_hbm, idx_v, rows_v, gsem, wsem):
        wid = lax.axis_index("s") * SC_CORES + lax.axis_index("c")
        base = wid * n_chunks
        pltpu.sync_copy(idx_hbm.at[pl.ds(base, n_chunks)], idx_v)

        def gather(c, b):
            return pltpu.make_async_copy(tab_hbm.at[idx_v.at[c]], rows_v.at[b], gsem.at[b])

        def write(c, b):
            return pltpu.make_async_copy(rows_v.at[b], out_hbm.at[pl.ds((base + c) * ch, ch)], wsem.at[b])

        gather(0, 0).start()

        @pl.loop(0, n_chunks, step=2)
        def _(c0):
            for b in range(2):
                c = c0 + b
                gather(c, b).wait()

                @pl.when(c + 1 < n_chunks)
                def _():
                    @pl.when(c >= 1)
                    def _():
                        write(c - 1, 1 - b).wait()

                    gather(c + 1, 1 - b).start()

                write(c, b).start()

        write(n_chunks - 2, 0).wait()
        write(n_chunks - 1, 1).wait()

    return gather_kernel(table, idx2).reshape(kk, n, sub, lanes)


def _ada_kernel(c_ref, w_ref, b_ref, o_ref):
    c = _silu(c_ref[...]).astype(BF16)
    o_ref[0] = _dot(c, w_ref[0].astype(BF16)) + b_ref[0]


def _ada_all(c_all, w_ada, b_ada):
    depth, d, n = w_ada.shape
    rows = c_all.shape[0]
    tn = 1536
    return pl.pallas_call(
        _ada_kernel,
        grid=(depth, n // tn),
        in_specs=[
            pl.BlockSpec((rows, d), lambda l, j: (0, 0)),
            pl.BlockSpec((1, d, tn), lambda l, j: (l, 0, j)),
            pl.BlockSpec((1, 1, tn), lambda l, j: (l, 0, j)),
        ],
        out_specs=pl.BlockSpec((1, rows, tn), lambda l, j: (l, 0, j)),
        out_shape=jax.ShapeDtypeStruct((depth, rows, n), F32),
        compiler_params=_cparams(("parallel", "parallel")),
        name="ada",
    )(c_all, w_ada, b_ada.reshape(depth, 1, n))


def _in_proj_kernel(x_ref, sh_ref, sc_ref, g_ref, w_ref, tie_ref, *refs, dils, fold, n_alias):
    refs = refs[n_alias:]
    s, t, d = x_ref.shape
    rows = s * t
    hb_s = refs[-1]

    @pl.when(pl.program_id(0) == 0)
    def _():
        hb_s[...] = jnp.zeros_like(hb_s)

    hb = hb_s[...]
    n_g = len(dils)
    if fold:
        u_ref = refs[0]
        q_refs = refs[1:1 + 2 * n_g:2] + (refs[1 + 2 * n_g],)
        kv_refs = refs[2:2 + 2 * n_g:2] + (refs[2 + 2 * n_g],)
        gl_ref = refs[3 + 2 * n_g]
        st_refs = refs[4 + 2 * n_g:5 + 3 * n_g]
        zs_ref = refs[5 + 3 * n_g]
    else:
        u_ref = refs[0]
        q_refs = refs[1:2 + n_g]
        kv_refs = (None,) * (n_g + 1)
        gl_ref = refs[2 + n_g]
        st_refs = refs[3 + n_g:4 + 2 * n_g]
        zs_ref = None

    def proj(c0, c1):
        return _dot(hb, w_ref[:, c0:c1])

    u_ref[...] = proj(0, POOL_WIDTH).reshape(s, t, POOL_WIDTH)
    q0 = POOL_WIDTH
    k0 = q0 + n_g * DIL_W
    v0 = k0 + n_g * DIL_W
    c0 = v0 + n_g * DIL_W
    for g, dil in enumerate(tuple(dils) + (1,)):
        if g < n_g:
            qw = DIL_W
            zq = proj(q0 + g * DIL_W, q0 + (g + 1) * DIL_W) * SCALE
            zkv = jnp.concatenate([proj(k0 + g * DIL_W, k0 + (g + 1) * DIL_W),
                                   proj(v0 + g * DIL_W, v0 + (g + 1) * DIL_W)], axis=1)
        else:
            qw = SWA_QW
            z = proj(c0, c0 + SWA_QW + 2 * SWA_KW)
            zq = z[:, :qw] * SCALE
            zkv = z[:, qw:]
        st = st_refs[g]
        st[...] = zkv[rows - st.shape[-1]:, :].T
        if not fold:
            q_refs[g][...] = zq
        elif dil == 1:
            q_refs[g][0, 0] = zq.astype(BF16)
            kv_refs[g][0, 0] = zkv.astype(BF16)
        else:
            zf = jnp.concatenate([zq, zkv], axis=1)
            n_c = zf.shape[1] // BLOCK
            for c in range(n_c):
                zs_ref[c] = zf[:, c * BLOCK:(c + 1) * BLOCK]
            n = rows // dil
            for r in range(dil):
                part = jnp.concatenate([zs_ref[c, pl.ds(r, n, stride=dil), :] for c in range(n_c)], axis=1)
                q_refs[g][0, r] = part[:, :qw].astype(BF16)
                kv_refs[g][0, r] = part[:, qw:].astype(BF16)
    gl_ref[...] = proj(c0 + SWA_QW + 2 * SWA_KW, c0 + SWA_QW + 2 * SWA_KW + 3 * d)
    h = _rms(x_ref[...], g_ref[...]) * (1.0 + sc_ref[...]) + sh_ref[...]
    hb_s[...] = h.reshape(rows, d).astype(BF16)


def _in_proj(x3, ada3, g_pre, w_perm, layer, tm, keeps, fold, tie, prev_states=None):
    nseq, tseq, d = x3.shape
    ntok = nseq * tseq
    s_blk, t_blk, tps = _tile_split(tseq, tm)
    n_tiles = ntok // tm
    in_w = w_perm.shape[-1]
    depth = g_pre.shape[0]
    dils = tuple(dil for _, dil in DIL_GROUPS)
    widths = [(DIL_W, 2 * DIL_W)] * len(dils) + [(SWA_QW, 2 * SWA_KW)]

    def feeds(tile_map):
        return lambda i: tile_map(jnp.minimum(i, n_tiles - 1))

    def drains(tile_map):
        return lambda i: tile_map(jnp.maximum(i - 1, 0))

    def x_tile(i):
        return (i // tps, i % tps, 0)

    def tok_tile(i):
        return (i, 0)

    out_shapes = [jax.ShapeDtypeStruct((nseq, tseq, POOL_WIDTH), F32)]
    out_specs = [pl.BlockSpec((s_blk, t_blk, POOL_WIDTH), drains(x_tile))]
    for (qw, kvw), dil in zip(widths, dils + (1,)):
        if fold:
            for w in (qw, kvw):
                out_shapes.append(jax.ShapeDtypeStruct((nseq, dil, tseq // dil, w), BF16))
                out_specs.append(pl.BlockSpec((1, dil, tm // dil, w), drains(lambda i: (i // tps, 0, i % tps, 0))))
        else:
            out_shapes.append(jax.ShapeDtypeStruct((ntok, qw), F32))
            out_specs.append(pl.BlockSpec((tm, qw), drains(tok_tile)))
    out_shapes.append(jax.ShapeDtypeStruct((ntok, 3 * d), F32))
    out_specs.append(pl.BlockSpec((tm, 3 * d), drains(tok_tile)))
    for keep, (_, kvw) in zip(keeps, widths):
        if fold:
            sb = min(tm, keep)
            bps = keep // sb
            first = (tseq - keep) // tm

            def st_map(i, bps=bps, first=first):
                return (layer, i // tps, 0, jnp.maximum(i % tps - first, 0) * (1 if bps > 1 else 0))

            out_shapes.append(jax.ShapeDtypeStruct((depth, nseq, kvw, keep), F32))
            out_specs.append(pl.BlockSpec((None, None, kvw, sb), drains(st_map)))
        else:
            out_shapes.append(jax.ShapeDtypeStruct((kvw, ntok), F32))
            out_specs.append(pl.BlockSpec((kvw, tm), drains(lambda i: (0, i))))
    scratch = ([pltpu.VMEM((3 * DIL_W // BLOCK, tm, BLOCK), F32)] if fold else []) + [pltpu.VMEM((tm, d), BF16)]
    aliased = list(prev_states) if prev_states is not None else []
    n_fixed = 6
    first_state = len(out_shapes) - len(keeps)
    aliases = {n_fixed + k: first_state + k for k in range(len(aliased))}
    return pl.pallas_call(
        functools.partial(_in_proj_kernel, dils=dils, fold=fold, n_alias=len(aliased)),
        grid=(n_tiles + 1,),
        in_specs=[
            pl.BlockSpec((s_blk, t_blk, d), feeds(x_tile)),
            pl.BlockSpec((s_blk, 1, d), feeds(lambda i: (i // tps, 0, 0))),
            pl.BlockSpec((s_blk, 1, d), feeds(lambda i: (i // tps, 0, 1))),
            pl.BlockSpec((1, 1, d), lambda i: (layer, 0, 0)),
            pl.BlockSpec((d, in_w), lambda i: (0, 0)),
            _TIE_SPEC,
        ] + [pl.BlockSpec(memory_space=pl.ANY)] * len(aliased),
        out_specs=out_specs,
        out_shape=out_shapes,
        input_output_aliases=aliases,
        scratch_shapes=scratch,
        compiler_params=_cparams(("arbitrary",)),
        name="in_proj",
    )(x3, ada3, ada3, g_pre, w_perm, tie, *aliased)


PAIR = 2 * HEAD_DIM


def _band_attn_kernel(*refs, n_heads, n_kv, nq, with_sink):
    if with_sink:
        sink_ref, q_ref, kv_ref, halo_ref, bm_ref, mask_ref, o_ref = refs
        lse_ref = None
    else:
        q_ref, kv_ref, halo_ref, bm_ref, mask_ref, o_ref, lse_ref = refs
        sink_ref = None
    kw = n_kv * HEAD_DIM
    grp = n_heads // n_kv
    chunk = pl.program_id(2)
    lo, hi = mask_ref[0], mask_ref[1]
    low_lanes = lax.broadcasted_iota(jnp.int32, (BLOCK, PAIR), 1) < HEAD_DIM
    col = lax.broadcasted_iota(jnp.int32, (BLOCK, 4 * BLOCK), 1)
    prev_penalty = jnp.where(col % (2 * BLOCK) < BLOCK, NEG_INF, 0.0)

    def swap_halves(x):
        return jnp.concatenate([x[:, HEAD_DIM:], x[:, :HEAD_DIM]], axis=1)

    def pair_sources(kv_blk, p):
        if grp == 1:
            k = kv_blk[:, p * PAIR:(p + 1) * PAIR]
            v = kv_blk[:, kw + p * PAIR:kw + (p + 1) * PAIR]
            return (k, k), (v, v)
        kh = (2 * p) // grp
        c0 = (kh // 2) * PAIR
        k = kv_blk[:, c0:c0 + PAIR]
        v = kv_blk[:, kw + c0:kw + c0 + PAIR]
        ks, vs = swap_halves(k), swap_halves(v)
        return ((k, ks), (v, vs)) if kh % 2 == 0 else ((ks, k), (vs, v))

    def one_block(r0, kv_prev, first):
        qb = q_ref[pl.ds(r0, BLOCK), :]
        kv_cur = kv_ref[pl.ds(r0, BLOCK), :]
        for p in range(n_heads // 2):
            (kpe, kpo), (vpe, vpo) = pair_sources(kv_prev, p)
            (kce, kco), (vce, vco) = pair_sources(kv_cur, p)
            k_blk = jnp.concatenate([kpe * lo, kce * lo, kpo * hi, kco * hi], axis=0)
            v_blk = jnp.concatenate([
                jnp.concatenate([vpe * lo, lo], axis=1), jnp.concatenate([vce * lo, lo], axis=1),
                jnp.concatenate([vpo * hi, hi], axis=1), jnp.concatenate([vco * hi, hi], axis=1)], axis=0)
            s = _dot_nt(qb[:, p * PAIR:(p + 1) * PAIR], k_blk) + bm_ref[p]
            if first is not None:
                s = s + jnp.where(first, prev_penalty, 0.0)
            s_e, s_o = s[:, :2 * BLOCK], s[:, 2 * BLOCK:]
            m_e = jnp.max(jnp.maximum(s_e[:, :BLOCK], s_e[:, BLOCK:]), axis=-1, keepdims=True)
            m_o = jnp.max(jnp.maximum(s_o[:, :BLOCK], s_o[:, BLOCK:]), axis=-1, keepdims=True)
            if with_sink:
                m_e = jnp.maximum(m_e, sink_ref[2 * p])
                m_o = jnp.maximum(m_o, sink_ref[2 * p + 1])
            pr = jnp.concatenate([jnp.exp(s_e - m_e), jnp.exp(s_o - m_o)], axis=1).astype(BF16)
            res = _dot(pr, v_blk)
            den = res[:, PAIR:]
            if with_sink:
                den = den + jnp.where(low_lanes, jnp.exp(sink_ref[2 * p] - m_e), jnp.exp(sink_ref[2 * p + 1] - m_o))
            o_ref[pl.ds(r0, BLOCK), p * PAIR:(p + 1) * PAIR] = res[:, :PAIR] / den
            if lse_ref is not None:
                lse_ref[pl.ds(r0, BLOCK), p * PAIR:(p + 1) * PAIR] = jnp.where(low_lanes, m_e, m_o) + jnp.log(den)

    one_block(0, halo_ref[...], chunk == 0)

    def body(j, carry):
        r0 = pl.multiple_of(j * BLOCK, BLOCK)
        one_block(r0, kv_ref[pl.ds(r0 - BLOCK, BLOCK), :], None)
        return carry

    if nq > 1:
        lax.fori_loop(1, nq, body, 0)


def _band_attn(q, kv, bm, n_heads, n_kv, sinks=None):
    batch, dil, fold, qw = q.shape
    kvw = kv.shape[-1]
    nq = min(8, fold // BLOCK)
    rows = nq * BLOCK
    n_chunks = fold // rows
    with_sink = sinks is not None
    lanes_low = np.arange(PAIR) < HEAD_DIM
    masks = jnp.asarray(np.broadcast_to(np.stack([lanes_low, ~lanes_low])[:, None, :], (2, BLOCK, PAIR)), F32).astype(BF16)
    in_specs = [
        pl.BlockSpec((None, None, rows, qw), lambda b, r, c: (b, r, c, 0)),
        pl.BlockSpec((None, None, rows, kvw), lambda b, r, c: (b, r, c, 0)),
        pl.BlockSpec((None, None, BLOCK, kvw), lambda b, r, c: (b, r, jnp.maximum(c * nq - 1, 0), 0)),
        pl.BlockSpec(bm.shape, lambda b, r, c: (0, 0, 0)),
        pl.BlockSpec(masks.shape, lambda b, r, c: (0, 0, 0)),
    ]
    args = [q, kv, kv, bm, masks]
    o_spec = pl.BlockSpec((None, None, rows, qw), lambda b, r, c: (b, r, c, 0))
    o_shape = jax.ShapeDtypeStruct((batch, dil, fold, qw), F32)
    if with_sink:
        in_specs = [pl.BlockSpec(memory_space=pltpu.SMEM)] + in_specs
        args = [sinks] + args
        out_specs, out_shape = o_spec, o_shape
    else:
        out_specs, out_shape = [o_spec, o_spec], [o_shape, o_shape]
    return pl.pallas_call(
        functools.partial(_band_attn_kernel, n_heads=n_heads, n_kv=n_kv, nq=nq, with_sink=with_sink),
        grid=(batch, dil, n_chunks),
        in_specs=in_specs,
        out_specs=out_specs,
        out_shape=out_shape,
        compiler_params=_cparams(("parallel", "parallel", "parallel")),
        name="band_attn",
    )(*args)


SAMPLE_SEQ_UNROLL = 4


def _sample_attn_kernel(*refs, n_heads, n_kv, sb, t_new, with_sink, aliased):
    refs = list(refs)
    q_ref, newt_ref, cache_ref, bmc_ref, bmn_ref, tie_ref = refs[:6]
    refs = refs[6:]
    sink_ref = refs.pop(0) if with_sink else None
    if aliased:
        refs.pop(0)
    o_ref = refs.pop(0)
    lse_ref = None if with_sink else refs.pop(0)
    cout_ref = refs.pop(0)
    kw = n_kv * HEAD_DIM
    grp = n_heads // n_kv
    w, n_buf = cache_ref.shape[2:]
    first_new = BLOCK - t_new
    per_blk = BLOCK // t_new
    i = pl.program_id(0)
    lane = lax.broadcasted_iota(jnp.int32, (w, BLOCK), 1)

    def one_seq(s, carry):
        r0 = s * t_new if isinstance(s, int) else pl.multiple_of(s * t_new, t_new)
        qs = q_ref[pl.ds(r0, t_new), :]
        q_rows = []
        for h in range(n_heads):
            piece = qs[:, h * HEAD_DIM:(h + 1) * HEAD_DIM]
            parts = [piece if k == h // grp else jnp.zeros_like(piece) for k in range(n_kv)]
            q_rows.append(jnp.concatenate(parts, axis=1))
        qbd = jnp.concatenate(q_rows, axis=0).astype(BF16)
        cache = cache_ref[0, s]
        off = ((i * sb + s) % per_blk) * t_new
        placed = pltpu.roll(newt_ref[...], first_new - off, axis=1)
        s_c = _dot(qbd, cache[:kw].astype(BF16)) + bmc_ref[...]
        s_n = _dot(qbd, placed[:kw].astype(BF16)) + bmn_ref[...]
        m = jnp.maximum(jnp.max(s_c, axis=-1, keepdims=True), jnp.max(s_n, axis=-1, keepdims=True))
        if with_sink:
            m = jnp.maximum(m, sink_ref[:, :1])
        p_c = jnp.exp(s_c - m)
        p_n = jnp.exp(s_n - m)
        den = jnp.sum(p_c, axis=-1, keepdims=True) + jnp.sum(p_n, axis=-1, keepdims=True)
        if with_sink:
            den = den + jnp.exp(sink_ref[:, :1] - m)
        o_all = (_dot_nt(p_c.astype(BF16), cache[kw:].astype(BF16))
                 + _dot_nt(p_n.astype(BF16), placed[kw:].astype(BF16))) / den
        outs = []
        for h in range(n_heads):
            kh = h // grp
            outs.append(o_all[h * t_new:(h + 1) * t_new, kh * HEAD_DIM:(kh + 1) * HEAD_DIM])
        o_ref[pl.ds(r0, t_new), :] = jnp.concatenate(outs, axis=1)
        if lse_ref is not None:
            lse = m + jnp.log(den)
            lse_ref[pl.ds(r0, t_new), :] = jnp.concatenate(
                [jnp.broadcast_to(lse[h * t_new:(h + 1) * t_new], (t_new, HEAD_DIM)) for h in range(n_heads)], axis=1)
        rolled = pltpu.roll(cache, n_buf - t_new, axis=1)
        if n_buf > BLOCK:
            cout_ref[0, s, :, : n_buf - BLOCK] = rolled[:, : n_buf - BLOCK]
        cout_ref[0, s, :, n_buf - BLOCK:] = jnp.where(lane >= first_new, placed, rolled[:, n_buf - BLOCK:])
        return carry

    group = min(sb, SAMPLE_SEQ_UNROLL)
    if sb == group:
        for s in range(sb):
            one_seq(s, 0)
    else:
        def body(it, carry):
            for u in range(group):
                one_seq(it * group + u, carry)
            return carry

        lax.fori_loop(0, sb // group, body, 0)


def _sample_attn(q, newt, cache_t, prev_out, bmc, bmn, layer, n_heads, n_kv, t_new, tie, sink_rows=None):
    depth, nseq, w, n_buf = cache_t.shape
    qw = n_heads * HEAD_DIM
    sb = max(1, min(BLOCK // t_new, 2048 // n_buf))
    with_sink = sink_rows is not None
    aliased = prev_out is not None
    rows = sb * t_new
    per_blk = BLOCK // t_new
    in_specs = [
        pl.BlockSpec((rows, qw), lambda i: (i, 0)),
        pl.BlockSpec((w, BLOCK), lambda i: (0, (i * sb) // per_blk)),
        pl.BlockSpec((1, sb, w, n_buf), lambda i: (layer, i, 0, 0)),
        pl.BlockSpec(bmc.shape, lambda i: (0, 0)),
        pl.BlockSpec(bmn.shape, lambda i: (0, 0)),
        _TIE_SPEC,
    ]
    args = [q, newt, cache_t, bmc, bmn, tie]
    if with_sink:
        in_specs.append(pl.BlockSpec(sink_rows.shape, lambda i: (0, 0)))
        args.append(sink_rows)
    aliases = {}
    if aliased:
        aliases = {len(args): 1 if with_sink else 2}
        in_specs.append(pl.BlockSpec(memory_space=pl.ANY))
        args.append(prev_out)
    o_spec = pl.BlockSpec((rows, qw), lambda i: (i, 0))
    o_shape = jax.ShapeDtypeStruct((nseq * t_new, qw), F32)
    c_spec = pl.BlockSpec((1, sb, w, n_buf), lambda i: (layer, i, 0, 0))
    c_shape = jax.ShapeDtypeStruct(cache_t.shape, F32)
    if with_sink:
        out_specs, out_shape = [o_spec, c_spec], [o_shape, c_shape]
    else:
        out_specs, out_shape = [o_spec, o_spec, c_spec], [o_shape, o_shape, c_shape]
    return pl.pallas_call(
        functools.partial(_sample_attn_kernel, n_heads=n_heads, n_kv=n_kv, sb=sb, t_new=t_new,
                          with_sink=with_sink, aliased=aliased),
        grid=(nseq // sb,),
        in_specs=in_specs,
        out_specs=out_specs,
        out_shape=out_shape,
        input_output_aliases=aliases,
        compiler_params=_cparams(("parallel",)),
        name="sample_attn",
    )(*args)


def _post_kernel(x_ref, gt_ref, shf_ref, scf_ref, g2_ref, g3_ref, u_ref, halo_ref,
                 o1_ref, l1_ref, o2_ref, l2_ref, o3_ref, l3_ref, oc_ref, gl_ref,
                 pw_ref, ps_ref, wb_ref, wc_ref, wo_ref, wrt_ref, rb_ref, tie_ref,
                 xmid_ref, h2_ref, idx_ref, rank_ref, wk_ref, counts_ref, wfull_ref, cnt_ref, *unfold_refs,
                 tps, full_windows, dils):
    s, t, d = x_ref.shape
    rows = s * t
    i = pl.program_id(0)

    u = u_ref[...]
    halo = halo_ref[...]
    if not full_windows:
        halo = jnp.where(i % tps == 0, 0.0, halo)
    ue = jnp.concatenate([halo, u], axis=1)
    if full_windows:
        row = None
    else:
        row = (i % tps) * t + lax.broadcasted_iota(jnp.int32, (1, t, 1), 1)
    parts = []
    for g, win in enumerate(POOL_WINDOWS):
        cs = slice(g * POOL_CH, (g + 1) * POOL_CH)
        acc = ue[:, :, cs]
        base = 0
        span = 1
        while span < win:
            acc = acc[:, span:, :] + acc[:, : acc.shape[1] - span, :]
            base += span
            span *= 2
        tot = acc[:, POOL_HALO - base:, :]
        if full_windows:
            mean = tot / float(win)
        else:
            cnt = jnp.minimum(row + 1, win).astype(F32)
            mean = tot / cnt
        zg = (mean - u[:, :, cs]).reshape(rows, POOL_CH).astype(BF16)
        parts.append(_dot(zg, pw_ref[0, g]))
    a = jnp.concatenate(parts, axis=-1) * ps_ref[0]

    scratch = list(unfold_refs)

    def token_order(ref, dil):
        if dil == 1:
            return ref[...]
        scr = scratch.pop(0)
        n = rows // dil
        n_c = scr.shape[0]
        for r in range(dil):
            part = ref[0, r]
            for c in range(n_c):
                scr[c, pl.ds(r, n, stride=dil), :] = part[:, c * BLOCK:(c + 1) * BLOCK]
        return jnp.concatenate([scr[c] for c in range(n_c)], axis=1)

    o1, l1 = token_order(o1_ref, dils[0]), token_order(l1_ref, dils[0])
    o2, l2 = token_order(o2_ref, dils[1]), token_order(l2_ref, dils[1])
    o3, l3 = token_order(o3_ref, dils[2]), token_order(l3_ref, dils[2])

    lm = jnp.maximum(jnp.maximum(l1, l2), l3)
    e1, e2, e3 = jnp.exp(l1 - lm), jnp.exp(l2 - lm), jnp.exp(l3 - lm)
    esum = e1 + e2 + e3
    bmix = (e1 / esum) * o1 + (e2 / esum) * o2 + (e3 / esum) * o3
    b = _dot(bmix.astype(BF16), wb_ref[...])
    c = _dot(oc_ref[...].astype(BF16), wc_ref[...])
    g_a = _sigmoid(gl_ref[:, :d])
    g_b = _sigmoid(gl_ref[:, d:2 * d])
    g_c = _sigmoid(gl_ref[:, 2 * d:])
    mix = _dot((g_a * a + g_b * b + g_c * c).astype(BF16), wo_ref[...])

    x = x_ref[...]
    xm = x + gt_ref[...] * _rms(mix, g2_ref[0]).reshape(s, t, d)
    xmid_ref[...] = xm
    h2f = (_rms(xm, g3_ref[...]) * (1.0 + scf_ref[...]) + shf_ref[...]).reshape(rows, d)
    h2 = h2f.astype(BF16)
    _store_rows(h2_ref, _pack_chunks(h2f), rows)

    scores = _sigmoid(_dot_nt(wrt_ref[...], h2))
    work = scores + rb_ref[0]
    e_iota = lax.broadcasted_iota(jnp.int32, (N_EXPERTS, rows), 0).astype(F32)
    sel = jnp.zeros((N_EXPERTS, rows), F32)
    picks = []
    for _ in range(TOP_K):
        mx = jnp.max(work, axis=0, keepdims=True)
        pick = jnp.min(jnp.where(work == mx, e_iota, float(N_EXPERTS)), axis=0, keepdims=True)
        hit = e_iota == pick
        sel = jnp.where(hit, 1.0, sel)
        work = jnp.where(hit, NEG_INF, work)
        picks.append(pick)
    top_s = scores * sel
    wmat = top_s / jnp.sum(top_s, axis=0, keepdims=True) * ROUTED_SCALE

    @pl.when(i == 0)
    def _():
        cnt_ref[...] = jnp.zeros_like(cnt_ref)

    r_i = lax.broadcasted_iota(jnp.int32, (rows, rows), 0)
    c_i = lax.broadcasted_iota(jnp.int32, (rows, rows), 1)
    before = jnp.where(r_i < c_i, 1.0, 0.0).astype(BF16)
    rank_all = cnt_ref[...] + _dot(sel.astype(BF16), before)
    cnt_ref[...] = cnt_ref[...] + jnp.sum(sel, axis=1, keepdims=True)
    counts_ref[...] = cnt_ref[...].astype(jnp.int32)

    krow = lax.broadcasted_iota(jnp.int32, (TOP_K, rows), 0)
    idx = jnp.zeros((TOP_K, rows), F32)
    rank = jnp.zeros((TOP_K, rows), F32)
    wk = jnp.zeros((TOP_K, rows), F32)
    for k, pick in enumerate(picks):
        hit = e_iota == pick
        idx = jnp.where(krow == k, pick, idx)
        rank = jnp.where(krow == k, jnp.sum(jnp.where(hit, rank_all, 0.0), axis=0, keepdims=True), rank)
        wk = jnp.where(krow == k, jnp.sum(jnp.where(hit, wmat, 0.0), axis=0, keepdims=True), wk)
    idx_ref[...] = idx.astype(jnp.int32)
    rank_ref[...] = rank.astype(jnp.int32)
    eye = jnp.where(r_i == c_i, 1.0, 0.0).astype(BF16)

    def token_major(w):
        hi = w.astype(BF16)
        rest = w - hi.astype(F32)
        mid = rest.astype(BF16)
        lo = (rest - mid.astype(F32)).astype(BF16)
        return _dot_nt(eye, hi) + _dot_nt(eye, mid) + _dot_nt(eye, lo)

    wk_ref[...] = token_major(wk)
    wfull_ref[...] = token_major(wmat)


def _post_mixer(x3, ada3, u3, halo3, attn, gl, weights, layer, tm, full_windows, dils, tie):
    nseq, tseq, d = x3.shape
    ntok = nseq * tseq
    s_blk, t_blk, tps = _tile_split(tseq, tm)
    n_tiles = ntok // tm
    (g2, g3, pool_w, pool_scale, w_b_up, w_c_up, w_out, w_router, router_bias) = weights

    def x_map(i):
        return (i // tps, i % tps, 0)

    def tok_map(i):
        return (i, 0)

    def ada_spec(j):
        return pl.BlockSpec((s_blk, 1, d), lambda i: (i // tps, 0, j))

    if full_windows:
        halo_spec = pl.BlockSpec((s_blk, POOL_HALO, POOL_WIDTH), lambda i: (i, 0, 0))
    else:
        hb = t_blk // POOL_HALO
        halo_spec = pl.BlockSpec((1, POOL_HALO, POOL_WIDTH),
                                 lambda i: (i // tps, jnp.maximum((i % tps) * hb - 1, 0), 0))
    vec = pl.BlockSpec((1, 1, d), lambda i: (layer, 0, 0))

    def per_layer(w):
        return pl.BlockSpec((None,) + w.shape[1:], lambda i: (layer, 0, 0))

    attn_specs = []
    for g, dil in enumerate(dils):
        if dil == 1:
            spec = pl.BlockSpec((tm, DIL_W), tok_map)
        else:
            spec = pl.BlockSpec((1, dil, tm // dil, DIL_W), lambda i: (i // tps, 0, i % tps, 0))
        attn_specs += [spec, spec]
    in_specs = [
        pl.BlockSpec((s_blk, t_blk, d), x_map), ada_spec(2), ada_spec(3), ada_spec(4), vec, vec,
        pl.BlockSpec((s_blk, t_blk, POOL_WIDTH), x_map), halo_spec,
    ] + attn_specs + [
        pl.BlockSpec((tm, SWA_QW), tok_map), pl.BlockSpec((tm, 3 * d), tok_map),
        pl.BlockSpec((1,) + pool_w.shape[1:], lambda i: (layer, 0, 0, 0)),
        vec,
        per_layer(w_b_up), per_layer(w_c_up), per_layer(w_out), per_layer(w_router),
        pl.BlockSpec((1, N_EXPERTS, 1), lambda i: (layer, 0, 0)),
        _TIE_SPEC,
    ]
    out_shape = [
        jax.ShapeDtypeStruct((nseq, tseq, d), F32),
        jax.ShapeDtypeStruct((ntok * ROW_SUB, ROW_LANES), ROW_DTYPE),
        jax.ShapeDtypeStruct((TOP_K, ntok), jnp.int32),
        jax.ShapeDtypeStruct((TOP_K, ntok), jnp.int32),
        jax.ShapeDtypeStruct((ntok, TOP_K), F32),
        jax.ShapeDtypeStruct((N_EXPERTS, 1), jnp.int32),
        jax.ShapeDtypeStruct((ntok, N_EXPERTS), F32),
    ]
    out_specs = [
        pl.BlockSpec((s_blk, t_blk, d), x_map),
        pl.BlockSpec((tm * ROW_SUB, ROW_LANES), tok_map),
        pl.BlockSpec((TOP_K, tm), lambda i: (0, i)),
        pl.BlockSpec((TOP_K, tm), lambda i: (0, i)),
        pl.BlockSpec((tm, TOP_K), tok_map),
        pl.BlockSpec((N_EXPERTS, 1), lambda i: (0, 0)),
        pl.BlockSpec((tm, N_EXPERTS), tok_map),
    ]
    n_unfold = 2 * sum(1 for dil in dils if dil > 1)
    scratch = [pltpu.VMEM((N_EXPERTS, 1), F32)] + [pltpu.VMEM((DIL_W // BLOCK, tm, BLOCK), F32)] * n_unfold
    return pl.pallas_call(
        functools.partial(_post_kernel, tps=tps, full_windows=full_windows, dils=tuple(dils)),
        grid=(n_tiles,),
        in_specs=in_specs,
        out_specs=out_specs,
        out_shape=out_shape,
        scratch_shapes=scratch,
        compiler_params=_cparams(("arbitrary",)),
        name="post_mixer",
    )(x3, ada3, ada3, ada3, g2, g3, u3, halo3, *attn, gl,
      pool_w, pool_scale, w_b_up, w_c_up, w_out, w_router, router_bias, tie)


EXPERT_ROWS = 256


def _expert_kernel(bexp_ref, nvalid_ref, nused_ref, x_ref, wg_ref, wu_ref, wd_ref, tie_ref, y_ref, wg_s, wu_s, wd_s, h_s):
    i = pl.program_id(0)
    bm = x_ref.shape[0] // ROW_SUB
    sub = min(bm, EXPERT_ROWS)
    last = bexp_ref.shape[0] - 1
    up_blk = jnp.minimum(i, last)
    dn_blk = jnp.maximum(i - 1, 0)

    @pl.when((i == 0) | (bexp_ref[up_blk] != bexp_ref[jnp.maximum(up_blk - 1, 0)]))
    def _():
        wg_s[...] = wg_ref[0, 0].astype(BF16)
        wu_s[...] = wu_ref[0, 0].astype(BF16)

    @pl.when((i <= 1) | (bexp_ref[dn_blk] != bexp_ref[jnp.maximum(dn_blk - 1, 0)]))
    def _():
        wd_s[...] = wd_ref[0, 0].astype(BF16)

    @pl.when(i == 0)
    def _():
        h_s[...] = jnp.zeros_like(h_s)

    @pl.when(i <= nused_ref[0])
    def _():
        for c in range(bm // sub):
            y = _dot(h_s[c * sub:(c + 1) * sub, :], wd_s[...])
            _store_rows(y_ref, _pack_chunks(y), sub, first=c * sub)
        for c in range(bm // sub):
            x = _unpack_chunks(_load_rows(x_ref, sub, first=c * sub))
            row = c * sub + lax.broadcasted_iota(jnp.int32, (sub, 1), 0)
            x = jnp.where(row < nvalid_ref[up_blk], x, 0.0).astype(BF16)
            gate = _dot(x, wg_s[...])
            up = _dot(x, wu_s[...])
            h_s[c * sub:(c + 1) * sub, :] = (_silu(gate) * up).astype(BF16)

    @pl.when(i > nused_ref[0])
    def _():
        y_ref[...] = jnp.zeros_like(y_ref)


def _experts(x_rows, block_exp, nvalid, n_used, we_gate, we_up, we_down, layer, bm, tie):
    n_rows = x_rows.shape[0] // ROW_SUB
    d, ff = we_gate.shape[-2:]
    n_blocks = n_rows // bm
    last = n_blocks - 1

    def up_map(i, be, nv, nu):
        return (layer, be[jnp.minimum(i, last)], 0, 0)

    def down_map(i, be, nv, nu):
        return (layer, be[jnp.maximum(i - 1, 0)], 0, 0)

    grid_spec = pltpu.PrefetchScalarGridSpec(
        num_scalar_prefetch=3,
        grid=(n_blocks + 1,),
        in_specs=[
            pl.BlockSpec((bm * ROW_SUB, ROW_LANES), lambda i, be, nv, nu: (jnp.minimum(i, last), 0)),
            pl.BlockSpec((1, 1, d, ff), up_map),
            pl.BlockSpec((1, 1, d, ff), up_map),
            pl.BlockSpec((1, 1, ff, d), down_map),
            _TIE_SPEC,
        ],
        out_specs=pl.BlockSpec((bm * ROW_SUB, ROW_LANES), lambda i, be, nv, nu: (jnp.maximum(i - 1, 0), 0)),
        scratch_shapes=[pltpu.VMEM((d, ff), BF16), pltpu.VMEM((d, ff), BF16), pltpu.VMEM((ff, d), BF16),
                        pltpu.VMEM((bm, ff), BF16)],
    )
    return pl.pallas_call(
        _expert_kernel,
        grid_spec=grid_spec,
        out_shape=jax.ShapeDtypeStruct(x_rows.shape, x_rows.dtype),
        compiler_params=_cparams(("arbitrary",)),
        name="experts",
    )(block_exp, nvalid, n_used, x_rows, we_gate, we_up, we_down, tie)


def _final_kernel(x_ref, gt_ref, g4_ref, h2_ref, tie_ref, *refs, dense):
    s, t, d = x_ref.shape
    rows = s * t
    if dense:
        yr_ref, wg_ref, wu_ref, wd_ref, o_ref = refs
    else:
        yk_ref, wk_ref, wg_ref, wu_ref, wd_ref, o_ref = refs
    h2 = _unpack_chunks(_load_rows(h2_ref, rows)).astype(BF16)
    f = _dot((_silu(_dot(h2, wg_ref[...])) * _dot(h2, wu_ref[...])).astype(BF16), wd_ref[...])
    if dense:
        f = f + yr_ref[...]
    else:
        wk = wk_ref[...]
        for k in range(TOP_K):
            f = f + wk[:, k:k + 1] * _unpack_chunks(_load_rows(yk_ref, rows, lead=k))
    o_ref[...] = x_ref[...] + gt_ref[...] * _rms(f, g4_ref[0]).reshape(s, t, d)


def _final(xmid3, ada3, g4, h2_rows, routed, ws_gate, ws_up, ws_down, layer, tm, tie):
    nseq, tseq, d = xmid3.shape
    ntok = nseq * tseq
    s_blk, t_blk, tps = _tile_split(tseq, tm)
    dense = not isinstance(routed, tuple)

    def x_map(i):
        return (i // tps, i % tps, 0)

    if dense:
        routed_specs = [pl.BlockSpec((tm, d), lambda i: (i, 0))]
        routed = (routed,)
    else:
        routed_specs = [pl.BlockSpec((TOP_K, tm * ROW_SUB, ROW_LANES), lambda i: (0, i, 0)),
                        pl.BlockSpec((tm, TOP_K), lambda i: (i, 0))]
    return pl.pallas_call(
        functools.partial(_final_kernel, dense=dense),
        grid=(ntok // tm,),
        in_specs=[
            pl.BlockSpec((s_blk, t_blk, d), x_map),
            pl.BlockSpec((s_blk, 1, d), lambda i: (i // tps, 0, 5)),
            pl.BlockSpec((1, 1, d), lambda i: (layer, 0, 0)),
            pl.BlockSpec((tm * ROW_SUB, ROW_LANES), lambda i: (i, 0)),
            _TIE_SPEC,
        ] + routed_specs + [
            pl.BlockSpec((None,) + ws_gate.shape[1:], lambda i: (layer, 0, 0)),
            pl.BlockSpec((None,) + ws_up.shape[1:], lambda i: (layer, 0, 0)),
            pl.BlockSpec((None,) + ws_down.shape[1:], lambda i: (layer, 0, 0)),
        ],
        out_specs=pl.BlockSpec((s_blk, t_blk, d), x_map),
        out_shape=jax.ShapeDtypeStruct((nseq, tseq, d), F32),
        compiler_params=_cparams(("parallel",)),
        name="final",
    )(xmid3, ada3, g4, h2_rows, tie, *routed, ws_gate, ws_up, ws_down)


DENSE_EXPERTS_PER_STEP = 2


def _dense_moe_kernel(h2_ref, wt_ref, wg_ref, wu_ref, wd_ref, y_ref, x_s):
    step = pl.program_id(0)
    rows = y_ref.shape[0]
    n_e = wg_ref.shape[1]

    @pl.when(step == 0)
    def _():
        x_s[...] = _unpack_chunks(_load_rows(h2_ref, rows)).astype(BF16)
        y_ref[...] = jnp.zeros_like(y_ref)

    x = x_s[...]
    wt = wt_ref[...]
    lane = lax.broadcasted_iota(jnp.int32, wt.shape, 1)
    acts, downs = [], []
    for j in range(n_e):
        w_col = jnp.sum(jnp.where(lane == step * n_e + j, wt, 0.0), axis=1, keepdims=True)
        gate = _dot(x, wg_ref[0, j].astype(BF16))
        up = _dot(x, wu_ref[0, j].astype(BF16))
        acts.append((w_col * (_silu(gate) * up)).astype(BF16))
        downs.append(wd_ref[0, j].astype(BF16))
    y_ref[...] += _dot(jnp.concatenate(acts, axis=1), jnp.concatenate(downs, axis=0))


def _dense_moe(h2_rows, wfull_t, we_gate, we_up, we_down, layer):
    ntok, n_exp = wfull_t.shape
    d, ff = we_gate.shape[-2:]
    n_e = DENSE_EXPERTS_PER_STEP
    return pl.pallas_call(
        _dense_moe_kernel,
        grid=(n_exp // n_e,),
        in_specs=[
            pl.BlockSpec(h2_rows.shape, lambda e: (0, 0)),
            pl.BlockSpec((ntok, n_exp), lambda e: (0, 0)),
            pl.BlockSpec((1, n_e, d, ff), lambda e: (layer, e, 0, 0)),
            pl.BlockSpec((1, n_e, d, ff), lambda e: (layer, e, 0, 0)),
            pl.BlockSpec((1, n_e, ff, d), lambda e: (layer, e, 0, 0)),
        ],
        out_specs=pl.BlockSpec((ntok, d), lambda e: (0, 0)),
        out_shape=jax.ShapeDtypeStruct((ntok, d), F32),
        scratch_shapes=[pltpu.VMEM((ntok, d), BF16)],
        compiler_params=_cparams(("arbitrary",)),
        name="dense_moe",
    )(h2_rows, wfull_t, we_gate, we_up, we_down)


def _t5_bucket(dist):
    n = np.asarray(dist, dtype=np.int64)
    exact = N_BUCKETS // 2
    log_ratio = np.log(np.maximum(n, 1) / exact) / np.log(MAX_DISTANCE / exact)
    large = np.minimum(exact + (log_ratio * (N_BUCKETS - exact)).astype(np.int64), N_BUCKETS - 1)
    return np.where(n < exact, n, large).astype(np.int32)


def _bias_table(rel_bias, h0, h1, steps, dil):
    steps = np.asarray(steps)
    buckets = np.where(steps >= 0, _t5_bucket(np.maximum(steps, 0) * dil), -1)
    onehot = buckets[..., None] == np.arange(N_BUCKETS)
    table = rel_bias[:, h0:h1].T.astype(F32).reshape((h1 - h0,) + (1,) * steps.ndim + (N_BUCKETS,))
    val = jnp.sum(jnp.where(onehot[None], table, 0.0), axis=-1)
    return jnp.where((steps >= 0)[None], val, NEG_INF)


def _band_steps():
    dist = BLOCK + np.arange(BLOCK)[:, None] - np.arange(2 * BLOCK)[None, :]
    return np.where((dist >= 0) & (dist <= BLOCK), dist, -1)


def _sample_steps(n_buf, dil, t_new):
    t = np.arange(t_new)[:, None]
    delta_c = n_buf + t - np.arange(n_buf)[None, :]
    ok_c = (delta_c % dil == 0) & (delta_c // dil <= BLOCK)
    lane = np.arange(BLOCK)[None, :]
    delta_n = t - (lane - (BLOCK - t_new))
    ok_n = (lane >= BLOCK - t_new) & (delta_n >= 0) & (delta_n % dil == 0) & (delta_n // dil <= BLOCK)
    return np.where(ok_c, delta_c // dil, -1), np.where(ok_n, delta_n // dil, -1)


def _route_plan(idx, rank, counts, bm):
    n_tok = idx.shape[1]
    n_blocks = -(-(n_tok * TOP_K + N_EXPERTS * (bm - 1)) // bm)
    counts = counts.reshape(N_EXPERTS)
    padded = (counts + bm - 1) // bm * bm
    pad_end = jnp.cumsum(padded)
    pad_start = pad_end - padded
    experts = jnp.arange(N_EXPERTS, dtype=jnp.int32)
    onehot = idx[:, :, None] == experts[None, None, :]
    dest = rank + jnp.sum(jnp.where(onehot, pad_start[None, None, :], 0), axis=-1)
    starts = jnp.arange(n_blocks, dtype=jnp.int32) * bm
    block_exp = jnp.minimum(jnp.sum((starts[:, None] >= pad_end[None, :]).astype(jnp.int32), axis=1), N_EXPERTS - 1)
    hot = block_exp[:, None] == experts[None, :]
    blk_cnt = jnp.sum(jnp.where(hot, counts[None, :], 0), axis=1)
    blk_start = jnp.sum(jnp.where(hot, pad_start[None, :], 0), axis=1)
    nvalid = jnp.clip(blk_cnt - (starts - blk_start), 0, bm).astype(jnp.int32)
    n_used = (pad_end[-1] // bm).astype(jnp.int32).reshape(1)
    return dest.astype(jnp.int32), block_exp.astype(jnp.int32), nvalid, n_used, n_blocks * bm


def _moe_routed(h2_rows, idx, rank, counts, we_gate, we_up, we_down, layer, bm, tie):
    n_tok = idx.shape[1]
    dest, block_exp, nvalid, n_used, n_rows = _route_plan(idx, rank, counts, bm)
    x_sorted = _sc_scatter_rows(h2_rows.reshape(n_tok, ROW_SUB, ROW_LANES), dest, n_rows)
    y_sorted = _experts(x_sorted.reshape(n_rows * ROW_SUB, ROW_LANES), block_exp, nvalid, n_used,
                        we_gate, we_up, we_down, layer, bm, tie)
    yk = _sc_gather_rows(y_sorted.reshape(n_rows, ROW_SUB, ROW_LANES), dest)
    return yk.reshape(TOP_K, n_tok * ROW_SUB, ROW_LANES), y_sorted


def _to_time_minor(c):
    depth, nseq, n_buf = c.shape[:3]
    return jnp.transpose(c, (0, 1, 3, 4, 5, 2)).reshape(depth, nseq, -1, n_buf)


def _from_time_minor(ct, n_heads):
    lead = ct.shape[:-2]
    rows = ct.shape[-1]
    nl = len(lead)
    x = ct.reshape(lead + (2, n_heads, HEAD_DIM, rows))
    return jnp.transpose(x, tuple(range(nl)) + (nl + 3, nl, nl + 1, nl + 2))


def kernel(x_prompt, x_sample, cache_b1, cache_b2, cache_b3, cache_c, state_pool, c_prompt, c_sample, rel_bias, w_ada, b_ada, g_pre_mix, g_post_mix, g_pre_ffn, g_post_ffn, w_in, pool_w, pool_scale, w_b_up, w_c_up, sinks, w_out, w_router, router_bias, we_gate, we_up, we_down, ws_gate, ws_up, ws_down):
    batch, seq, d = x_prompt.shape
    dec_batch, dec_seq, _ = x_sample.shape
    depth = w_in.shape[0]
    tm_p = 256
    tm_s = 128
    n_dil = len(DIL_GROUPS)
    dils = tuple(dil for _, dil in DIL_GROUPS)

    n_seq_all = batch + dec_batch
    pad = -n_seq_all % 8
    c_all = jnp.concatenate([c_prompt, c_sample, jnp.zeros((pad, d), F32)], axis=0)
    ada = _ada_all(c_all, w_ada, b_ada)

    head0 = [g * DIL_HEADS for g in range(n_dil)] + [n_dil * DIL_HEADS]
    head1 = [(g + 1) * DIL_HEADS for g in range(n_dil)] + [n_dil * DIL_HEADS + SWA_HEADS]
    band = []
    for g, dil in enumerate(dils + (1,)):
        tab = _bias_table(rel_bias, head0[g], head1[g], _band_steps(), dil)
        n_h = head1[g] - head0[g]
        band.append(tab.reshape(n_h // 2, 2, BLOCK, 2 * BLOCK).transpose(0, 2, 1, 3).reshape(n_h // 2, BLOCK, 4 * BLOCK))
    caches = (cache_b1, cache_b2, cache_b3, cache_c)
    samp = []
    for g, dil in enumerate(dils + (1,)):
        n_buf = caches[g].shape[2]
        sc_steps, sn_steps = _sample_steps(n_buf, dil, dec_seq)
        rows = (head1[g] - head0[g]) * dec_seq
        samp.append((_bias_table(rel_bias, head0[g], head1[g], sc_steps, dil).reshape(rows, n_buf),
                     _bias_table(rel_bias, head0[g], head1[g], sn_steps, dil).reshape(rows, BLOCK)))

    vec3 = lambda a: a.reshape(depth, 1, -1)
    g1, g2, g3, g4 = vec3(g_pre_mix), vec3(g_post_mix), vec3(g_pre_ffn), vec3(g_post_ffn)
    ps3, rb3 = vec3(pool_scale), router_bias.reshape(depth, N_EXPERTS, 1)
    cache_t = [_to_time_minor(c) for c in caches]
    pool_halo = jnp.pad(state_pool, ((0, 0), (0, 0), (POOL_HALO - state_pool.shape[2], 0), (0, 0)))
    pw = pool_w.astype(BF16)
    wb_all, wc_all, wo_all = w_b_up.astype(BF16), w_c_up.astype(BF16), w_out.astype(BF16)
    wr_all = jnp.swapaxes(w_router, 1, 2).astype(BF16)
    wsg, wsu, wsd = ws_gate.astype(BF16), ws_up.astype(BF16), ws_down.astype(BF16)
    post_w = (g2, g3, pw, ps3, wb_all, wc_all, wo_all, wr_all, rb3)

    xp, xs = x_prompt, x_sample
    no_tie = jnp.zeros((8, BLOCK), F32)
    cache_out = [None] * (n_dil + 1)
    states_p = None
    pool_p = []
    pool_s = []
    keeps_p = tuple(min(win, seq) for win, _ in DIL_GROUPS) + (min(BLOCK, seq),)
    n_heads = (DIL_HEADS,) * n_dil + (SWA_HEADS,)
    n_kvs = (DIL_HEADS,) * n_dil + (SWA_KV_HEADS,)
    for l in range(depth):
        w_perm = w_in[l].astype(BF16)
        ada_p = ada[l, :batch].reshape(batch, 1, -1)
        ada_s = ada[l, batch:n_seq_all].reshape(dec_batch, 1, -1)
        sink_rows = jnp.broadcast_to(jnp.repeat(sinks[l].astype(F32), dec_seq)[:, None], (SWA_HEADS * dec_seq, BLOCK))

        outs = _in_proj(xp, ada_p, g1, w_perm, l, tm_p, keeps_p, True, no_tie, states_p)
        u, gl, states_p = outs[0], outs[3 + 2 * n_dil], outs[4 + 2 * n_dil:]
        attn = []
        for g in range(n_dil):
            o, lse = _band_attn(outs[1 + 2 * g], outs[2 + 2 * g], band[g], DIL_HEADS, DIL_HEADS)
            if dils[g] == 1:
                o, lse = o.reshape(batch * seq, DIL_W), lse.reshape(batch * seq, DIL_W)
            attn += [o, lse]
        oc = _band_attn(outs[1 + 2 * n_dil], outs[2 + 2 * n_dil], band[n_dil], SWA_HEADS, SWA_KV_HEADS, sinks=sinks[l])
        attn.append(oc.reshape(batch * seq, SWA_QW))
        xmid_p, h2_p, idx, rank, wk, counts, _ = _post_mixer(xp, ada_p, u, u, attn, gl, post_w, l, tm_p, False, dils,
                                                             no_tie)
        pool_p.append(u[:, seq - (POOL_HALO - 1):])

        outs = _in_proj(xs, ada_s, g1, w_perm, l, tm_s, (dec_seq,) * (n_dil + 1), False, h2_p)
        u, gl, states = outs[0], outs[2 + n_dil], outs[3 + n_dil:]
        early = max(range(n_dil + 1), key=lambda g: cache_t[g].shape[-1])
        attn_s = [None] * (n_dil + 1)

        def sample_attn(g, tie):
            res = _sample_attn(outs[1 + g], states[g], cache_t[g], cache_out[g], samp[g][0], samp[g][1], l,
                               n_heads[g], n_kvs[g], dec_seq, tie, sink_rows=sink_rows if g == n_dil else None)
            attn_s[g] = list(res[:-1])
            cache_out[g] = res[-1]

        sample_attn(early, no_tie)
        y_k, y_sorted = _moe_routed(h2_p, idx, rank, counts, we_gate, we_up, we_down, l, 512, attn_s[early][0])
        for g in range(n_dil + 1):
            if g != early:
                sample_attn(g, y_sorted)
        attn = [a for pair in attn_s for a in pair]
        xmid, h2, _, _, _, _, wfull = _post_mixer(xs, ada_s, u, pool_halo[l], attn, gl, post_w, l, tm_s, True,
                                                  (1,) * n_dil, y_sorted)
        y_r = _dense_moe(h2, wfull, we_gate, we_up, we_down, l)
        xs = _final(xmid, ada_s, g4, h2, y_r, wsg, wsu, wsd, l, tm_s, no_tie)
        xp = _final(xmid_p, ada_p, g4, h2_p, (y_k, wk), wsg, wsu, wsd, l, tm_p, xs)
        pool_s.append(jnp.concatenate([state_pool[l], u], axis=1)[:, -(POOL_HALO - 1):])

    b1_p, b2_p, b3_p, c_p = [_from_time_minor(st, n_kvs[k]) for k, st in enumerate(states_p)]
    pool_p = jnp.stack(pool_p, axis=0)
    outs_s = [_from_time_minor(co, n_kvs[g]) for g, co in enumerate(cache_out)]
    return (xp, xs, b1_p, b2_p, b3_p, c_p, pool_p, outs_s[0], outs_s[1], outs_s[2], outs_s[3], jnp.stack(pool_s, axis=0))
```

```python
import functools

import numpy as np
import jax
import jax.numpy as jnp
from jax import lax
from jax.experimental import pallas as pl
from jax.experimental.pallas import tpu as pltpu
from jax.experimental.pallas import tpu_sc as plsc

F32 = jnp.float32
BF16 = jnp.bfloat16

HEAD_DIM = 64
SCALE = HEAD_DIM ** -0.5
BLOCK = 128
POOL_WINDOWS = (2, 4, 8, 16)
POOL_CH = 128
POOL_WIDTH = len(POOL_WINDOWS) * POOL_CH
POOL_HALO = 16
DIL_GROUPS = ((128, 1), (512, 4), (2048, 16))
DIL_HEADS = 4
DIL_W = DIL_HEADS * HEAD_DIM
SWA_HEADS = 8
SWA_KV_HEADS = 2
SWA_QW = SWA_HEADS * HEAD_DIM
SWA_KW = SWA_KV_HEADS * HEAD_DIM
N_BUCKETS = 32
MAX_DISTANCE = 2048
N_EXPERTS = 64
TOP_K = 8
ROUTED_SCALE = 2.5
EPS = 1e-6
NEG_INF = float("-inf")

V7X_VMEM_BYTES = 64 * 1024 * 1024
VMEM_LIMIT = 56 * 1024 * 1024


def _cparams(sem):
    return pltpu.CompilerParams(dimension_semantics=sem, vmem_limit_bytes=VMEM_LIMIT)


def _rms(x, g):
    ms = jnp.mean(x * x, axis=-1, keepdims=True)
    return x * lax.rsqrt(ms + EPS) * g


def _sigmoid(x):
    return 1.0 / (1.0 + jnp.exp(-x))


def _silu(x):
    return x * _sigmoid(x)


def _dot(a, b):
    return jnp.dot(a, b, preferred_element_type=F32)


def _dot_nt(a, b):
    return lax.dot_general(a, b, (((1,), (1,)), ((), ())), preferred_element_type=F32)


def _tile_split(tseq, tm):
    if tseq >= tm:
        return 1, tm, tseq // tm
    return tm // tseq, tseq, 1


_TIE_SPEC = pl.BlockSpec(memory_space=pl.ANY)


ROW_SUB = 4
ROW_LANES = 128
ROW_DTYPE = jnp.uint32
HI_MASK = 0xFFFF0000


def _pack_chunks(x):
    half = x.shape[1] // 2
    hi = pltpu.bitcast(x[:, :half].astype(BF16).astype(F32), jnp.uint32) & jnp.uint32(HI_MASK)
    lo = pltpu.bitcast(x[:, half:].astype(BF16).astype(F32), jnp.uint32) >> 16
    w = hi | lo
    return [w[:, j * ROW_LANES:(j + 1) * ROW_LANES] for j in range(ROW_SUB)]


def _unpack_chunks(chunks):
    his = [pltpu.bitcast(w & jnp.uint32(HI_MASK), F32) for w in chunks]
    los = [pltpu.bitcast(w << 16, F32) for w in chunks]
    return jnp.concatenate(his + los, axis=1)


def _store_rows(ref, chunks, rows, first=0):
    for j, c in enumerate(chunks):
        ref[pl.ds(first * ROW_SUB + j, rows, stride=ROW_SUB), :] = c


def _load_rows(ref, rows, first=0, lead=None):
    if lead is None:
        return [ref[pl.ds(first * ROW_SUB + j, rows, stride=ROW_SUB), :] for j in range(ROW_SUB)]
    return [ref[lead, pl.ds(first * ROW_SUB + j, rows, stride=ROW_SUB), :] for j in range(ROW_SUB)]


SC_CORES = 2
SC_SUBCORES = 16
SC_WORKERS = SC_CORES * SC_SUBCORES
SC_CHUNK = 64


def _sc_scatter_rows(src, dest, n_out):
    n, sub, lanes = src.shape
    kk = dest.shape[0]
    ch = min(SC_CHUNK, n // SC_WORKERS)
    n_chunks = n // (SC_WORKERS * ch)
    idx3 = dest.reshape(kk, n // ch, ch).transpose(1, 0, 2)
    mesh = plsc.VectorSubcoreMesh(core_axis_name="c", subcore_axis_name="s")

    @functools.partial(
        pl.kernel, mesh=mesh,
        out_type=jax.ShapeDtypeStruct((n_out, sub, lanes), src.dtype),
        scratch_types=[pltpu.VMEM((kk, ch), jnp.int32), pltpu.VMEM((ch, sub, lanes), src.dtype),
                       pltpu.SemaphoreType.DMA],
    )
    def scatter_kernel(src_hbm, idx_hbm, out_hbm, idx_v, rows_v, sem):
        wid = lax.axis_index("s") * SC_CORES + lax.axis_index("c")

        @pl.loop(0, n_chunks)
        def _(c):
            j = wid * n_chunks + c
            pltpu.sync_copy(idx_hbm.at[j], idx_v)
            pltpu.sync_copy(src_hbm.at[pl.ds(j * ch, ch)], rows_v)
            copies = [pltpu.async_copy(rows_v, out_hbm.at[idx_v.at[q]], sem) for q in range(kk)]
            for cp in copies:
                cp.wait()

    return scatter_kernel(src, idx3)


def _sc_gather_rows(table, dest):
    v, sub, lanes = table.shape
    kk, n = dest.shape
    total = n * kk
    ch = min(SC_CHUNK, total // SC_WORKERS)
    n_chunks = total // (SC_WORKERS * ch)
    idx2 = dest.reshape(total // ch, ch)
    mesh = plsc.VectorSubcoreMesh(core_axis_name="c", subcore_axis_name="s")

    assert n_chunks % 2 == 0, "the two-buffer gather pipeline walks chunks in pairs"

    @functools.partial(
        pl.kernel, mesh=mesh,
        out_type=jax.ShapeDtypeStruct((total, sub, lanes), table.dtype),
        scratch_types=[pltpu.VMEM((n_chunks, ch), jnp.int32), pltpu.VMEM((2, ch, sub, lanes), table.dtype),
                       pltpu.SemaphoreType.DMA((2,)), pltpu.SemaphoreType.DMA((2,))],
    )
    def gather_kernel(tab_hbm, idx_hbm, out_hbm, idx_v, rows_v, gsem, wsem):
        wid = lax.axis_index("s") * SC_CORES + lax.axis_index("c")
        base = wid * n_chunks
        pltpu.sync_copy(idx_hbm.at[pl.ds(base, n_chunks)], idx_v)

        def gather(c, b):
            return pltpu.make_async_copy(tab_hbm.at[idx_v.at[c]], rows_v.at[b], gsem.at[b])

        def write(c, b):
            return pltpu.make_async_copy(rows_v.at[b], out_hbm.at[pl.ds((base + c) * ch, ch)], wsem.at[b])

        gather(0, 0).start()

        @pl.loop(0, n_chunks, step=2)
        def _(c0):
            for b in range(2):
                c = c0 + b
                gather(c, b).wait()

                @pl.when(c + 1 < n_chunks)
                def _():
                    @pl.when(c >= 1)
                    def _():
                        write(c - 1, 1 - b).wait()

                    gather(c + 1, 1 - b).start()

                write(c, b).start()

        write(n_chunks - 2, 0).wait()
        write(n_chunks - 1, 1).wait()

    return gather_kernel(table, idx2).reshape(kk, n, sub, lanes)


def _ada_kernel(c_ref, w_ref, b_ref, o_ref):
    c = _silu(c_ref[...]).astype(BF16)
    o_ref[0] = _dot(c, w_ref[0].astype(BF16)) + b_ref[0]


def _ada_all(c_all, w_ada, b_ada):
    depth, d, n = w_ada.shape
    rows = c_all.shape[0]
    tn = 1536
    return pl.pallas_call(
        _ada_kernel,
        grid=(depth, n // tn),
        in_specs=[
            pl.BlockSpec((rows, d), lambda l, j: (0, 0)),
            pl.BlockSpec((1, d, tn), lambda l, j: (l, 0, j)),
            pl.BlockSpec((1, 1, tn), lambda l, j: (l, 0, j)),
        ],
        out_specs=pl.BlockSpec((1, rows, tn), lambda l, j: (l, 0, j)),
        out_shape=jax.ShapeDtypeStruct((depth, rows, n), F32),
        compiler_params=_cparams(("parallel", "parallel")),
        name="ada",
    )(c_all, w_ada, b_ada.reshape(depth, 1, n))


def _in_proj_kernel(x_ref, sh_ref, sc_ref, g_ref, w_ref, tie_ref, *refs, dils, fold, n_alias):
    refs = refs[n_alias:]
    s, t, d = x_ref.shape
    rows = s * t
    hb_s = refs[-1]

    @pl.when(pl.program_id(0) == 0)
    def _():
        hb_s[...] = jnp.zeros_like(hb_s)

    hb = hb_s[...]
    n_g = len(dils)
    if fold:
        u_ref = refs[0]
        q_refs = refs[1:1 + 2 * n_g:2] + (refs[1 + 2 * n_g],)
        kv_refs = refs[2:2 + 2 * n_g:2] + (refs[2 + 2 * n_g],)
        gl_ref = refs[3 + 2 * n_g]
        st_refs = refs[4 + 2 * n_g:5 + 3 * n_g]
        zs_ref = refs[5 + 3 * n_g]
    else:
        u_ref = refs[0]
        q_refs = refs[1:2 + n_g]
        kv_refs = (None,) * (n_g + 1)
        gl_ref = refs[2 + n_g]
        st_refs = refs[3 + n_g:4 + 2 * n_g]
        zs_ref = None

    def proj(c0, c1):
        return _dot(hb, w_ref[:, c0:c1])

    u_ref[...] = proj(0, POOL_WIDTH).reshape(s, t, POOL_WIDTH)
    q0 = POOL_WIDTH
    k0 = q0 + n_g * DIL_W
    v0 = k0 + n_g * DIL_W
    c0 = v0 + n_g * DIL_W
    for g, dil in enumerate(tuple(dils) + (1,)):
        if g < n_g:
            qw = DIL_W
            zq = proj(q0 + g * DIL_W, q0 + (g + 1) * DIL_W) * SCALE
            zkv = jnp.concatenate([proj(k0 + g * DIL_W, k0 + (g + 1) * DIL_W),
                                   proj(v0 + g * DIL_W, v0 + (g + 1) * DIL_W)], axis=1)
        else:
            qw = SWA_QW
            z = proj(c0, c0 + SWA_QW + 2 * SWA_KW)
            zq = z[:, :qw] * SCALE
            zkv = z[:, qw:]
        st = st_refs[g]
        st[...] = zkv[rows - st.shape[-1]:, :].T
        if not fold:
            q_refs[g][...] = zq
        elif dil == 1:
            q_refs[g][0, 0] = zq.astype(BF16)
            kv_refs[g][0, 0] = zkv.astype(BF16)
        else:
            zf = jnp.concatenate([zq, zkv], axis=1)
            n_c = zf.shape[1] // BLOCK
            for c in range(n_c):
                zs_ref[c] = zf[:, c * BLOCK:(c + 1) * BLOCK]
            n = rows // dil
            for r in range(dil):
                part = jnp.concatenate([zs_ref[c, pl.ds(r, n, stride=dil), :] for c in range(n_c)], axis=1)
                q_refs[g][0, r] = part[:, :qw].astype(BF16)
                kv_refs[g][0, r] = part[:, qw:].astype(BF16)
    gl_ref[...] = proj(c0 + SWA_QW + 2 * SWA_KW, c0 + SWA_QW + 2 * SWA_KW + 3 * d)
    h = _rms(x_ref[...], g_ref[...]) * (1.0 + sc_ref[...]) + sh_ref[...]
    hb_s[...] = h.reshape(rows, d).astype(BF16)


def _in_proj(x3, ada3, g_pre, w_perm, layer, tm, keeps, fold, tie, prev_states=None):
    nseq, tseq, d = x3.shape
    ntok = nseq * tseq
    s_blk, t_blk, tps = _tile_split(tseq, tm)
    n_tiles = ntok // tm
    in_w = w_perm.shape[-1]
    depth = g_pre.shape[0]
    dils = tuple(dil for _, dil in DIL_GROUPS)
    widths = [(DIL_W, 2 * DIL_W)] * len(dils) + [(SWA_QW, 2 * SWA_KW)]

    def feeds(tile_map):
        return lambda i: tile_map(jnp.minimum(i, n_tiles - 1))

    def drains(tile_map):
        return lambda i: tile_map(jnp.maximum(i - 1, 0))

    def x_tile(i):
        return (i // tps, i % tps, 0)

    def tok_tile(i):
        return (i, 0)

    out_shapes = [jax.ShapeDtypeStruct((nseq, tseq, POOL_WIDTH), F32)]
    out_specs = [pl.BlockSpec((s_blk, t_blk, POOL_WIDTH), drains(x_tile))]
    for (qw, kvw), dil in zip(widths, dils + (1,)):
        if fold:
            for w in (qw, kvw):
                out_shapes.append(jax.ShapeDtypeStruct((nseq, dil, tseq // dil, w), BF16))
                out_specs.append(pl.BlockSpec((1, dil, tm // dil, w), drains(lambda i: (i // tps, 0, i % tps, 0))))
        else:
            out_shapes.append(jax.ShapeDtypeStruct((ntok, qw), F32))
            out_specs.append(pl.BlockSpec((tm, qw), drains(tok_tile)))
    out_shapes.append(jax.ShapeDtypeStruct((ntok, 3 * d), F32))
    out_specs.append(pl.BlockSpec((tm, 3 * d), drains(tok_tile)))
    for keep, (_, kvw) in zip(keeps, widths):
        if fold:
            sb = min(tm, keep)
            bps = keep // sb
            first = (tseq - keep) // tm

            def st_map(i, bps=bps, first=first):
                return (layer, i // tps, 0, jnp.maximum(i % tps - first, 0) * (1 if bps > 1 else 0))

            out_shapes.append(jax.ShapeDtypeStruct((depth, nseq, kvw, keep), F32))
            out_specs.append(pl.BlockSpec((None, None, kvw, sb), drains(st_map)))
        else:
            out_shapes.append(jax.ShapeDtypeStruct((kvw, ntok), F32))
            out_specs.append(pl.BlockSpec((kvw, tm), drains(lambda i: (0, i))))
    scratch = ([pltpu.VMEM((3 * DIL_W // BLOCK, tm, BLOCK), F32)] if fold else []) + [pltpu.VMEM((tm, d), BF16)]
    aliased = list(prev_states) if prev_states is not None else []
    n_fixed = 6
    first_state = len(out_shapes) - len(keeps)
    aliases = {n_fixed + k: first_state + k for k in range(len(aliased))}
    return pl.pallas_call(
        functools.partial(_in_proj_kernel, dils=dils, fold=fold, n_alias=len(aliased)),
        grid=(n_tiles + 1,),
        in_specs=[
            pl.BlockSpec((s_blk, t_blk, d), feeds(x_tile)),
            pl.BlockSpec((s_blk, 1, d), feeds(lambda i: (i // tps, 0, 0))),
            pl.BlockSpec((s_blk, 1, d), feeds(lambda i: (i // tps, 0, 1))),
            pl.BlockSpec((1, 1, d), lambda i: (layer, 0, 0)),
            pl.BlockSpec((d, in_w), lambda i: (0, 0)),
            _TIE_SPEC,
        ] + [pl.BlockSpec(memory_space=pl.ANY)] * len(aliased),
        out_specs=out_specs,
        out_shape=out_shapes,
        input_output_aliases=aliases,
        scratch_shapes=scratch,
        compiler_params=_cparams(("arbitrary",)),
        name="in_proj",
    )(x3, ada3, ada3, g_pre, w_perm, tie, *aliased)


PAIR = 2 * HEAD_DIM


def _band_attn_kernel(*refs, n_heads, n_kv, nq, with_sink):
    if with_sink:
        sink_ref, q_ref, kv_ref, halo_ref, bm_ref, mask_ref, o_ref = refs
        lse_ref = None
    else:
        q_ref, kv_ref, halo_ref, bm_ref, mask_ref, o_ref, lse_ref = refs
        sink_ref = None
    kw = n_kv * HEAD_DIM
    grp = n_heads // n_kv
    chunk = pl.program_id(2)
    lo, hi = mask_ref[0], mask_ref[1]
    low_lanes = lax.broadcasted_iota(jnp.int32, (BLOCK, PAIR), 1) < HEAD_DIM
    col = lax.broadcasted_iota(jnp.int32, (BLOCK, 4 * BLOCK), 1)
    prev_penalty = jnp.where(col % (2 * BLOCK) < BLOCK, NEG_INF, 0.0)

    def swap_halves(x):
        return jnp.concatenate([x[:, HEAD_DIM:], x[:, :HEAD_DIM]], axis=1)

    def pair_sources(kv_blk, p):
        if grp == 1:
            k = kv_blk[:, p * PAIR:(p + 1) * PAIR]
            v = kv_blk[:, kw + p * PAIR:kw + (p + 1) * PAIR]
            return (k, k), (v, v)
        kh = (2 * p) // grp
        c0 = (kh // 2) * PAIR
        k = kv_blk[:, c0:c0 + PAIR]
        v = kv_blk[:, kw + c0:kw + c0 + PAIR]
        ks, vs = swap_halves(k), swap_halves(v)
        return ((k, ks), (v, vs)) if kh % 2 == 0 else ((ks, k), (vs, v))

    def one_block(r0, kv_prev, first):
        qb = q_ref[pl.ds(r0, BLOCK), :]
        kv_cur = kv_ref[pl.ds(r0, BLOCK), :]
        for p in range(n_heads // 2):
            (kpe, kpo), (vpe, vpo) = pair_sources(kv_prev, p)
            (kce, kco), (vce, vco) = pair_sources(kv_cur, p)
            k_blk = jnp.concatenate([kpe * lo, kce * lo, kpo * hi, kco * hi], axis=0)
            v_blk = jnp.concatenate([
                jnp.concatenate([vpe * lo, lo], axis=1), jnp.concatenate([vce * lo, lo], axis=1),
                jnp.concatenate([vpo * hi, hi], axis=1), jnp.concatenate([vco * hi, hi], axis=1)], axis=0)
            s = _dot_nt(qb[:, p * PAIR:(p + 1) * PAIR], k_blk) + bm_ref[p]
            if first is not None:
                s = s + jnp.where(first, prev_penalty, 0.0)
            s_e, s_o = s[:, :2 * BLOCK], s[:, 2 * BLOCK:]
            m_e = jnp.max(jnp.maximum(s_e[:, :BLOCK], s_e[:, BLOCK:]), axis=-1, keepdims=True)
            m_o = jnp.max(jnp.maximum(s_o[:, :BLOCK], s_o[:, BLOCK:]), axis=-1, keepdims=True)
            if with_sink:
                m_e = jnp.maximum(m_e, sink_ref[2 * p])
                m_o = jnp.maximum(m_o, sink_ref[2 * p + 1])
            pr = jnp.concatenate([jnp.exp(s_e - m_e), jnp.exp(s_o - m_o)], axis=1).astype(BF16)
            res = _dot(pr, v_blk)
            den = res[:, PAIR:]
            if with_sink:
                den = den + jnp.where(low_lanes, jnp.exp(sink_ref[2 * p] - m_e), jnp.exp(sink_ref[2 * p + 1] - m_o))
            o_ref[pl.ds(r0, BLOCK), p * PAIR:(p + 1) * PAIR] = res[:, :PAIR] / den
            if lse_ref is not None:
                lse_ref[pl.ds(r0, BLOCK), p * PAIR:(p + 1) * PAIR] = jnp.where(low_lanes, m_e, m_o) + jnp.log(den)

    one_block(0, halo_ref[...], chunk == 0)

    def body(j, carry):
        r0 = pl.multiple_of(j * BLOCK, BLOCK)
        one_block(r0, kv_ref[pl.ds(r0 - BLOCK, BLOCK), :], None)
        return carry

    if nq > 1:
        lax.fori_loop(1, nq, body, 0)


def _band_attn(q, kv, bm, n_heads, n_kv, sinks=None):
    batch, dil, fold, qw = q.shape
    kvw = kv.shape[-1]
    nq = min(8, fold // BLOCK)
    rows = nq * BLOCK
    n_chunks = fold // rows
    with_sink = sinks is not None
    lanes_low = np.arange(PAIR) < HEAD_DIM
    masks = jnp.asarray(np.broadcast_to(np.stack([lanes_low, ~lanes_low])[:, None, :], (2, BLOCK, PAIR)), F32).astype(BF16)
    in_specs = [
        pl.BlockSpec((None, None, rows, qw), lambda b, r, c: (b, r, c, 0)),
        pl.BlockSpec((None, None, rows, kvw), lambda b, r, c: (b, r, c, 0)),
        pl.BlockSpec((None, None, BLOCK, kvw), lambda b, r, c: (b, r, jnp.maximum(c * nq - 1, 0), 0)),
        pl.BlockSpec(bm.shape, lambda b, r, c: (0, 0, 0)),
        pl.BlockSpec(masks.shape, lambda b, r, c: (0, 0, 0)),
    ]
    args = [q, kv, kv, bm, masks]
    o_spec = pl.BlockSpec((None, None, rows, qw), lambda b, r, c: (b, r, c, 0))
    o_shape = jax.ShapeDtypeStruct((batch, dil, fold, qw), F32)
    if with_sink:
        in_specs = [pl.BlockSpec(memory_space=pltpu.SMEM)] + in_specs
        args = [sinks] + args
        out_specs, out_shape = o_spec, o_shape
    else:
        out_specs, out_shape = [o_spec, o_spec], [o_shape, o_shape]
    return pl.pallas_call(
        functools.partial(_band_attn_kernel, n_heads=n_heads, n_kv=n_kv, nq=nq, with_sink=with_sink),
        grid=(batch, dil, n_chunks),
        in_specs=in_specs,
        out_specs=out_specs,
        out_shape=out_shape,
        compiler_params=_cparams(("parallel", "parallel", "parallel")),
        name="band_attn",
    )(*args)


SAMPLE_SEQ_UNROLL = 4


def _sample_attn_kernel(*refs, n_heads, n_kv, sb, t_new, with_sink, aliased):
    refs = list(refs)
    q_ref, newt_ref, cache_ref, bmc_ref, bmn_ref, tie_ref = refs[:6]
    refs = refs[6:]
    sink_ref = refs.pop(0) if with_sink else None
    if aliased:
        refs.pop(0)
    o_ref = refs.pop(0)
    lse_ref = None if with_sink else refs.pop(0)
    cout_ref = refs.pop(0)
    kw = n_kv * HEAD_DIM
    grp = n_heads // n_kv
    w, n_buf = cache_ref.shape[2:]
    first_new = BLOCK - t_new
    per_blk = BLOCK // t_new
    i = pl.program_id(0)
    lane = lax.broadcasted_iota(jnp.int32, (w, BLOCK), 1)

    def one_seq(s, carry):
        r0 = s * t_new if isinstance(s, int) else pl.multiple_of(s * t_new, t_new)
        qs = q_ref[pl.ds(r0, t_new), :]
        q_rows = []
        for h in range(n_heads):
            piece = qs[:, h * HEAD_DIM:(h + 1) * HEAD_DIM]
            parts = [piece if k == h // grp else jnp.zeros_like(piece) for k in range(n_kv)]
            q_rows.append(jnp.concatenate(parts, axis=1))
        qbd = jnp.concatenate(q_rows, axis=0).astype(BF16)
        cache = cache_ref[0, s]
        off = ((i * sb + s) % per_blk) * t_new
        placed = pltpu.roll(newt_ref[...], first_new - off, axis=1)
        s_c = _dot(qbd, cache[:kw].astype(BF16)) + bmc_ref[...]
        s_n = _dot(qbd, placed[:kw].astype(BF16)) + bmn_ref[...]
        m = jnp.maximum(jnp.max(s_c, axis=-1, keepdims=True), jnp.max(s_n, axis=-1, keepdims=True))
        if with_sink:
            m = jnp.maximum(m, sink_ref[:, :1])
        p_c = jnp.exp(s_c - m)
        p_n = jnp.exp(s_n - m)
        den = jnp.sum(p_c, axis=-1, keepdims=True) + jnp.sum(p_n, axis=-1, keepdims=True)
        if with_sink:
            den = den + jnp.exp(sink_ref[:, :1] - m)
        o_all = (_dot_nt(p_c.astype(BF16), cache[kw:].astype(BF16))
                 + _dot_nt(p_n.astype(BF16), placed[kw:].astype(BF16))) / den
        outs = []
        for h in range(n_heads):
            kh = h // grp
            outs.append(o_all[h * t_new:(h + 1) * t_new, kh * HEAD_DIM:(kh + 1) * HEAD_DIM])
        o_ref[pl.ds(r0, t_new), :] = jnp.concatenate(outs, axis=1)
        if lse_ref is not None:
            lse = m + jnp.log(den)
            lse_ref[pl.ds(r0, t_new), :] = jnp.concatenate(
                [jnp.broadcast_to(lse[h * t_new:(h + 1) * t_new], (t_new, HEAD_DIM)) for h in range(n_heads)], axis=1)
        rolled = pltpu.roll(cache, n_buf - t_new, axis=1)
        if n_buf > BLOCK:
            cout_ref[0, s, :, : n_buf - BLOCK] = rolled[:, : n_buf - BLOCK]
        cout_ref[0, s, :, n_buf - BLOCK:] = jnp.where(lane >= first_new, placed, rolled[:, n_buf - BLOCK:])
        return carry

    group = min(sb, SAMPLE_SEQ_UNROLL)
    if sb == group:
        for s in range(sb):
            one_seq(s, 0)
    else:
        def body(it, carry):
            for u in range(group):
                one_seq(it * group + u, carry)
            return carry

        lax.fori_loop(0, sb // group, body, 0)


def _sample_attn(q, newt, cache_t, prev_out, bmc, bmn, layer, n_heads, n_kv, t_new, tie, sink_rows=None):
    depth, nseq, w, n_buf = cache_t.shape
    qw = n_heads * HEAD_DIM
    sb = max(1, min(BLOCK // t_new, 2048 // n_buf))
    with_sink = sink_rows is not None
    aliased = prev_out is not None
    rows = sb * t_new
    per_blk = BLOCK // t_new
    in_specs = [
        pl.BlockSpec((rows, qw), lambda i: (i, 0)),
        pl.BlockSpec((w, BLOCK), lambda i: (0, (i * sb) // per_blk)),
        pl.BlockSpec((1, sb, w, n_buf), lambda i: (layer, i, 0, 0)),
        pl.BlockSpec(bmc.shape, lambda i: (0, 0)),
        pl.BlockSpec(bmn.shape, lambda i: (0, 0)),
        _TIE_SPEC,
    ]
    args = [q, newt, cache_t, bmc, bmn, tie]
    if with_sink:
        in_specs.append(pl.BlockSpec(sink_rows.shape, lambda i: (0, 0)))
        args.append(sink_rows)
    aliases = {}
    if aliased:
        aliases = {len(args): 1 if with_sink else 2}
        in_specs.append(pl.BlockSpec(memory_space=pl.ANY))
        args.append(prev_out)
    o_spec = pl.BlockSpec((rows, qw), lambda i: (i, 0))
    o_shape = jax.ShapeDtypeStruct((nseq * t_new, qw), F32)
    c_spec = pl.BlockSpec((1, sb, w, n_buf), lambda i: (layer, i, 0, 0))
    c_shape = jax.ShapeDtypeStruct(cache_t.shape, F32)
    if with_sink:
        out_specs, out_shape = [o_spec, c_spec], [o_shape, c_shape]
    else:
        out_specs, out_shape = [o_spec, o_spec, c_spec], [o_shape, o_shape, c_shape]
    return pl.pallas_call(
        functools.partial(_sample_attn_kernel, n_heads=n_heads, n_kv=n_kv, sb=sb, t_new=t_new,
                          with_sink=with_sink, aliased=aliased),
        grid=(nseq // sb,),
        in_specs=in_specs,
        out_specs=out_specs,
        out_shape=out_shape,
        input_output_aliases=aliases,
        compiler_params=_cparams(("parallel",)),
        name="sample_attn",
    )(*args)


def _post_kernel(x_ref, gt_ref, shf_ref, scf_ref, g2_ref, g3_ref, u_ref, halo_ref,
                 o1_ref, l1_ref, o2_ref, l2_ref, o3_ref, l3_ref, oc_ref, gl_ref,
                 pw_ref, ps_ref, wb_ref, wc_ref, wo_ref, wrt_ref, rb_ref, tie_ref,
                 xmid_ref, h2_ref, idx_ref, rank_ref, wk_ref, counts_ref, wfull_ref, cnt_ref, *unfold_refs,
                 tps, full_windows, dils):
    s, t, d = x_ref.shape
    rows = s * t
    i = pl.program_id(0)

    u = u_ref[...]
    halo = halo_ref[...]
    if not full_windows:
        halo = jnp.where(i % tps == 0, 0.0, halo)
    ue = jnp.concatenate([halo, u], axis=1)
    if full_windows:
        row = None
    else:
        row = (i % tps) * t + lax.broadcasted_iota(jnp.int32, (1, t, 1), 1)
    parts = []
    for g, win in enumerate(POOL_WINDOWS):
        cs = slice(g * POOL_CH, (g + 1) * POOL_CH)
        acc = ue[:, :, cs]
        base = 0
        span = 1
        while span < win:
            acc = acc[:, span:, :] + acc[:, : acc.shape[1] - span, :]
            base += span
            span *= 2
        tot = acc[:, POOL_HALO - base:, :]
        if full_windows:
            mean = tot / float(win)
        else:
            cnt = jnp.minimum(row + 1, win).astype(F32)
            mean = tot / cnt
        zg = (mean - u[:, :, cs]).reshape(rows, POOL_CH).astype(BF16)
        parts.append(_dot(zg, pw_ref[0, g]))
    a = jnp.concatenate(parts, axis=-1) * ps_ref[0]

    scratch = list(unfold_refs)

    def token_order(ref, dil):
        if dil == 1:
            return ref[...]
        scr = scratch.pop(0)
        n = rows // dil
        n_c = scr.shape[0]
        for r in range(dil):
            part = ref[0, r]
            for c in range(n_c):
                scr[c, pl.ds(r, n, stride=dil), :] = part[:, c * BLOCK:(c + 1) * BLOCK]
        return jnp.concatenate([scr[c] for c in range(n_c)], axis=1)

    o1, l1 = token_order(o1_ref, dils[0]), token_order(l1_ref, dils[0])
    o2, l2 = token_order(o2_ref, dils[1]), token_order(l2_ref, dils[1])
    o3, l3 = token_order(o3_ref, dils[2]), token_order(l3_ref, dils[2])

    lm = jnp.maximum(jnp.maximum(l1, l2), l3)
    e1, e2, e3 = jnp.exp(l1 - lm), jnp.exp(l2 - lm), jnp.exp(l3 - lm)
    esum = e1 + e2 + e3
    bmix = (e1 / esum) * o1 + (e2 / esum) * o2 + (e3 / esum) * o3
    b = _dot(bmix.astype(BF16), wb_ref[...])
    c = _dot(oc_ref[...].astype(BF16), wc_ref[...])
    g_a = _sigmoid(gl_ref[:, :d])
    g_b = _sigmoid(gl_ref[:, d:2 * d])
    g_c = _sigmoid(gl_ref[:, 2 * d:])
    mix = _dot((g_a * a + g_b * b + g_c * c).astype(BF16), wo_ref[...])

    x = x_ref[...]
    xm = x + gt_ref[...] * _rms(mix, g2_ref[0]).reshape(s, t, d)
    xmid_ref[...] = xm
    h2f = (_rms(xm, g3_ref[...]) * (1.0 + scf_ref[...]) + shf_ref[...]).reshape(rows, d)
    h2 = h2f.astype(BF16)
    _store_rows(h2_ref, _pack_chunks(h2f), rows)

    scores = _sigmoid(_dot_nt(wrt_ref[...], h2))
    work = scores + rb_ref[0]
    e_iota = lax.broadcasted_iota(jnp.int32, (N_EXPERTS, rows), 0).astype(F32)
    sel = jnp.zeros((N_EXPERTS, rows), F32)
    picks = []
    for _ in range(TOP_K):
        mx = jnp.max(work, axis=0, keepdims=True)
        pick = jnp.min(jnp.where(work == mx, e_iota, float(N_EXPERTS)), axis=0, keepdims=True)
        hit = e_iota == pick
        sel = jnp.where(hit, 1.0, sel)
        work = jnp.where(hit, NEG_INF, work)
        picks.append(pick)
    top_s = scores * sel
    wmat = top_s / jnp.sum(top_s, axis=0, keepdims=True) * ROUTED_SCALE

    @pl.when(i == 0)
    def _():
        cnt_ref[...] = jnp.zeros_like(cnt_ref)

    r_i = lax.broadcasted_iota(jnp.int32, (rows, rows), 0)
    c_i = lax.broadcasted_iota(jnp.int32, (rows, rows), 1)
    before = jnp.where(r_i < c_i, 1.0, 0.0).astype(BF16)
    rank_all = cnt_ref[...] + _dot(sel.astype(BF16), before)
    cnt_ref[...] = cnt_ref[...] + jnp.sum(sel, axis=1, keepdims=True)
    counts_ref[...] = cnt_ref[...].astype(jnp.int32)

    krow = lax.broadcasted_iota(jnp.int32, (TOP_K, rows), 0)
    idx = jnp.zeros((TOP_K, rows), F32)
    rank = jnp.zeros((TOP_K, rows), F32)
    wk = jnp.zeros((TOP_K, rows), F32)
    for k, pick in enumerate(picks):
        hit = e_iota == pick
        idx = jnp.where(krow == k, pick, idx)
        rank = jnp.where(krow == k, jnp.sum(jnp.where(hit, rank_all, 0.0), axis=0, keepdims=True), rank)
        wk = jnp.where(krow == k, jnp.sum(jnp.where(hit, wmat, 0.0), axis=0, keepdims=True), wk)
    idx_ref[...] = idx.astype(jnp.int32)
    rank_ref[...] = rank.astype(jnp.int32)
    eye = jnp.where(r_i == c_i, 1.0, 0.0).astype(BF16)

    def token_major(w):
        hi = w.astype(BF16)
        rest = w - hi.astype(F32)
        mid = rest.astype(BF16)
        lo = (rest - mid.astype(F32)).astype(BF16)
        return _dot_nt(eye, hi) + _dot_nt(eye, mid) + _dot_nt(eye, lo)

    wk_ref[...] = token_major(wk)
    wfull_ref[...] = token_major(wmat)


def _post_mixer(x3, ada3, u3, halo3, attn, gl, weights, layer, tm, full_windows, dils, tie):
    nseq, tseq, d = x3.shape
    ntok = nseq * tseq
    s_blk, t_blk, tps = _tile_split(tseq, tm)
    n_tiles = ntok // tm
    (g2, g3, pool_w, pool_scale, w_b_up, w_c_up, w_out, w_router, router_bias) = weights

    def x_map(i):
        return (i // tps, i % tps, 0)

    def tok_map(i):
        return (i, 0)

    def ada_spec(j):
        return pl.BlockSpec((s_blk, 1, d), lambda i: (i // tps, 0, j))

    if full_windows:
        halo_spec = pl.BlockSpec((s_blk, POOL_HALO, POOL_WIDTH), lambda i: (i, 0, 0))
    else:
        hb = t_blk // POOL_HALO
        halo_spec = pl.BlockSpec((1, POOL_HALO, POOL_WIDTH),
                                 lambda i: (i // tps, jnp.maximum((i % tps) * hb - 1, 0), 0))
    vec = pl.BlockSpec((1, 1, d), lambda i: (layer, 0, 0))

    def per_layer(w):
        return pl.BlockSpec((None,) + w.shape[1:], lambda i: (layer, 0, 0))

    attn_specs = []
    for g, dil in enumerate(dils):
        if dil == 1:
            spec = pl.BlockSpec((tm, DIL_W), tok_map)
        else:
            spec = pl.BlockSpec((1, dil, tm // dil, DIL_W), lambda i: (i // tps, 0, i % tps, 0))
        attn_specs += [spec, spec]
    in_specs = [
        pl.BlockSpec((s_blk, t_blk, d), x_map), ada_spec(2), ada_spec(3), ada_spec(4), vec, vec,
        pl.BlockSpec((s_blk, t_blk, POOL_WIDTH), x_map), halo_spec,
    ] + attn_specs + [
        pl.BlockSpec((tm, SWA_QW), tok_map), pl.BlockSpec((tm, 3 * d), tok_map),
        pl.BlockSpec((1,) + pool_w.shape[1:], lambda i: (layer, 0, 0, 0)),
        vec,
        per_layer(w_b_up), per_layer(w_c_up), per_layer(w_out), per_layer(w_router),
        pl.BlockSpec((1, N_EXPERTS, 1), lambda i: (layer, 0, 0)),
        _TIE_SPEC,
    ]
    out_shape = [
        jax.ShapeDtypeStruct((nseq, tseq, d), F32),
        jax.ShapeDtypeStruct((ntok * ROW_SUB, ROW_LANES), ROW_DTYPE),
        jax.ShapeDtypeStruct((TOP_K, ntok), jnp.int32),
        jax.ShapeDtypeStruct((TOP_K, ntok), jnp.int32),
        jax.ShapeDtypeStruct((ntok, TOP_K), F32),
        jax.ShapeDtypeStruct((N_EXPERTS, 1), jnp.int32),
        jax.ShapeDtypeStruct((ntok, N_EXPERTS), F32),
    ]
    out_specs = [
        pl.BlockSpec((s_blk, t_blk, d), x_map),
        pl.BlockSpec((tm * ROW_SUB, ROW_LANES), tok_map),
        pl.BlockSpec((TOP_K, tm), lambda i: (0, i)),
        pl.BlockSpec((TOP_K, tm), lambda i: (0, i)),
        pl.BlockSpec((tm, TOP_K), tok_map),
        pl.BlockSpec((N_EXPERTS, 1), lambda i: (0, 0)),
        pl.BlockSpec((tm, N_EXPERTS), tok_map),
    ]
    n_unfold = 2 * sum(1 for dil in dils if dil > 1)
    scratch = [pltpu.VMEM((N_EXPERTS, 1), F32)] + [pltpu.VMEM((DIL_W // BLOCK, tm, BLOCK), F32)] * n_unfold
    return pl.pallas_call(
        functools.partial(_post_kernel, tps=tps, full_windows=full_windows, dils=tuple(dils)),
        grid=(n_tiles,),
        in_specs=in_specs,
        out_specs=out_specs,
        out_shape=out_shape,
        scratch_shapes=scratch,
        compiler_params=_cparams(("arbitrary",)),
        name="post_mixer",
    )(x3, ada3, ada3, ada3, g2, g3, u3, halo3, *attn, gl,
      pool_w, pool_scale, w_b_up, w_c_up, w_out, w_router, router_bias, tie)


EXPERT_ROWS = 256


def _expert_kernel(bexp_ref, nvalid_ref, nused_ref, x_ref, wg_ref, wu_ref, wd_ref, tie_ref, y_ref, wg_s, wu_s, wd_s, h_s):
    i = pl.program_id(0)
    bm = x_ref.shape[0] // ROW_SUB
    sub = min(bm, EXPERT_ROWS)
    last = bexp_ref.shape[0] - 1
    up_blk = jnp.minimum(i, last)
    dn_blk = jnp.maximum(i - 1, 0)

    @pl.when((i == 0) | (bexp_ref[up_blk] != bexp_ref[jnp.maximum(up_blk - 1, 0)]))
    def _():
        wg_s[...] = wg_ref[0, 0].astype(BF16)
        wu_s[...] = wu_ref[0, 0].astype(BF16)

    @pl.when((i <= 1) | (bexp_ref[dn_blk] != bexp_ref[jnp.maximum(dn_blk - 1, 0)]))
    def _():
        wd_s[...] = wd_ref[0, 0].astype(BF16)

    @pl.when(i == 0)
    def _():
        h_s[...] = jnp.zeros_like(h_s)

    @pl.when(i <= nused_ref[0])
    def _():
        for c in range(bm // sub):
            y = _dot(h_s[c * sub:(c + 1) * sub, :], wd_s[...])
            _store_rows(y_ref, _pack_chunks(y), sub, first=c * sub)
        for c in range(bm // sub):
            x = _unpack_chunks(_load_rows(x_ref, sub, first=c * sub))
            row = c * sub + lax.broadcasted_iota(jnp.int32, (sub, 1), 0)
            x = jnp.where(row < nvalid_ref[up_blk], x, 0.0).astype(BF16)
            gate = _dot(x, wg_s[...])
            up = _dot(x, wu_s[...])
            h_s[c * sub:(c + 1) * sub, :] = (_silu(gate) * up).astype(BF16)

    @pl.when(i > nused_ref[0])
    def _():
        y_ref[...] = jnp.zeros_like(y_ref)


def _experts(x_rows, block_exp, nvalid, n_used, we_gate, we_up, we_down, layer, bm, tie):
    n_rows = x_rows.shape[0] // ROW_SUB
    d, ff = we_gate.shape[-2:]
    n_blocks = n_rows // bm
    last = n_blocks - 1

    def up_map(i, be, nv, nu):
        return (layer, be[jnp.minimum(i, last)], 0, 0)

    def down_map(i, be, nv, nu):
        return (layer, be[jnp.maximum(i - 1, 0)], 0, 0)

    grid_spec = pltpu.PrefetchScalarGridSpec(
        num_scalar_prefetch=3,
        grid=(n_blocks + 1,),
        in_specs=[
            pl.BlockSpec((bm * ROW_SUB, ROW_LANES), lambda i, be, nv, nu: (jnp.minimum(i, last), 0)),
            pl.BlockSpec((1, 1, d, ff), up_map),
            pl.BlockSpec((1, 1, d, ff), up_map),
            pl.BlockSpec((1, 1, ff, d), down_map),
            _TIE_SPEC,
        ],
        out_specs=pl.BlockSpec((bm * ROW_SUB, ROW_LANES), lambda i, be, nv, nu: (jnp.maximum(i - 1, 0), 0)),
        scratch_shapes=[pltpu.VMEM((d, ff), BF16), pltpu.VMEM((d, ff), BF16), pltpu.VMEM((ff, d), BF16),
                        pltpu.VMEM((bm, ff), BF16)],
    )
    return pl.pallas_call(
        _expert_kernel,
        grid_spec=grid_spec,
        out_shape=jax.ShapeDtypeStruct(x_rows.shape, x_rows.dtype),
        compiler_params=_cparams(("arbitrary",)),
        name="experts",
    )(block_exp, nvalid, n_used, x_rows, we_gate, we_up, we_down, tie)


def _final_kernel(x_ref, gt_ref, g4_ref, h2_ref, tie_ref, *refs, dense):
    s, t, d = x_ref.shape
    rows = s * t
    if dense:
        yr_ref, wg_ref, wu_ref, wd_ref, o_ref = refs
    else:
        yk_ref, wk_ref, wg_ref, wu_ref, wd_ref, o_ref = refs
    h2 = _unpack_chunks(_load_rows(h2_ref, rows)).astype(BF16)
    f = _dot((_silu(_dot(h2, wg_ref[...])) * _dot(h2, wu_ref[...])).astype(BF16), wd_ref[...])
    if dense:
        f = f + yr_ref[...]
    else:
        wk = wk_ref[...]
        for k in range(TOP_K):
            f = f + wk[:, k:k + 1] * _unpack_chunks(_load_rows(yk_ref, rows, lead=k))
    o_ref[...] = x_ref[...] + gt_ref[...] * _rms(f, g4_ref[0]).reshape(s, t, d)


def _final(xmid3, ada3, g4, h2_rows, routed, ws_gate, ws_up, ws_down, layer, tm, tie):
    nseq, tseq, d = xmid3.shape
    ntok = nseq * tseq
    s_blk, t_blk, tps = _tile_split(tseq, tm)
    dense = not isinstance(routed, tuple)

    def x_map(i):
        return (i // tps, i % tps, 0)

    if dense:
        routed_specs = [pl.BlockSpec((tm, d), lambda i: (i, 0))]
        routed = (routed,)
    else:
        routed_specs = [pl.BlockSpec((TOP_K, tm * ROW_SUB, ROW_LANES), lambda i: (0, i, 0)),
                        pl.BlockSpec((tm, TOP_K), lambda i: (i, 0))]
    return pl.pallas_call(
        functools.partial(_final_kernel, dense=dense),
        grid=(ntok // tm,),
        in_specs=[
            pl.BlockSpec((s_blk, t_blk, d), x_map),
            pl.BlockSpec((s_blk, 1, d), lambda i: (i // tps, 0, 5)),
            pl.BlockSpec((1, 1, d), lambda i: (layer, 0, 0)),
            pl.BlockSpec((tm * ROW_SUB, ROW_LANES), lambda i: (i, 0)),
            _TIE_SPEC,
        ] + routed_specs + [
            pl.BlockSpec((None,) + ws_gate.shape[1:], lambda i: (layer, 0, 0)),
            pl.BlockSpec((None,) + ws_up.shape[1:], lambda i: (layer, 0, 0)),
            pl.BlockSpec((None,) + ws_down.shape[1:], lambda i: (layer, 0, 0)),
        ],
        out_specs=pl.BlockSpec((s_blk, t_blk, d), x_map),
        out_shape=jax.ShapeDtypeStruct((nseq, tseq, d), F32),
        compiler_params=_cparams(("parallel",)),
        name="final",
    )(xmid3, ada3, g4, h2_rows, tie, *routed, ws_gate, ws_up, ws_down)


DENSE_EXPERTS_PER_STEP = 4


def _dense_moe_kernel(h2_ref, wt_ref, wg_ref, wu_ref, wd_ref, y_ref, x_s):
    step = pl.program_id(0)
    rows = y_ref.shape[0]
    n_e = wg_ref.shape[1]

    @pl.when(step == 0)
    def _():
        x_s[...] = _unpack_chunks(_load_rows(h2_ref, rows)).astype(BF16)
        y_ref[...] = jnp.zeros_like(y_ref)

    x = x_s[...]
    wt = wt_ref[...]
    lane = lax.broadcasted_iota(jnp.int32, wt.shape, 1)
    acts, downs = [], []
    for j in range(n_e):
        w_col = jnp.sum(jnp.where(lane == step * n_e + j, wt, 0.0), axis=1, keepdims=True)
        gate = _dot(x, wg_ref[0, j].astype(BF16))
        up = _dot(x, wu_ref[0, j].astype(BF16))
        acts.append((w_col * (_silu(gate) * up)).astype(BF16))
        downs.append(wd_ref[0, j].astype(BF16))
    y_ref[...] += _dot(jnp.concatenate(acts, axis=1), jnp.concatenate(downs, axis=0))


def _dense_moe(h2_rows, wfull_t, we_gate, we_up, we_down, layer):
    ntok, n_exp = wfull_t.shape
    d, ff = we_gate.shape[-2:]
    n_e = DENSE_EXPERTS_PER_STEP
    return pl.pallas_call(
        _dense_moe_kernel,
        grid=(n_exp // n_e,),
        in_specs=[
            pl.BlockSpec(h2_rows.shape, lambda e: (0, 0)),
            pl.BlockSpec((ntok, n_exp), lambda e: (0, 0)),
            pl.BlockSpec((1, n_e, d, ff), lambda e: (layer, e, 0, 0)),
            pl.BlockSpec((1, n_e, d, ff), lambda e: (layer, e, 0, 0)),
            pl.BlockSpec((1, n_e, ff, d), lambda e: (layer, e, 0, 0)),
        ],
        out_specs=pl.BlockSpec((ntok, d), lambda e: (0, 0)),
        out_shape=jax.ShapeDtypeStruct((ntok, d), F32),
        scratch_shapes=[pltpu.VMEM((ntok, d), BF16)],
        compiler_params=_cparams(("arbitrary",)),
        name="dense_moe",
    )(h2_rows, wfull_t, we_gate, we_up, we_down)


def _t5_bucket(dist):
    n = np.asarray(dist, dtype=np.int64)
    exact = N_BUCKETS // 2
    log_ratio = np.log(np.maximum(n, 1) / exact) / np.log(MAX_DISTANCE / exact)
    large = np.minimum(exact + (log_ratio * (N_BUCKETS - exact)).astype(np.int64), N_BUCKETS - 1)
    return np.where(n < exact, n, large).astype(np.int32)


def _bias_table(rel_bias, h0, h1, steps, dil):
    steps = np.asarray(steps)
    buckets = np.where(steps >= 0, _t5_bucket(np.maximum(steps, 0) * dil), -1)
    onehot = buckets[..., None] == np.arange(N_BUCKETS)
    table = rel_bias[:, h0:h1].T.astype(F32).reshape((h1 - h0,) + (1,) * steps.ndim + (N_BUCKETS,))
    val = jnp.sum(jnp.where(onehot[None], table, 0.0), axis=-1)
    return jnp.where((steps >= 0)[None], val, NEG_INF)


def _band_steps():
    dist = BLOCK + np.arange(BLOCK)[:, None] - np.arange(2 * BLOCK)[None, :]
    return np.where((dist >= 0) & (dist <= BLOCK), dist, -1)


def _sample_steps(n_buf, dil, t_new):
    t = np.arange(t_new)[:, None]
    delta_c = n_buf + t - np.arange(n_buf)[None, :]
    ok_c = (delta_c % dil == 0) & (delta_c // dil <= BLOCK)
    lane = np.arange(BLOCK)[None, :]
    delta_n = t - (lane - (BLOCK - t_new))
    ok_n = (lane >= BLOCK - t_new) & (delta_n >= 0) & (delta_n % dil == 0) & (delta_n // dil <= BLOCK)
    return np.where(ok_c, delta_c // dil, -1), np.where(ok_n, delta_n // dil, -1)


def _route_plan(idx, rank, counts, bm):
    n_tok = idx.shape[1]
    n_blocks = -(-(n_tok * TOP_K + N_EXPERTS * (bm - 1)) // bm)
    counts = counts.reshape(N_EXPERTS)
    padded = (counts + bm - 1) // bm * bm
    pad_end = jnp.cumsum(padded)
    pad_start = pad_end - padded
    experts = jnp.arange(N_EXPERTS, dtype=jnp.int32)
    onehot = idx[:, :, None] == experts[None, None, :]
    dest = rank + jnp.sum(jnp.where(onehot, pad_start[None, None, :], 0), axis=-1)
    starts = jnp.arange(n_blocks, dtype=jnp.int32) * bm
    block_exp = jnp.minimum(jnp.sum((starts[:, None] >= pad_end[None, :]).astype(jnp.int32), axis=1), N_EXPERTS - 1)
    hot = block_exp[:, None] == experts[None, :]
    blk_cnt = jnp.sum(jnp.where(hot, counts[None, :], 0), axis=1)
    blk_start = jnp.sum(jnp.where(hot, pad_start[None, :], 0), axis=1)
    nvalid = jnp.clip(blk_cnt - (starts - blk_start), 0, bm).astype(jnp.int32)
    n_used = (pad_end[-1] // bm).astype(jnp.int32).reshape(1)
    return dest.astype(jnp.int32), block_exp.astype(jnp.int32), nvalid, n_used, n_blocks * bm


def _moe_routed(h2_rows, idx, rank, counts, we_gate, we_up, we_down, layer, bm, tie):
    n_tok = idx.shape[1]
    dest, block_exp, nvalid, n_used, n_rows = _route_plan(idx, rank, counts, bm)
    x_sorted = _sc_scatter_rows(h2_rows.reshape(n_tok, ROW_SUB, ROW_LANES), dest, n_rows)
    y_sorted = _experts(x_sorted.reshape(n_rows * ROW_SUB, ROW_LANES), block_exp, nvalid, n_used,
                        we_gate, we_up, we_down, layer, bm, tie)
    yk = _sc_gather_rows(y_sorted.reshape(n_rows, ROW_SUB, ROW_LANES), dest)
    return yk.reshape(TOP_K, n_tok * ROW_SUB, ROW_LANES), y_sorted


def _to_time_minor(c):
    depth, nseq, n_buf = c.shape[:3]
    return jnp.transpose(c, (0, 1, 3, 4, 5, 2)).reshape(depth, nseq, -1, n_buf)


def _from_time_minor(ct, n_heads):
    lead = ct.shape[:-2]
    rows = ct.shape[-1]
    nl = len(lead)
    x = ct.reshape(lead + (2, n_heads, HEAD_DIM, rows))
    return jnp.transpose(x, tuple(range(nl)) + (nl + 3, nl, nl + 1, nl + 2))


def kernel(x_prompt, x_sample, cache_b1, cache_b2, cache_b3, cache_c, state_pool, c_prompt, c_sample, rel_bias, w_ada, b_ada, g_pre_mix, g_post_mix, g_pre_ffn, g_post_ffn, w_in, pool_w, pool_scale, w_b_up, w_c_up, sinks, w_out, w_router, router_bias, we_gate, we_up, we_down, ws_gate, ws_up, ws_down):
    batch, seq, d = x_prompt.shape
    dec_batch, dec_seq, _ = x_sample.shape
    depth = w_in.shape[0]
    tm_p = 256
    tm_s = 128
    n_dil = len(DIL_GROUPS)
    dils = tuple(dil for _, dil in DIL_GROUPS)

    n_seq_all = batch + dec_batch
    pad = -n_seq_all % 8
    c_all = jnp.concatenate([c_prompt, c_sample, jnp.zeros((pad, d), F32)], axis=0)
    ada = _ada_all(c_all, w_ada, b_ada)

    head0 = [g * DIL_HEADS for g in range(n_dil)] + [n_dil * DIL_HEADS]
    head1 = [(g + 1) * DIL_HEADS for g in range(n_dil)] + [n_dil * DIL_HEADS + SWA_HEADS]
    band = []
    for g, dil in enumerate(dils + (1,)):
        tab = _bias_table(rel_bias, head0[g], head1[g], _band_steps(), dil)
        n_h = head1[g] - head0[g]
        band.append(tab.reshape(n_h // 2, 2, BLOCK, 2 * BLOCK).transpose(0, 2, 1, 3).reshape(n_h // 2, BLOCK, 4 * BLOCK))
    caches = (cache_b1, cache_b2, cache_b3, cache_c)
    samp = []
    for g, dil in enumerate(dils + (1,)):
        n_buf = caches[g].shape[2]
        sc_steps, sn_steps = _sample_steps(n_buf, dil, dec_seq)
        rows = (head1[g] - head0[g]) * dec_seq
        samp.append((_bias_table(rel_bias, head0[g], head1[g], sc_steps, dil).reshape(rows, n_buf),
                     _bias_table(rel_bias, head0[g], head1[g], sn_steps, dil).reshape(rows, BLOCK)))

    vec3 = lambda a: a.reshape(depth, 1, -1)
    g1, g2, g3, g4 = vec3(g_pre_mix), vec3(g_post_mix), vec3(g_pre_ffn), vec3(g_post_ffn)
    ps3, rb3 = vec3(pool_scale), router_bias.reshape(depth, N_EXPERTS, 1)
    cache_t = [_to_time_minor(c) for c in caches]
    pool_halo = jnp.pad(state_pool, ((0, 0), (0, 0), (POOL_HALO - state_pool.shape[2], 0), (0, 0)))
    pw = pool_w.astype(BF16)
    wb_all, wc_all, wo_all = w_b_up.astype(BF16), w_c_up.astype(BF16), w_out.astype(BF16)
    wr_all = jnp.swapaxes(w_router, 1, 2).astype(BF16)
    wsg, wsu, wsd = ws_gate.astype(BF16), ws_up.astype(BF16), ws_down.astype(BF16)
    post_w = (g2, g3, pw, ps3, wb_all, wc_all, wo_all, wr_all, rb3)

    xp, xs = x_prompt, x_sample
    no_tie = jnp.zeros((8, BLOCK), F32)
    cache_out = [None] * (n_dil + 1)
    states_p = None
    pool_p = []
    pool_s = []
    keeps_p = tuple(min(win, seq) for win, _ in DIL_GROUPS) + (min(BLOCK, seq),)
    n_heads = (DIL_HEADS,) * n_dil + (SWA_HEADS,)
    n_kvs = (DIL_HEADS,) * n_dil + (SWA_KV_HEADS,)
    for l in range(depth):
        w_perm = w_in[l].astype(BF16)
        ada_p = ada[l, :batch].reshape(batch, 1, -1)
        ada_s = ada[l, batch:n_seq_all].reshape(dec_batch, 1, -1)
        sink_rows = jnp.broadcast_to(jnp.repeat(sinks[l].astype(F32), dec_seq)[:, None], (SWA_HEADS * dec_seq, BLOCK))

        outs = _in_proj(xp, ada_p, g1, w_perm, l, tm_p, keeps_p, True, no_tie, states_p)
        u, gl, states_p = outs[0], outs[3 + 2 * n_dil], outs[4 + 2 * n_dil:]
        attn = []
        for g in range(n_dil):
            o, lse = _band_attn(outs[1 + 2 * g], outs[2 + 2 * g], band[g], DIL_HEADS, DIL_HEADS)
            if dils[g] == 1:
                o, lse = o.reshape(batch * seq, DIL_W), lse.reshape(batch * seq, DIL_W)
            attn += [o, lse]
        oc = _band_attn(outs[1 + 2 * n_dil], outs[2 + 2 * n_dil], band[n_dil], SWA_HEADS, SWA_KV_HEADS, sinks=sinks[l])
        attn.append(oc.reshape(batch * seq, SWA_QW))
        xmid_p, h2_p, idx, rank, wk, counts, _ = _post_mixer(xp, ada_p, u, u, attn, gl, post_w, l, tm_p, False, dils,
                                                             no_tie)
        pool_p.append(u[:, seq - (POOL_HALO - 1):])

        outs = _in_proj(xs, ada_s, g1, w_perm, l, tm_s, (dec_seq,) * (n_dil + 1), False, h2_p)
        u, gl, states = outs[0], outs[2 + n_dil], outs[3 + n_dil:]
        early = max(range(n_dil + 1), key=lambda g: cache_t[g].shape[-1])
        attn_s = [None] * (n_dil + 1)

        def sample_attn(g, tie):
            res = _sample_attn(outs[1 + g], states[g], cache_t[g], cache_out[g], samp[g][0], samp[g][1], l,
                               n_heads[g], n_kvs[g], dec_seq, tie, sink_rows=sink_rows if g == n_dil else None)
            attn_s[g] = list(res[:-1])
            cache_out[g] = res[-1]

        sample_attn(early, no_tie)
        y_k, y_sorted = _moe_routed(h2_p, idx, rank, counts, we_gate, we_up, we_down, l, 512, attn_s[early][0])
        for g in range(n_dil + 1):
            if g != early:
                sample_attn(g, y_sorted)
        attn = [a for pair in attn_s for a in pair]
        xmid, h2, _, _, _, _, wfull = _post_mixer(xs, ada_s, u, pool_halo[l], attn, gl, post_w, l, tm_s, True,
                                                  (1,) * n_dil, y_sorted)
        y_r = _dense_moe(h2, wfull, we_gate, we_up, we_down, l)
        xs = _final(xmid, ada_s, g4, h2, y_r, wsg, wsu, wsd, l, tm_s, no_tie)
        xp = _final(xmid_p, ada_p, g4, h2_p, (y_k, wk), wsg, wsu, wsd, l, tm_p, xs)
        pool_s.append(jnp.concatenate([state_pool[l], u], axis=1)[:, -(POOL_HALO - 1):])

    b1_p, b2_p, b3_p, c_p = [_from_time_minor(st, n_kvs[k]) for k, st in enumerate(states_p)]
    pool_p = jnp.stack(pool_p, axis=0)
    outs_s = [_from_time_minor(co, n_kvs[g]) for g, co in enumerate(cache_out)]
    return (xp, xs, b1_p, b2_p, b3_p, c_p, pool_p, outs_s[0], outs_s[1], outs_s[2], outs_s[3], jnp.stack(pool_s, axis=0))
```
